```python
import math
import numpy as np
import jax
import jax.numpy as jnp
from jax import lax

D_MODEL = 1024
BATCH = 8
SEQ = 8192
DEPTH = 2

HEAD_DIM = 64
NSA_HEADS = 8
NSA_KV_HEADS = 2
NSA_GROUP = NSA_HEADS // NSA_KV_HEADS
CMP_BLOCK = 32
CMP_STRIDE = 16
CMP_HIDDEN = 256
SLC_BLOCK = 64
N_SELECT = 16
WINDOW = 512
NSA_Q_BLOCK = 64
N_BRANCH = 3
CONV_CHANNELS = 256
CONV_WIDTH = 31
GLA_HEADS = 4
GLA_KEY_DIM = 128
GLA_VALUE_DIM = 256
GLA_GATE_RANK = 16
GLA_TAU = 16.0
GLA_CHUNK = 64
REL_BUCKETS = 32
REL_MAX_DIST = 128
D_FF = -(-8 * D_MODEL // (3 * 256)) * 256
PLE_DIM = 256
EPS = 1e-6
NEG = -1e30
BIG = 1e30

D_MIX = NSA_HEADS * HEAD_DIM + CONV_CHANNELS + GLA_VALUE_DIM
IN_SPLITS = (
    NSA_HEADS * HEAD_DIM,
    NSA_KV_HEADS * HEAD_DIM, NSA_KV_HEADS * HEAD_DIM,
    NSA_KV_HEADS * HEAD_DIM, NSA_KV_HEADS * HEAD_DIM,
    NSA_KV_HEADS * HEAD_DIM, NSA_KV_HEADS * HEAD_DIM,
    NSA_HEADS * N_BRANCH,
    2 * CONV_CHANNELS,
    GLA_KEY_DIM, GLA_KEY_DIM, GLA_VALUE_DIM,
    GLA_GATE_RANK,
    GLA_VALUE_DIM,
)
D_IN = sum(IN_SPLITS)

kernel_name = "hymba_nsa_conformer_gla_block"


def rms_norm(x, g):
    x32 = x.astype(jnp.float32)
    y = x32 * lax.rsqrt(jnp.mean(x32 * x32, axis=-1, keepdims=True) + EPS)
    return (y * g.astype(jnp.float32)).astype(x.dtype)


def layer_norm(x, g, b):
    x32 = x.astype(jnp.float32)
    mu = jnp.mean(x32, axis=-1, keepdims=True)
    xc = x32 - mu
    y = xc * lax.rsqrt(jnp.mean(xc * xc, axis=-1, keepdims=True) + EPS)
    return (y * g.astype(jnp.float32) + b.astype(jnp.float32)).astype(x.dtype)


def t5_bucket(dist):
    n = jnp.maximum(dist, 0)
    max_exact = REL_BUCKETS // 2
    nf = jnp.maximum(n, 1).astype(jnp.float32)
    large = max_exact + (jnp.log(nf / max_exact) / math.log(REL_MAX_DIST / max_exact)
                         * (REL_BUCKETS - max_exact)).astype(jnp.int32)
    large = jnp.minimum(large, REL_BUCKETS - 1)
    return jnp.where(n < max_exact, n, large)


def masked_softmax(s, mask):
    p = jax.nn.softmax(jnp.where(mask, s, NEG), axis=-1)
    return jnp.where(mask, p, 0.0)


def compress_kv(kv, pos, w1, b1, w2):
    B, S, Hk, D = kv.shape
    r = CMP_BLOCK // CMP_STRIDE
    nsub = S // CMP_STRIDE
    ncmp = nsub - r + 1
    sub = kv.reshape(B, nsub, CMP_STRIDE, Hk, D)
    blocks = jnp.concatenate([sub[:, j:ncmp + j] for j in range(r)], axis=2)
    blocks = blocks + pos[None, None, :, None, :]
    flat = blocks.transpose(0, 1, 3, 2, 4).reshape(B, ncmp, Hk, CMP_BLOCK * D)
    return jax.nn.silu(flat @ w1 + b1) @ w2


def nsa_attention(q, kc, vc, ks, vs, kw, vw, gates, rel_bias):
    B, S = q.shape[0], q.shape[1]
    Hk, G, D = NSA_KV_HEADS, NSA_GROUP, HEAD_DIM
    ncmp = kc.shape[1]
    nsel = S // SLC_BLOCK
    n_top = min(N_SELECT, nsel)
    q = q.reshape(B, S, Hk, G, D) * (D ** -0.5)
    gates = gates.reshape(B, S, Hk, G, N_BRANCH)
    bias_h = rel_bias.T.astype(jnp.float32)
    bias_hkg = bias_h.reshape(Hk, G, REL_BUCKETS).transpose(0, 2, 1)
    cmp_end = jnp.arange(ncmp, dtype=jnp.int32) * CMP_STRIDE + (CMP_BLOCK - 1)
    rs, rc = SLC_BLOCK // CMP_STRIDE, CMP_BLOCK // CMP_STRIDE
    agg = (np.arange(nsel)[:, None, None] * rs + np.arange(rs)[None, :, None]
           - np.arange(rc)[None, None, :]).reshape(nsel, rs * rc)
    agg_valid = jnp.asarray((agg >= 0) & (agg < ncmp))
    agg_idx = jnp.asarray(np.clip(agg, 0, ncmp - 1), dtype=jnp.int32)
    ks_blk = ks.reshape(B, nsel, SLC_BLOCK, Hk, D).transpose(0, 3, 1, 2, 4)
    vs_blk = vs.reshape(B, nsel, SLC_BLOCK, Hk, D).transpose(0, 3, 1, 2, 4)
    kw_pad = jnp.pad(kw, ((0, 0), (WINDOW, 0), (0, 0), (0, 0)))
    vw_pad = jnp.pad(vw, ((0, 0), (WINDOW, 0), (0, 0), (0, 0)))
    b_idx = jnp.arange(B)[:, None, None, None]
    h_idx = jnp.arange(Hk)[None, :, None, None]
    blk_ids = jnp.arange(nsel, dtype=jnp.int32)
    offs = jnp.arange(SLC_BLOCK, dtype=jnp.int32)
    win_offs = jnp.arange(NSA_Q_BLOCK + WINDOW, dtype=jnp.int32)
    Q = NSA_Q_BLOCK

    def head_bias(dist):
        return bias_h[:, t5_bucket(dist)].reshape(Hk, G, *dist.shape)

    def block_fn(c):
        q0 = c * Q
        t = q0 + jnp.arange(Q, dtype=jnp.int32)
        qc = lax.dynamic_slice_in_dim(q, q0, Q, axis=1)
        gc = lax.dynamic_slice_in_dim(gates, q0, Q, axis=1)
        dist_c = t[:, None] - cmp_end[None, :]
        s_c = jnp.einsum('bqhgd,bnhd->bhgqn', qc, kc).astype(jnp.float32) + head_bias(dist_c)
        p_c = masked_softmax(s_c, dist_c >= 0)
        o_c = jnp.einsum('bhgqn,bnhd->bqhgd', p_c.astype(vc.dtype), vc)
        imp = p_c.sum(axis=2)
        imp = jnp.where(agg_valid, jnp.take(imp, agg_idx, axis=-1), 0.0).sum(-1)
        cur = t // SLC_BLOCK
        forced = ((blk_ids[None, :] == 0) | (blk_ids[None, :] == cur[:, None])
                  | (blk_ids[None, :] == cur[:, None] - 1))
        causal_blk = blk_ids[None, :] * SLC_BLOCK <= t[:, None]
        score = jnp.where(causal_blk, jnp.where(forced, BIG, imp), NEG)
        top_val, top_idx = lax.top_k(score, n_top)
        sel_ok = top_val > 0.5 * NEG
        k_sel = ks_blk[b_idx, h_idx, top_idx]
        v_sel = vs_blk[b_idx, h_idx, top_idx]
        key_pos = top_idx[..., None] * SLC_BLOCK + offs
        dist_s = t[None, None, :, None, None] - key_pos
        s_s = jnp.einsum('bqhgd,bhqnkd->bhgqnk', qc, k_sel).astype(jnp.float32)
        bias_s = bias_hkg[h_idx[..., None], t5_bucket(dist_s)]
        s_s = s_s + jnp.moveaxis(bias_s, -1, 2)
        mask_s = (sel_ok[..., None] & (dist_s >= 0))[:, :, None]
        p_s = masked_softmax(s_s.reshape(B, Hk, G, Q, n_top * SLC_BLOCK),
                             mask_s.reshape(B, Hk, 1, Q, n_top * SLC_BLOCK)).reshape(s_s.shape)
        o_s = jnp.einsum('bhgqnk,bhqnkd->bqhgd', p_s.astype(v_sel.dtype), v_sel)
        kwc = lax.dynamic_slice_in_dim(kw_pad, q0, Q + WINDOW, axis=1)
        vwc = lax.dynamic_slice_in_dim(vw_pad, q0, Q + WINDOW, axis=1)
        key_pos_w = q0 - WINDOW + win_offs
        dist_w = t[:, None] - key_pos_w[None, :]
        mask_w = (dist_w >= 0) & (dist_w < WINDOW) & (key_pos_w[None, :] >= 0)
        s_w = jnp.einsum('bqhgd,bkhd->bhgqk', qc, kwc).astype(jnp.float32) + head_bias(dist_w)
        p_w = masked_softmax(s_w, mask_w)
        o_w = jnp.einsum('bhgqk,bkhd->bqhgd', p_w.astype(vwc.dtype), vwc)
        return gc[..., 0:1] * o_c + gc[..., 1:2] * o_s + gc[..., 2:3] * o_w

    out = lax.map(block_fn, jnp.arange(S // Q, dtype=jnp.int32))
    return out.transpose(1, 0, 2, 3, 4, 5).reshape(B, S, NSA_HEADS * HEAD_DIM)


def conformer_conv(u, w_dw, b_dw, ln_g, ln_b):
    a, g = jnp.split(u, 2, axis=-1)
    h = a * jax.nn.sigmoid(g)
    h = lax.conv_general_dilated(h, w_dw[:, None, :], window_strides=(1,),
                                 padding=[(CONV_WIDTH - 1, 0)],
                                 dimension_numbers=('NWC', 'WIO', 'NWC'),
                                 feature_group_count=CONV_CHANNELS) + b_dw
    return jax.nn.silu(layer_norm(h, ln_g, ln_b))


def gla_chunked(q, k, v, log_a):
    B, S, H, dk = q.shape
    dv = v.shape[-1]
    C = GLA_CHUNK
    N = S // C

    def chunks(a):
        return a.astype(jnp.float32).reshape(B, N, C, H, a.shape[-1]).transpose(0, 3, 1, 2, 4)

    q, k, v, la = chunks(q), chunks(k), chunks(v), chunks(log_a)
    b = jnp.cumsum(la, axis=3)
    b_last = b[:, :, :, -1:]
    q_t = q * (dk ** -0.5) * jnp.exp(b)
    k_t = k * jnp.exp(-b)
    k_d = k * jnp.exp(b_last - b)
    causal = jnp.tril(jnp.ones((C, C), dtype=bool))
    attn = jnp.where(causal, jnp.einsum('bhncd,bhnsd->bhncs', q_t, k_t), 0.0)
    o_intra = jnp.einsum('bhncs,bhnsv->bhncv', attn, v)

    def step(state, xs):
        qn, kn, vn, dn = xs
        o = jnp.einsum('bhcd,bhdv->bhcv', qn, state)
        state = dn[..., None] * state + jnp.einsum('bhcd,bhcv->bhdv', kn, vn)
        return state, o

    xs = (jnp.moveaxis(q_t, 2, 0), jnp.moveaxis(k_d, 2, 0), jnp.moveaxis(v, 2, 0),
          jnp.moveaxis(jnp.exp(b_last[:, :, :, 0]), 2, 0))
    _, o_inter = lax.scan(step, jnp.zeros((B, H, dk, dv), jnp.float32), xs)
    o = o_intra + jnp.moveaxis(o_inter, 0, 2)
    return o.transpose(0, 2, 3, 1, 4).reshape(B, S, H, dv)


def setup_inputs(seed: int = 0) -> dict:
    key = jax.random.key(seed)
    ks = jax.random.split(key, 32)
    L = DEPTH

    def nrm(k, shape, scale):
        return jax.random.normal(k, shape, jnp.float32) * scale

    def gain(k, shape):
        return 1.0 + 0.05 * jax.random.normal(k, shape, jnp.float32)

    return {
        "x": nrm(ks[0], (BATCH, SEQ, D_MODEL), 1.0),
        "p": nrm(ks[1], (DEPTH, BATCH, SEQ, PLE_DIM), 1.0),
        "rel_bias": nrm(ks[2], (REL_BUCKETS, NSA_HEADS), 0.5),
        "mix_norm_g": gain(ks[3], (L, D_MODEL)),
        "w_in": nrm(ks[4], (L, D_MODEL, D_IN), D_MODEL ** -0.5),
        "w_out": nrm(ks[5], (L, D_MIX, D_MODEL), D_MIX ** -0.5),
        "cmp_pos_k": nrm(ks[6], (L, CMP_BLOCK, HEAD_DIM), 0.1),
        "cmp_w1_k": nrm(ks[7], (L, CMP_BLOCK * HEAD_DIM, CMP_HIDDEN), (CMP_BLOCK * HEAD_DIM) ** -0.5),
        "cmp_b1_k": nrm(ks[8], (L, CMP_HIDDEN), 0.02),
        "cmp_w2_k": nrm(ks[9], (L, CMP_HIDDEN, HEAD_DIM), CMP_HIDDEN ** -0.5),
        "cmp_pos_v": nrm(ks[10], (L, CMP_BLOCK, HEAD_DIM), 0.1),
        "cmp_w1_v": nrm(ks[11], (L, CMP_BLOCK * HEAD_DIM, CMP_HIDDEN), (CMP_BLOCK * HEAD_DIM) ** -0.5),
        "cmp_b1_v": nrm(ks[12], (L, CMP_HIDDEN), 0.02),
        "cmp_w2_v": nrm(ks[13], (L, CMP_HIDDEN, HEAD_DIM), CMP_HIDDEN ** -0.5),
        "conv_w": nrm(ks[14], (L, CONV_WIDTH, CONV_CHANNELS), CONV_WIDTH ** -0.5),
        "conv_b": nrm(ks[15], (L, CONV_CHANNELS), 0.02),
        "conv_ln_g": gain(ks[16], (L, CONV_CHANNELS)),
        "conv_ln_b": nrm(ks[17], (L, CONV_CHANNELS), 0.02),
        "gla_w_alpha": nrm(ks[18], (L, GLA_GATE_RANK, GLA_KEY_DIM), GLA_GATE_RANK ** -0.5),
        "gla_b_alpha": nrm(ks[19], (L, GLA_KEY_DIM), 0.02),
        "gla_norm_g": gain(ks[20], (L, GLA_VALUE_DIM)),
        "ffn_norm_g": gain(ks[21], (L, D_MODEL)),
        "ffn_w_gate": nrm(ks[22], (L, D_MODEL, D_FF), D_MODEL ** -0.5),
        "ffn_w_up": nrm(ks[23], (L, D_MODEL, D_FF), D_MODEL ** -0.5),
        "ffn_w_down": nrm(ks[24], (L, D_FF, D_MODEL), D_FF ** -0.5),
        "ple_norm_g": gain(ks[25], (L, D_MODEL)),
        "ple_w_gate": nrm(ks[26], (L, D_MODEL, D_MODEL), D_MODEL ** -0.5),
        "ple_w_proj": nrm(ks[27], (L, PLE_DIM, D_MODEL), PLE_DIM ** -0.5),
        "final_norm_g": gain(ks[28], (D_MODEL,)),
    }


def reference(x, p, rel_bias, mix_norm_g, w_in, w_out,
              cmp_pos_k, cmp_w1_k, cmp_b1_k, cmp_w2_k,
              cmp_pos_v, cmp_w1_v, cmp_b1_v, cmp_w2_v,
              conv_w, conv_b, conv_ln_g, conv_ln_b,
              gla_w_alpha, gla_b_alpha, gla_norm_g,
              ffn_norm_g, ffn_w_gate, ffn_w_up, ffn_w_down,
              ple_norm_g, ple_w_gate, ple_w_proj, final_norm_g):
    B, S, _ = x.shape
    split_at = np.cumsum(IN_SPLITS)[:-1].tolist()

    def heads(a, n):
        return a.reshape(B, S, n, -1)

    for i in range(DEPTH):
        h = rms_norm(x, mix_norm_g[i])
        z = h @ w_in[i]
        (nq, kc, vc, ks_, vs_, kw, vw, ng, cu, gq, gk, gv, ga, gr) = jnp.split(z, split_at, axis=-1)
        kc = compress_kv(heads(kc, NSA_KV_HEADS), cmp_pos_k[i], cmp_w1_k[i], cmp_b1_k[i], cmp_w2_k[i])
        vc = compress_kv(heads(vc, NSA_KV_HEADS), cmp_pos_v[i], cmp_w1_v[i], cmp_b1_v[i], cmp_w2_v[i])
        y_nsa = nsa_attention(heads(nq, NSA_HEADS), kc, vc,
                              heads(ks_, NSA_KV_HEADS), heads(vs_, NSA_KV_HEADS),
                              heads(kw, NSA_KV_HEADS), heads(vw, NSA_KV_HEADS),
                              jax.nn.sigmoid(ng).reshape(B, S, NSA_HEADS, N_BRANCH), rel_bias)
        y_conv = conformer_conv(cu, conv_w[i], conv_b[i], conv_ln_g[i], conv_ln_b[i])
        log_a = jax.nn.log_sigmoid((ga @ gla_w_alpha[i] + gla_b_alpha[i]).astype(jnp.float32)) / GLA_TAU
        o = gla_chunked(heads(gq, GLA_HEADS), heads(gk, GLA_HEADS), heads(gv, GLA_HEADS),
                        heads(log_a, GLA_HEADS))
        o = rms_norm(o, gla_norm_g[i].reshape(GLA_HEADS, -1)).reshape(B, S, GLA_VALUE_DIM)
        y_gla = (o * jax.nn.silu(gr)).astype(x.dtype)
        x = x + jnp.concatenate([y_nsa, y_conv, y_gla], axis=-1) @ w_out[i]
        h = rms_norm(x, ffn_norm_g[i])
        x = x + (jax.nn.silu(h @ ffn_w_gate[i]) * (h @ ffn_w_up[i])) @ ffn_w_down[i]
        gate = jax.nn.sigmoid(rms_norm(x, ple_norm_g[i]) @ ple_w_gate[i])
        x = x + (p[i] @ ple_w_proj[i]) * gate
    return rms_norm(x, final_norm_g)
```

```python
import functools
import math

import numpy as np
import jax
import jax.numpy as jnp
from jax import lax
from jax.experimental import pallas as pl
from jax.experimental.pallas import tpu as pltpu

F32 = jnp.float32
MXU_DTYPE = jnp.bfloat16

HEAD_DIM = 64
NSA_HEADS = 8
NSA_KV_HEADS = 2
NSA_GROUP = NSA_HEADS // NSA_KV_HEADS
CMP_BLOCK = 32
CMP_STRIDE = 16
CMP_HIDDEN = 256
SLC_BLOCK = 64
N_SELECT = 16
WINDOW = 512
N_BRANCH = 3
CONV_CHANNELS = 256
CONV_WIDTH = 31
GLA_HEADS = 4
GLA_KEY_DIM = 128
GLA_VALUE_DIM = 256
GLA_GATE_RANK = 16
GLA_TAU = 16.0
GLA_CHUNK = 64
REL_BUCKETS = 32
REL_MAX_DIST = 128
PLE_DIM = 256
EPS = 1e-6
NEG_BIG = -1e30

LANES = 128
VMEM_LIMIT_BYTES = 56 * 1024 * 1024

TM_PROJ = 512
TQ = 256
ROWS = NSA_GROUP * TQ
NEAR_W = 2 * TQ // CMP_STRIDE
TS_CONV = 512
CONV_HALO = 32
TS_GLA = 1024
FF_CHUNKS = 2

C_Q = NSA_HEADS * LANES
C_KV4 = 4 * NSA_KV_HEADS * HEAD_DIM
C_KVC = 2 * NSA_KV_HEADS * HEAD_DIM
C_CU = 2 * CONV_CHANNELS
C_GLA = 2 * GLA_KEY_DIM + 2 * GLA_VALUE_DIM
C_MISC = NSA_KV_HEADS * LANES
GA_LANE0 = 16
FAR_LANE = HEAD_DIM


def _split_hi_lo(x):
    hi = x.astype(MXU_DTYPE)
    lo = (x - hi.astype(F32)).astype(MXU_DTYPE)
    return hi, lo


def _split3(x):
    hi = x.astype(MXU_DTYPE)
    r = x - hi.astype(F32)
    mid = r.astype(MXU_DTYPE)
    lo = (r - mid.astype(F32)).astype(MXU_DTYPE)
    return hi, mid, lo


def _dot(a, b):
    return jnp.dot(a, b, preferred_element_type=F32)


def _dot_nt(a, b):
    return lax.dot_general(a, b, (((1,), (1,)), ((), ())), preferred_element_type=F32)


def _dot_tn(a, b):
    return lax.dot_general(a, b, (((0,), (0,)), ((), ())), preferred_element_type=F32)


def _rms(x, g):
    return x * lax.rsqrt(jnp.mean(x * x, axis=-1, keepdims=True) + EPS) * g


def _sigmoid(x):
    return 1.0 / (1.0 + jnp.exp(-x))


def _silu(x):
    return x * _sigmoid(x)


def _params(*sem):
    return pltpu.CompilerParams(dimension_semantics=sem, vmem_limit_bytes=VMEM_LIMIT_BYTES)


def _full(shape):
    nd = len(shape)
    return pl.BlockSpec(shape, lambda *_: (0,) * nd)


def _t5_bucket_np(dist):
    n = np.maximum(dist, 0)
    max_exact = REL_BUCKETS // 2
    nf = np.maximum(n, 1).astype(np.float32)
    large = max_exact + (np.log(nf / np.float32(max_exact)) / np.float32(math.log(REL_MAX_DIST / max_exact))
                         * np.float32(REL_BUCKETS - max_exact)).astype(np.int32)
    large = np.minimum(large, REL_BUCKETS - 1)
    return np.where(n < max_exact, n, large).astype(np.int32)


def _in_proj_columns(in_splits):
    offs = np.concatenate([[0], np.cumsum(in_splits)])
    (o_nq, o_kc, o_vc, o_ks, o_vs, o_kw, o_vw, o_ng, o_cu, o_gq, o_gk, o_gv, o_ga, o_gr) = offs[:-1]
    src, scale = [], []

    def put(cols, s=1.0):
        src.extend(cols)
        scale.extend([s] * len(cols))

    for hd in range(NSA_HEADS):
        put(list(range(o_nq + hd * HEAD_DIM, o_nq + (hd + 1) * HEAD_DIM)), HEAD_DIM ** -0.5)
        put([-1] * (LANES - HEAD_DIM))
    kvw = NSA_KV_HEADS * HEAD_DIM
    for o in (o_ks, o_vs, o_kw, o_vw, o_kc, o_vc):
        put(list(range(o, o + kvw)))
    put(list(range(o_cu, o_cu + C_CU)))
    put(list(range(o_gq, o_gq + GLA_KEY_DIM)))
    put(list(range(o_gk, o_gk + GLA_KEY_DIM)))
    put(list(range(o_gv, o_gv + GLA_VALUE_DIM)))
    put(list(range(o_gr, o_gr + GLA_VALUE_DIM)))
    per = NSA_GROUP * N_BRANCH
    for h in range(NSA_KV_HEADS):
        slab = [-1] * LANES
        slab[:per] = list(range(o_ng + h * per, o_ng + (h + 1) * per))
        if h == 0:
            slab[GA_LANE0:GA_LANE0 + GLA_GATE_RANK] = list(range(o_ga, o_ga + GLA_GATE_RANK))
        put(slab)
    return np.asarray(src, np.int32), np.asarray(scale, np.float32)


IN_SPLITS = (
    NSA_HEADS * HEAD_DIM,
    NSA_KV_HEADS * HEAD_DIM, NSA_KV_HEADS * HEAD_DIM,
    NSA_KV_HEADS * HEAD_DIM, NSA_KV_HEADS * HEAD_DIM,
    NSA_KV_HEADS * HEAD_DIM, NSA_KV_HEADS * HEAD_DIM,
    NSA_HEADS * N_BRANCH,
    2 * CONV_CHANNELS,
    GLA_KEY_DIM, GLA_KEY_DIM, GLA_VALUE_DIM,
    GLA_GATE_RANK,
    GLA_VALUE_DIM,
)
_W_SRC, _W_SCALE = _in_proj_columns(IN_SPLITS)
C_ALL = C_Q + C_KV4 + C_KVC + C_CU + C_GLA + C_MISC
assert _W_SRC.shape[0] == C_ALL


def _nsa_bias_tables(rel_bias):
    i = np.arange(TQ)[:, None]
    j = np.arange(TQ)[None, :]
    far = rel_bias[REL_BUCKETS - 1]
    tab = rel_bias - far[None, :]

    def rel(dist):
        vals = jnp.take(tab, jnp.asarray(_t5_bucket_np(dist)), axis=0)
        vals = jnp.where(jnp.asarray(dist >= 0)[..., None], vals, NEG_BIG)
        vals = vals.transpose(2, 0, 1).reshape(NSA_KV_HEADS, ROWS, dist.shape[1])
        return vals

    tri = jnp.asarray(np.where(j <= i, NEG_BIG, 0.0).astype(np.float32))
    tri = jnp.broadcast_to(jnp.tile(tri, (NSA_GROUP, 1))[None], (NSA_KV_HEADS, ROWS, TQ))
    bm = jnp.concatenate([tri, rel(i + TQ - j), rel(i - j)], axis=-1)

    m = np.arange(NEAR_W)[None, :]
    near = rel(i - CMP_STRIDE * m + (NEAR_W // 2) * CMP_STRIDE - (CMP_BLOCK - 1))
    hi, lo = _split_hi_lo(near)
    pad = jnp.zeros((NSA_KV_HEADS, ROWS, LANES - 2 * NEAR_W - 1), MXU_DTYPE)
    big = jnp.full((NSA_KV_HEADS, ROWS, 1), NEG_BIG, MXU_DTYPE)
    ac = jnp.concatenate([hi, lo, big, pad], axis=-1)
    return bm, ac


def _agg_matrix(ncp, nbp):
    rs, rc = SLC_BLOCK // CMP_STRIDE, CMP_BLOCK // CMP_STRIDE
    agg = np.zeros((ncp, nbp), np.float32)
    ncmp = ncp - rc + 1
    for jb in range(nbp):
        for mm in range(rs):
            for nn in range(rc):
                idx = jb * rs + mm - nn
                if 0 <= idx < ncmp:
                    agg[idx, jb] += 1.0
    return agg


def _in_proj_kernel(x_ref, g_ref, w_ref, qc_ref, q_ref, kv4_ref, kvc_ref, cu_ref, gla_ref, misc_ref):
    h = _rms(x_ref[...], g_ref[...]).astype(MXU_DTYPE)
    o = 0
    zq = _dot(h, w_ref[:, o:o + C_Q]) + qc_ref[...]
    for hd in range(NSA_HEADS):
        q_ref[hd] = zq[:, hd * LANES:(hd + 1) * LANES].astype(q_ref.dtype)
    o += C_Q
    kv4_ref[...] = _dot(h, w_ref[:, o:o + C_KV4]).astype(kv4_ref.dtype)
    o += C_KV4
    kvc_ref[...] = _dot(h, w_ref[:, o:o + C_KVC])
    o += C_KVC
    cu_ref[...] = _dot(h, w_ref[:, o:o + C_CU])
    o += C_CU
    gla_ref[...] = _dot(h, w_ref[:, o:o + C_GLA])
    o += C_GLA
    zm = _dot(h, w_ref[:, o:o + C_MISC])
    for s in range(NSA_KV_HEADS):
        misc_ref[s] = zm[:, s * LANES:(s + 1) * LANES]


def _in_proj(x2d, g, w_all, qconst):
    t, d = x2d.shape
    tm = min(TM_PROJ, t)
    row = lambda i: (i, 0)
    return pl.pallas_call(
        _in_proj_kernel,
        grid=(t // tm,),
        in_specs=[pl.BlockSpec((tm, d), row), _full((1, d)), _full((d, C_ALL)), _full((1, C_Q))],
        out_specs=[
            pl.BlockSpec((NSA_HEADS, tm, LANES), lambda i: (0, i, 0)),
            pl.BlockSpec((tm, C_KV4), row),
            pl.BlockSpec((tm, C_KVC), row),
            pl.BlockSpec((tm, C_CU), row),
            pl.BlockSpec((tm, C_GLA), row),
            pl.BlockSpec((NSA_KV_HEADS, tm, LANES), lambda i: (0, i, 0)),
        ],
        out_shape=[
            jax.ShapeDtypeStruct((NSA_HEADS, t, LANES), MXU_DTYPE),
            jax.ShapeDtypeStruct((t, C_KV4), MXU_DTYPE),
            jax.ShapeDtypeStruct((t, C_KVC), F32),
            jax.ShapeDtypeStruct((t, C_CU), F32),
            jax.ShapeDtypeStruct((t, C_GLA), F32),
            jax.ShapeDtypeStruct((NSA_KV_HEADS, t, LANES), F32),
        ],
        compiler_params=_params("parallel"),
        name="in_proj",
    )(x2d, g, w_all, qconst)


def _compress_kernel(sub_ref, pos_ref, w1_ref, b1_ref, w2_ref, o_ref):
    sub = sub_ref[...]
    nc = sub.shape[0]
    half = sub.shape[1]
    top = _dot((sub + pos_ref[:, :half]).astype(MXU_DTYPE), w1_ref[:half, :])
    bot = _dot((sub + pos_ref[:, half:]).astype(MXU_DTYPE), w1_ref[half:, :])
    hid = _silu(top + pltpu.roll(bot, nc - 1, 0) + b1_ref[...])
    o_ref[...] = _dot(hid.astype(MXU_DTYPE), w2_ref[...]).astype(o_ref.dtype)


def _compress(sub, pos, w1, b1, w2):
    _, b, hk, nc, half = sub.shape
    kv = lambda s, bi, h: (s, 0, 0)
    return pl.pallas_call(
        _compress_kernel,
        grid=(2, b, hk),
        in_specs=[
            pl.BlockSpec((None, None, None, nc, half), lambda s, bi, h: (s, bi, h, 0, 0)),
            pl.BlockSpec((None, 1, 2 * half), kv),
            pl.BlockSpec((None, 2 * half, CMP_HIDDEN), kv),
            pl.BlockSpec((None, 1, CMP_HIDDEN), kv),
            pl.BlockSpec((None, CMP_HIDDEN, HEAD_DIM), kv),
        ],
        out_specs=pl.BlockSpec((None, None, None, nc, HEAD_DIM), lambda s, bi, h: (s, bi, h, 0, 0)),
        out_shape=jax.ShapeDtypeStruct((2, b, hk, nc, HEAD_DIM), MXU_DTYPE),
        compiler_params=_params("parallel", "parallel", "parallel"),
        name="compress_kv",
    )(sub, pos, w1, b1, w2)


def _nsa_kernel(q_ref, ks_ref, vs_ref, kw_ref, vw_ref, kc_ref, vc_ref, gl_ref, ac_ref, bm_ref, agg_ref,
                o_ref, kcx_scr, m_scr, acc_scr):
    c = pl.program_id(2)
    nc = kc_ref.shape[1]
    nb = agg_ref.shape[1]
    q = q_ref[...].reshape(ROWS, LANES)

    rowi = lax.broadcasted_iota(jnp.int32, (LANES, nc), 0)
    coli = lax.broadcasted_iota(jnp.int32, (LANES, nc), 1)
    near0 = c * (TQ // CMP_STRIDE) - NEAR_W // 2
    shift = (coli == near0 + (rowi & (NEAR_W - 1))) & (rowi < 2 * NEAR_W)
    future = (rowi == 2 * NEAR_W) & (coli >= near0 + NEAR_W)
    kcx_scr[0:LANES, :] = kc_ref[...]
    kcx_scr[LANES:2 * LANES, :] = jnp.where(shift | future, 1.0, 0.0).astype(MXU_DTYPE)
    s_c = _dot(jnp.concatenate([q, ac_ref[...]], axis=1), kcx_scr[...])
    m_c = jnp.maximum(jnp.max(s_c, axis=1, keepdims=True), 0.1 * NEG_BIG)
    e_c = jnp.exp(s_c - m_c)
    l_c = jnp.sum(e_c, axis=1, keepdims=True)
    p_c = e_c * (1.0 / jnp.maximum(l_c, 1e-30))
    acc_c = _dot(p_c.astype(MXU_DTYPE), vc_ref[...])

    imp = p_c[0:TQ] + p_c[TQ:2 * TQ] + p_c[2 * TQ:3 * TQ] + p_c[3 * TQ:4 * TQ]
    agg = agg_ref[...]
    i_hi, i_mid, i_lo = _split3(imp)
    imps = _dot(i_hi, agg) + _dot(i_mid, agg) + _dot(i_lo, agg)
    t_q = c * TQ + lax.broadcasted_iota(jnp.int32, (TQ, nb), 0)
    blk = lax.broadcasted_iota(jnp.int32, (TQ, nb), 1)
    cur = lax.shift_right_logical(t_q, int(math.log2(SLC_BLOCK)))
    causal = blk <= cur
    forced = (blk == 0) | (blk == cur) | (blk == cur - 1)
    neg_inf = -jnp.inf
    cand = jnp.where(causal & jnp.logical_not(forced), imps, neg_inf)
    sel = forced & causal
    blk_f = blk.astype(F32)
    for _ in range(N_SELECT - 3):
        best = jnp.max(cand, axis=1, keepdims=True)
        first = jnp.min(jnp.where(cand == best, blk_f, float(nb)), axis=1, keepdims=True)
        hit = (blk_f == first) & (best > neg_inf)
        sel = sel | hit
        cand = jnp.where(hit, neg_inf, cand)
    selneg = jnp.where(sel, 0.0, NEG_BIG).astype(MXU_DTYPE)
    qs = jnp.concatenate([q, jnp.concatenate([selneg] * NSA_GROUP, axis=0)], axis=1)

    def flash_init():
        m_scr[...] = jnp.full(m_scr.shape, NEG_BIG, F32)
        acc_scr[...] = jnp.zeros(acc_scr.shape, F32)

    def flash_step(s, v):
        m_prev = m_scr[...]
        m_new = jnp.maximum(m_prev, jnp.max(s, axis=1, keepdims=True))
        p = jnp.exp(s - jnp.concatenate([m_new] * (TQ // LANES), axis=1))
        acc_scr[...] = jnp.exp(m_prev - m_new) * acc_scr[...] + _dot(p.astype(MXU_DTYPE), v)
        m_scr[...] = m_new

    def flash_out():
        acc = acc_scr[...]
        return acc * (1.0 / acc[:, HEAD_DIM:HEAD_DIM + 1])

    def key_tile(ref, kt):
        return ref[:, pl.ds(pl.multiple_of(kt * TQ, TQ), TQ)]

    def val_tile(ref, kt):
        return ref[pl.ds(pl.multiple_of(kt * TQ, TQ), TQ), :]

    flash_init()

    def far_body(kt, carry):
        flash_step(_dot(qs, key_tile(ks_ref, kt)), val_tile(vs_ref, kt))
        return carry

    lax.fori_loop(0, jnp.maximum(c - 1, 0), far_body, 0)

    @pl.when(c >= 1)
    def _():
        flash_step(_dot(qs, key_tile(ks_ref, c - 1)) + bm_ref[:, TQ:2 * TQ], val_tile(vs_ref, c - 1))

    flash_step(_dot(qs, key_tile(ks_ref, c)) + bm_ref[:, 2 * TQ:3 * TQ], val_tile(vs_ref, c))
    o_s = flash_out()

    flash_init()

    @pl.when(c >= 2)
    def _():
        flash_step(_dot(q, key_tile(kw_ref, c - 2)) + bm_ref[:, 0:TQ], val_tile(vw_ref, c - 2))

    @pl.when(c >= 1)
    def _():
        flash_step(_dot(q, key_tile(kw_ref, c - 1)) + bm_ref[:, TQ:2 * TQ], val_tile(vw_ref, c - 1))

    flash_step(_dot(q, key_tile(kw_ref, c)) + bm_ref[:, 2 * TQ:3 * TQ], val_tile(vw_ref, c))
    o_w = flash_out()

    gates = _sigmoid(gl_ref[...])
    lane = lax.broadcasted_iota(jnp.int32, (TQ, LANES), 1)
    outs = []
    for g in range(NSA_GROUP):
        r = slice(g * TQ, (g + 1) * TQ)
        gc = gates[:, g * N_BRANCH + 0:g * N_BRANCH + 1]
        gs = gates[:, g * N_BRANCH + 1:g * N_BRANCH + 2]
        gw = gates[:, g * N_BRANCH + 2:g * N_BRANCH + 3]
        outs.append(gc * acc_c[r] + gs * o_s[r] + gw * o_w[r])
    for pair in range(NSA_GROUP // 2):
        both = jnp.where(lane < HEAD_DIM, outs[2 * pair], pltpu.roll(outs[2 * pair + 1], HEAD_DIM, 1))
        o_ref[:, pair * LANES:(pair + 1) * LANES] = both.astype(o_ref.dtype)


def _nsa(q8, ks, vs, kw, vw, kc, vc, gl, ac, bm, agg):
    _, b, s, _ = q8.shape
    nc = kc.shape[-1]
    nb = agg.shape[1]
    kvmap = lambda bi, h, c: (bi, h, 0, 0)
    hmap = lambda bi, h, c: (h, 0, 0)
    return pl.pallas_call(
        _nsa_kernel,
        grid=(b, NSA_KV_HEADS, s // TQ),
        in_specs=[
            pl.BlockSpec((NSA_GROUP, None, TQ, LANES), lambda bi, h, c: (h, bi, c, 0)),
            pl.BlockSpec((None, None, 2 * LANES, s), kvmap),
            pl.BlockSpec((None, None, s, LANES), kvmap),
            pl.BlockSpec((None, None, LANES, s), kvmap),
            pl.BlockSpec((None, None, s, LANES), kvmap),
            pl.BlockSpec((None, None, LANES, nc), kvmap),
            pl.BlockSpec((None, None, nc, LANES), kvmap),
            pl.BlockSpec((None, None, TQ, LANES), lambda bi, h, c: (h, bi, c, 0)),
            pl.BlockSpec((None, ROWS, LANES), hmap),
            pl.BlockSpec((None, ROWS, 3 * TQ), hmap),
            _full((nc, nb)),
        ],
        out_specs=pl.BlockSpec((None, TQ, NSA_GROUP * HEAD_DIM), lambda bi, h, c: (bi, c, h)),
        out_shape=jax.ShapeDtypeStruct((b, s, NSA_HEADS * HEAD_DIM), MXU_DTYPE),
        scratch_shapes=[
            pltpu.VMEM((2 * LANES, nc), MXU_DTYPE),
            pltpu.VMEM((ROWS, LANES), F32),
            pltpu.VMEM((ROWS, LANES), F32),
        ],
        compiler_params=_params("parallel", "parallel", "arbitrary"),
        name="nsa_attention",
    )(q8, ks, vs, kw, vw, kc, vc, gl, ac, bm, agg)


def _conv_kernel(cu_ref, halo_ref, w_ref, b_ref, g_ref, bb_ref, o_ref, hh_scr):
    ts = cu_ref.shape[0]
    ch = CONV_CHANNELS

    def glu(u):
        return u[:, :ch] * _sigmoid(u[:, ch:])

    first = pl.program_id(1) == 0
    hh_scr[0:CONV_HALO, :] = jnp.where(first, 0.0, glu(halo_ref[...]))
    hh_scr[CONV_HALO:CONV_HALO + ts, :] = glu(cu_ref[...])
    acc = jnp.broadcast_to(b_ref[...], (ts, ch))
    for w in range(CONV_WIDTH):
        acc = acc + hh_scr[pl.ds(CONV_HALO - (CONV_WIDTH - 1) + w, ts), :] * w_ref[w:w + 1, :]
    mu = jnp.mean(acc, axis=-1, keepdims=True)
    xc = acc - mu
    y = xc * lax.rsqrt(jnp.mean(xc * xc, axis=-1, keepdims=True) + EPS) * g_ref[...] + bb_ref[...]
    o_ref[...] = _silu(y).astype(o_ref.dtype)


def _conv(cu, w, bias, ln_g, ln_b):
    b, s, _ = cu.shape
    ts = min(TS_CONV, s)
    per = ts // CONV_HALO
    return pl.pallas_call(
        _conv_kernel,
        grid=(b, s // ts),
        in_specs=[
            pl.BlockSpec((None, ts, C_CU), lambda bi, i: (bi, i, 0)),
            pl.BlockSpec((None, CONV_HALO, C_CU), lambda bi, i: (bi, jnp.maximum(i * per - 1, 0), 0)),
            _full((CONV_HALO, CONV_CHANNELS)), _full((1, CONV_CHANNELS)),
            _full((1, CONV_CHANNELS)), _full((1, CONV_CHANNELS)),
        ],
        out_specs=pl.BlockSpec((None, ts, CONV_CHANNELS), lambda bi, i: (bi, i, 0)),
        out_shape=jax.ShapeDtypeStruct((b, s, CONV_CHANNELS), MXU_DTYPE),
        scratch_shapes=[pltpu.VMEM((CONV_HALO + ts, CONV_CHANNELS), F32)],
        compiler_params=_params("parallel", "parallel"),
        name="conformer_conv",
    )(cu, cu, w, bias, ln_g, ln_b)


def _gla_kernel(q_ref, k_ref, v_ref, r_ref, ga_ref, wa_ref, ba_ref, g_ref, tri_ref, hm_ref, vm_ref, bmk_ref,
                gm_ref, o_ref, st_scr, sb_scr, b_scr, u_scr):
    ts = q_ref.shape[0]
    nch = ts // GLA_CHUNK
    dk = GLA_KEY_DIM // GLA_HEADS
    cs = GLA_CHUNK

    @pl.when(pl.program_id(1) == 0)
    def _():
        st_scr[...] = jnp.zeros(st_scr.shape, F32)

    x = _dot(ga_ref[...].astype(MXU_DTYPE), wa_ref[...]) + ba_ref[...]
    log_a = (jnp.minimum(x, 0.0) - jnp.log(1.0 + jnp.exp(-jnp.abs(x)))) * (1.0 / GLA_TAU)
    l_hi, l_mid, l_lo = _split3(log_a)
    tri = tri_ref[...]
    for n in range(nch):
        r = slice(n * cs, (n + 1) * cs)
        b_scr[r, :] = _dot(tri, l_hi[r]) + _dot(tri, l_mid[r]) + _dot(tri, l_lo[r])
    b = b_scr[...]
    eb = jnp.exp(b)
    q_t = (q_ref[...] * (dk ** -0.5) * eb).astype(MXU_DTYPE)
    k_all = k_ref[...]
    k_t = (k_all * jnp.exp(-b)).astype(MXU_DTYPE)
    v_all = v_ref[...].astype(MXU_DTYPE)
    causal = tri_ref[...] > 0
    blockmask = bmk_ref[...]

    for n in range(nch):
        r = slice(n * cs, (n + 1) * cs)
        b_last = b[(n + 1) * cs - 1:(n + 1) * cs, :]
        k_d = (k_all[r] * jnp.exp(b_last - b[r])).astype(MXU_DTYPE)
        u_scr[n] = _dot_tn(v_all[r], k_d) * blockmask
    state = st_scr[...]
    for n in range(nch):
        sb_scr[n] = state.astype(MXU_DTYPE)
        b_last = b[(n + 1) * cs - 1:(n + 1) * cs, :]
        state = state * jnp.exp(b_last) + u_scr[n]
    st_scr[...] = state

    for n in range(nch):
        r = slice(n * cs, (n + 1) * cs)
        qn = q_t[r]
        o = _dot_nt(qn, sb_scr[n])
        for h in range(GLA_HEADS):
            attn = jnp.where(causal, _dot_nt(qn * hm_ref[h:h + 1, :], k_t[r]), 0.0)
            o = o + _dot(attn.astype(MXU_DTYPE), v_all[r]) * vm_ref[h:h + 1, :]
        o2 = o * o
        o_hi, o_mid, o_lo = _split3(o2)
        gm = gm_ref[...]
        ms = _dot(o_hi, gm) + _dot(o_mid, gm) + _dot(o_lo, gm)
        y = o * lax.rsqrt(ms + EPS) * g_ref[...]
        o_ref[r, :] = (y * _silu(r_ref[r, :])).astype(o_ref.dtype)


def _gla(gla, misc, wa_pad, ba, g):
    b, s, _ = gla.shape
    ts = min(TS_GLA, s)
    nch = ts // GLA_CHUNK
    dk, dv = GLA_KEY_DIM // GLA_HEADS, GLA_VALUE_DIM // GLA_HEADS
    tri = np.tril(np.ones((GLA_CHUNK, GLA_CHUNK), np.float32))
    hm = (np.arange(GLA_KEY_DIM)[None, :] // dk == np.arange(GLA_HEADS)[:, None]).astype(np.float32)
    vm = (np.arange(GLA_VALUE_DIM)[None, :] // dv == np.arange(GLA_HEADS)[:, None]).astype(np.float32)
    bmk = (np.arange(GLA_VALUE_DIM)[:, None] // dv == np.arange(GLA_KEY_DIM)[None, :] // dk).astype(np.float32)
    gm = (np.arange(GLA_VALUE_DIM)[:, None] // dv == np.arange(GLA_VALUE_DIM)[None, :] // dv).astype(np.float32) / dv
    return pl.pallas_call(
        _gla_kernel,
        grid=(b, s // ts),
        in_specs=[
            pl.BlockSpec((None, ts, GLA_KEY_DIM), lambda bi, i: (bi, i, 0)),
            pl.BlockSpec((None, ts, GLA_KEY_DIM), lambda bi, i: (bi, i, 1)),
            pl.BlockSpec((None, ts, GLA_VALUE_DIM), lambda bi, i: (bi, i, 1)),
            pl.BlockSpec((None, ts, GLA_VALUE_DIM), lambda bi, i: (bi, i, 2)),
            pl.BlockSpec((None, None, ts, LANES), lambda bi, i: (0, bi, i, 0)),
            _full((LANES, GLA_KEY_DIM)), _full((1, GLA_KEY_DIM)), _full((1, GLA_VALUE_DIM)),
            _full(tri.shape), _full(hm.shape), _full(vm.shape), _full(bmk.shape), _full(gm.shape),
        ],
        out_specs=pl.BlockSpec((None, ts, GLA_VALUE_DIM), lambda bi, i: (bi, i, 0)),
        out_shape=jax.ShapeDtypeStruct((b, s, GLA_VALUE_DIM), MXU_DTYPE),
        scratch_shapes=[
            pltpu.VMEM((GLA_VALUE_DIM, GLA_KEY_DIM), F32),
            pltpu.VMEM((nch, GLA_VALUE_DIM, GLA_KEY_DIM), MXU_DTYPE),
            pltpu.VMEM((ts, GLA_KEY_DIM), F32),
            pltpu.VMEM((nch, GLA_VALUE_DIM, GLA_KEY_DIM), F32),
        ],
        compiler_params=_params("parallel", "arbitrary"),
        name="gla",
    )(gla, gla, gla, gla, misc, wa_pad, ba, g,
      jnp.asarray(tri, MXU_DTYPE), jnp.asarray(hm, MXU_DTYPE), jnp.asarray(vm), jnp.asarray(bmk),
      jnp.asarray(gm, MXU_DTYPE))


def _post_kernel(x_ref, yn_ref, yc_ref, yg_ref, p_ref, wo_ref, gf_ref, wg_ref, wu_ref, wd_ref,
                 gp_ref, wpg_ref, wpp_ref, gfin_ref, o_ref, *, final):
    dn = yn_ref.shape[1]
    dc = yc_ref.shape[1]
    mix = _dot(yn_ref[...], wo_ref[0:dn, :]) + _dot(yc_ref[...], wo_ref[dn:dn + dc, :]) \
        + _dot(yg_ref[...], wo_ref[dn + dc:, :])
    x = x_ref[...] + mix
    h = _rms(x, gf_ref[...]).astype(MXU_DTYPE)
    dff = wg_ref.shape[1]
    step = dff // FF_CHUNKS
    ff = None
    for j in range(FF_CHUNKS):
        cols = slice(j * step, (j + 1) * step)
        act = _silu(_dot(h, wg_ref[:, cols])) * _dot(h, wu_ref[:, cols])
        down = _dot(act.astype(MXU_DTYPE), wd_ref[cols, :])
        ff = down if ff is None else ff + down
    x = x + ff
    gate = _sigmoid(_dot(_rms(x, gp_ref[...]).astype(MXU_DTYPE), wpg_ref[...]))
    x = x + _dot(p_ref[...].astype(MXU_DTYPE), wpp_ref[...]) * gate
    if final:
        x = _rms(x, gfin_ref[...])
    o_ref[...] = x


def _post(x2d, yn, yc, yg, p2d, wo, gf, wg, wu, wd, gp, wpg, wpp, gfin, final):
    t, d = x2d.shape
    tm = min(TM_PROJ, t)
    row = lambda i: (i, 0)
    once = dict(pipeline_mode=pl.Buffered(1))

    def const(a):
        return pl.BlockSpec(a.shape, lambda i: (0,) * a.ndim, **once)

    return pl.pallas_call(
        functools.partial(_post_kernel, final=final),
        grid=(t // tm,),
        in_specs=[
            pl.BlockSpec((tm, d), row),
            pl.BlockSpec((tm, yn.shape[1]), row), pl.BlockSpec((tm, yc.shape[1]), row),
            pl.BlockSpec((tm, yg.shape[1]), row), pl.BlockSpec((tm, p2d.shape[1]), row),
            const(wo), const(gf), const(wg), const(wu), const(wd), const(gp), const(wpg), const(wpp),
            const(gfin),
        ],
        out_specs=pl.BlockSpec((tm, d), row),
        out_shape=jax.ShapeDtypeStruct((t, d), F32),
        compiler_params=_params("parallel"),
        name="post_mixer",
    )(x2d, yn, yc, yg, p2d, wo, gf, wg, wu, wd, gp, wpg, wpp, gfin)


def _layer(x2d, p2d, bsz, seq, rel_tabs, agg, lw, final_g, final):
    (mix_g, w_in, w_out, pos_kv, w1_kv, b1_kv, w2_kv, conv_w, conv_b, conv_ln_g, conv_ln_b,
     w_alpha, b_alpha, gla_g, ffn_g, w_gate, w_up, w_down, ple_g, w_pg, w_pp, qconst) = lw
    bm, ac = rel_tabs
    t = bsz * seq
    hk = NSA_KV_HEADS
    kvw = hk * HEAD_DIM

    w_all = (jnp.where(jnp.asarray(_W_SRC >= 0)[None, :], jnp.take(w_in, jnp.asarray(np.maximum(_W_SRC, 0)), axis=1), 0.0)
             * jnp.asarray(_W_SCALE)[None, :]).astype(MXU_DTYPE)
    q8, kv4, kvc, cu, gla, misc = _in_proj(x2d, mix_g[None, :], w_all, qconst)

    nsub = seq // CMP_STRIDE
    sub = kvc.reshape(bsz, nsub, CMP_STRIDE, 2, hk, HEAD_DIM).transpose(3, 0, 4, 1, 2, 5)
    sub = sub.reshape(2, bsz, hk, nsub, CMP_STRIDE * HEAD_DIM)
    cmp = _compress(sub, pos_kv, w1_kv, b1_kv, w2_kv)
    ones_row = jnp.ones((bsz, hk, 2, nsub), MXU_DTYPE)
    kc = jnp.concatenate([cmp[0].transpose(0, 1, 3, 2), ones_row,
                          jnp.zeros((bsz, hk, LANES - HEAD_DIM - 2, nsub), MXU_DTYPE)], axis=2)
    ones_col = jnp.ones((bsz, hk, nsub, 1), MXU_DTYPE)
    vc = jnp.concatenate([cmp[1], ones_col, jnp.zeros((bsz, hk, nsub, LANES - HEAD_DIM - 1), MXU_DTYPE)], axis=3)

    kv4 = kv4.reshape(bsz, seq, 4, hk, HEAD_DIM)
    def keys_t(i):
        return kv4[:, :, i].transpose(0, 2, 3, 1)
    def vals(i):
        v = kv4[:, :, i].transpose(0, 2, 1, 3)
        return jnp.concatenate([v, jnp.ones((bsz, hk, seq, 1), MXU_DTYPE),
                                jnp.zeros((bsz, hk, seq, LANES - HEAD_DIM - 1), MXU_DTYPE)], axis=3)
    ones2 = jnp.ones((bsz, hk, 2, seq), MXU_DTYPE)
    zpad = jnp.zeros((bsz, hk, LANES - HEAD_DIM - 2, seq), MXU_DTYPE)
    nbp = agg.shape[1]
    blk_rows = (np.arange(seq)[None, :] // SLC_BLOCK == np.arange(nbp)[:, None])
    blk_rows = jnp.broadcast_to(jnp.asarray(blk_rows, MXU_DTYPE)[None, None], (bsz, hk, nbp, seq))
    ks = jnp.concatenate([keys_t(0), ones2, zpad, blk_rows], axis=2)
    kw = jnp.concatenate([keys_t(2), ones2, zpad], axis=2)
    y_nsa = _nsa(q8.reshape(NSA_HEADS, bsz, seq, LANES), ks, vals(1), kw, vals(3), kc, vc,
                 misc.reshape(hk, bsz, seq, LANES), ac, bm, agg)

    y_conv = _conv(cu.reshape(bsz, seq, C_CU), conv_w, conv_b[None, :], conv_ln_g[None, :], conv_ln_b[None, :])
    y_gla = _gla(gla.reshape(bsz, seq, C_GLA), misc.reshape(hk, bsz, seq, LANES), w_alpha, b_alpha[None, :],
                 gla_g[None, :])

    return _post(x2d, y_nsa.reshape(t, -1), y_conv.reshape(t, -1), y_gla.reshape(t, -1), p2d,
                 w_out.astype(MXU_DTYPE), ffn_g[None, :], w_gate.astype(MXU_DTYPE), w_up.astype(MXU_DTYPE),
                 w_down.astype(MXU_DTYPE), ple_g[None, :], w_pg.astype(MXU_DTYPE), w_pp.astype(MXU_DTYPE),
                 final_g[None, :], final)


def kernel(x, p, rel_bias, mix_norm_g, w_in, w_out, cmp_pos_k, cmp_w1_k, cmp_b1_k, cmp_w2_k, cmp_pos_v, cmp_w1_v, cmp_b1_v, cmp_w2_v, conv_w, conv_b, conv_ln_g, conv_ln_b, gla_w_alpha, gla_b_alpha, gla_norm_g, ffn_norm_g, ffn_w_gate, ffn_w_up, ffn_w_down, ple_norm_g, ple_w_gate, ple_w_proj, final_norm_g):
    bsz, seq, d = x.shape
    depth = w_in.shape[0]
    assert seq % TQ == 0 and seq // SLC_BLOCK <= LANES
    t = bsz * seq
    rel_tabs = _nsa_bias_tables(rel_bias.astype(F32))
    nbp = LANES
    agg = jnp.asarray(_agg_matrix(seq // CMP_STRIDE, nbp), MXU_DTYPE)

    far = rel_bias[REL_BUCKETS - 1].astype(F32)
    f_hi, f_lo = _split_hi_lo(far)
    qconst = jnp.zeros((NSA_HEADS, LANES), F32)
    qconst = qconst.at[:, FAR_LANE].set(f_hi.astype(F32)).at[:, FAR_LANE + 1].set(f_lo.astype(F32))
    qconst = qconst.reshape(1, C_Q)

    x2d = x.reshape(t, d)
    for i in range(depth):
        wa_pad = jnp.zeros((LANES, GLA_KEY_DIM), F32).at[GA_LANE0:GA_LANE0 + GLA_GATE_RANK].set(gla_w_alpha[i])
        conv_w_pad = jnp.zeros((CONV_HALO, CONV_CHANNELS), F32).at[:CONV_WIDTH].set(conv_w[i])
        lw = (mix_norm_g[i], w_in[i], w_out[i],
              jnp.stack([cmp_pos_k[i].reshape(1, -1), cmp_pos_v[i].reshape(1, -1)]),
              jnp.stack([cmp_w1_k[i], cmp_w1_v[i]]).astype(MXU_DTYPE),
              jnp.stack([cmp_b1_k[i][None, :], cmp_b1_v[i][None, :]]),
              jnp.stack([cmp_w2_k[i], cmp_w2_v[i]]).astype(MXU_DTYPE),
              conv_w_pad, conv_b[i], conv_ln_g[i], conv_ln_b[i],
              wa_pad.astype(MXU_DTYPE), gla_b_alpha[i], gla_norm_g[i],
              ffn_norm_g[i], ffn_w_gate[i], ffn_w_up[i], ffn_w_down[i],
              ple_norm_g[i], ple_w_gate[i], ple_w_proj[i], qconst)
        x2d = _layer(x2d, p[i].reshape(t, -1), bsz, seq, rel_tabs, agg, lw, final_norm_g, i == depth - 1)
    return x2d.reshape(bsz, seq, d)
```

```python
import functools
import math

import numpy as np
import jax
import jax.numpy as jnp
from jax import lax
from jax.experimental import pallas as pl
from jax.experimental.pallas import tpu as pltpu

F32 = jnp.float32
MXU_DTYPE = jnp.bfloat16

HEAD_DIM = 64
NSA_HEADS = 8
NSA_KV_HEADS = 2
NSA_GROUP = NSA_HEADS // NSA_KV_HEADS
CMP_BLOCK = 32
CMP_STRIDE = 16
CMP_HIDDEN = 256
SLC_BLOCK = 64
N_SELECT = 16
WINDOW = 512
N_BRANCH = 3
CONV_CHANNELS = 256
CONV_WIDTH = 31
GLA_HEADS = 4
GLA_KEY_DIM = 128
GLA_VALUE_DIM = 256
GLA_GATE_RANK = 16
GLA_TAU = 16.0
GLA_CHUNK = 64
REL_BUCKETS = 32
REL_MAX_DIST = 128
PLE_DIM = 256
EPS = 1e-6
NEG_BIG = -1e30

LANES = 128
VMEM_LIMIT_BYTES = 56 * 1024 * 1024

TM_PROJ = 512
TQ = 256
ROWS = NSA_GROUP * TQ
NEAR_W = 2 * TQ // CMP_STRIDE
TS_CONV = 512
CONV_HALO = 32
TS_GLA = 1024
FF_CHUNKS = 2

C_Q = NSA_HEADS * LANES
C_KV4 = 4 * NSA_KV_HEADS * HEAD_DIM
C_KVC = 2 * NSA_KV_HEADS * HEAD_DIM
C_CU = 2 * CONV_CHANNELS
C_GLA = 2 * GLA_KEY_DIM + 2 * GLA_VALUE_DIM
C_MISC = NSA_KV_HEADS * LANES
GA_LANE0 = 16
FAR_LANE = HEAD_DIM


def _split_hi_lo(x):
    hi = x.astype(MXU_DTYPE)
    lo = (x - hi.astype(F32)).astype(MXU_DTYPE)
    return hi, lo


def _split3(x):
    hi = x.astype(MXU_DTYPE)
    r = x - hi.astype(F32)
    mid = r.astype(MXU_DTYPE)
    lo = (r - mid.astype(F32)).astype(MXU_DTYPE)
    return hi, mid, lo


def _dot(a, b):
    return jnp.dot(a, b, preferred_element_type=F32)


def _dot_nt(a, b):
    return lax.dot_general(a, b, (((1,), (1,)), ((), ())), preferred_element_type=F32)


def _dot_tn(a, b):
    return lax.dot_general(a, b, (((0,), (0,)), ((), ())), preferred_element_type=F32)


def _rms(x, g):
    return x * lax.rsqrt(jnp.mean(x * x, axis=-1, keepdims=True) + EPS) * g


def _sigmoid(x):
    return 1.0 / (1.0 + jnp.exp(-x))


def _silu(x):
    return x * _sigmoid(x)


def _params(*sem):
    return pltpu.CompilerParams(dimension_semantics=sem, vmem_limit_bytes=VMEM_LIMIT_BYTES)


def _full(shape):
    nd = len(shape)
    return pl.BlockSpec(shape, lambda *_: (0,) * nd)


def _t5_bucket_np(dist):
    n = np.maximum(dist, 0)
    max_exact = REL_BUCKETS // 2
    nf = np.maximum(n, 1).astype(np.float32)
    large = max_exact + (np.log(nf / np.float32(max_exact)) / np.float32(math.log(REL_MAX_DIST / max_exact))
                         * np.float32(REL_BUCKETS - max_exact)).astype(np.int32)
    large = np.minimum(large, REL_BUCKETS - 1)
    return np.where(n < max_exact, n, large).astype(np.int32)


def _in_proj_columns(in_splits):
    offs = np.concatenate([[0], np.cumsum(in_splits)])
    (o_nq, o_kc, o_vc, o_ks, o_vs, o_kw, o_vw, o_ng, o_cu, o_gq, o_gk, o_gv, o_ga, o_gr) = offs[:-1]
    src, scale = [], []

    def put(cols, s=1.0):
        src.extend(cols)
        scale.extend([s] * len(cols))

    for hd in range(NSA_HEADS):
        put(list(range(o_nq + hd * HEAD_DIM, o_nq + (hd + 1) * HEAD_DIM)), HEAD_DIM ** -0.5)
        put([-1] * (LANES - HEAD_DIM))
    kvw = NSA_KV_HEADS * HEAD_DIM
    for o in (o_ks, o_vs, o_kw, o_vw, o_kc, o_vc):
        put(list(range(o, o + kvw)))
    put(list(range(o_cu, o_cu + C_CU)))
    put(list(range(o_gq, o_gq + GLA_KEY_DIM)))
    put(list(range(o_gk, o_gk + GLA_KEY_DIM)))
    put(list(range(o_gv, o_gv + GLA_VALUE_DIM)))
    put(list(range(o_gr, o_gr + GLA_VALUE_DIM)))
    per = NSA_GROUP * N_BRANCH
    for h in range(NSA_KV_HEADS):
        slab = [-1] * LANES
        slab[:per] = list(range(o_ng + h * per, o_ng + (h + 1) * per))
        if h == 0:
            slab[GA_LANE0:GA_LANE0 + GLA_GATE_RANK] = list(range(o_ga, o_ga + GLA_GATE_RANK))
        put(slab)
    return np.asarray(src, np.int32), np.asarray(scale, np.float32)


IN_SPLITS = (
    NSA_HEADS * HEAD_DIM,
    NSA_KV_HEADS * HEAD_DIM, NSA_KV_HEADS * HEAD_DIM,
    NSA_KV_HEADS * HEAD_DIM, NSA_KV_HEADS * HEAD_DIM,
    NSA_KV_HEADS * HEAD_DIM, NSA_KV_HEADS * HEAD_DIM,
    NSA_HEADS * N_BRANCH,
    2 * CONV_CHANNELS,
    GLA_KEY_DIM, GLA_KEY_DIM, GLA_VALUE_DIM,
    GLA_GATE_RANK,
    GLA_VALUE_DIM,
)
_W_SRC, _W_SCALE = _in_proj_columns(IN_SPLITS)
C_ALL = C_Q + C_KV4 + C_KVC + C_CU + C_GLA + C_MISC
assert _W_SRC.shape[0] == C_ALL


def _nsa_bias_tables(rel_bias):
    i = np.arange(TQ)[:, None]
    j = np.arange(TQ)[None, :]
    far = rel_bias[REL_BUCKETS - 1]
    tab = rel_bias - far[None, :]

    def rel(dist):
        vals = jnp.take(tab, jnp.asarray(_t5_bucket_np(dist)), axis=0)
        vals = jnp.where(jnp.asarray(dist >= 0)[..., None], vals, NEG_BIG)
        vals = vals.transpose(2, 0, 1).reshape(NSA_KV_HEADS, ROWS, dist.shape[1])
        return vals

    tri = jnp.asarray(np.where(j <= i, NEG_BIG, 0.0).astype(np.float32))
    tri = jnp.broadcast_to(jnp.tile(tri, (NSA_GROUP, 1))[None], (NSA_KV_HEADS, ROWS, TQ))
    bm = jnp.concatenate([tri, rel(i + TQ - j), rel(i - j)], axis=-1)

    m = np.arange(NEAR_W)[None, :]
    near = rel(i - CMP_STRIDE * m + (NEAR_W // 2) * CMP_STRIDE - (CMP_BLOCK - 1))
    hi, lo = _split_hi_lo(near)
    pad = jnp.zeros((NSA_KV_HEADS, ROWS, LANES - 2 * NEAR_W - 1), MXU_DTYPE)
    big = jnp.full((NSA_KV_HEADS, ROWS, 1), NEG_BIG, MXU_DTYPE)
    ac = jnp.concatenate([hi, lo, big, pad], axis=-1)
    return bm, ac


def _agg_matrix(ncp, nbp):
    rs, rc = SLC_BLOCK // CMP_STRIDE, CMP_BLOCK // CMP_STRIDE
    agg = np.zeros((ncp, nbp), np.float32)
    ncmp = ncp - rc + 1
    for jb in range(nbp):
        for mm in range(rs):
            for nn in range(rc):
                idx = jb * rs + mm - nn
                if 0 <= idx < ncmp:
                    agg[idx, jb] += 1.0
    return agg


def _in_proj_kernel(x_ref, g_ref, w_ref, qc_ref, q_ref, kv4_ref, kvc_ref, cu_ref, gla_ref, misc_ref):
    h = _rms(x_ref[...], g_ref[...]).astype(MXU_DTYPE)
    o = 0
    zq = _dot(h, w_ref[:, o:o + C_Q]) + qc_ref[...]
    for hd in range(NSA_HEADS):
        q_ref[hd] = zq[:, hd * LANES:(hd + 1) * LANES].astype(q_ref.dtype)
    o += C_Q
    kv4_ref[...] = _dot(h, w_ref[:, o:o + C_KV4]).astype(kv4_ref.dtype)
    o += C_KV4
    kvc_ref[...] = _dot(h, w_ref[:, o:o + C_KVC])
    o += C_KVC
    cu_ref[...] = _dot(h, w_ref[:, o:o + C_CU])
    o += C_CU
    gla_ref[...] = _dot(h, w_ref[:, o:o + C_GLA])
    o += C_GLA
    zm = _dot(h, w_ref[:, o:o + C_MISC])
    for s in range(NSA_KV_HEADS):
        misc_ref[s] = zm[:, s * LANES:(s + 1) * LANES]


def _in_proj(x2d, g, w_all, qconst):
    t, d = x2d.shape
    tm = min(TM_PROJ, t)
    row = lambda i: (i, 0)
    return pl.pallas_call(
        _in_proj_kernel,
        grid=(t // tm,),
        in_specs=[pl.BlockSpec((tm, d), row), _full((1, d)), _full((d, C_ALL)), _full((1, C_Q))],
        out_specs=[
            pl.BlockSpec((NSA_HEADS, tm, LANES), lambda i: (0, i, 0)),
            pl.BlockSpec((tm, C_KV4), row),
            pl.BlockSpec((tm, C_KVC), row),
            pl.BlockSpec((tm, C_CU), row),
            pl.BlockSpec((tm, C_GLA), row),
            pl.BlockSpec((NSA_KV_HEADS, tm, LANES), lambda i: (0, i, 0)),
        ],
        out_shape=[
            jax.ShapeDtypeStruct((NSA_HEADS, t, LANES), MXU_DTYPE),
            jax.ShapeDtypeStruct((t, C_KV4), MXU_DTYPE),
            jax.ShapeDtypeStruct((t, C_KVC), F32),
            jax.ShapeDtypeStruct((t, C_CU), F32),
            jax.ShapeDtypeStruct((t, C_GLA), F32),
            jax.ShapeDtypeStruct((NSA_KV_HEADS, t, LANES), F32),
        ],
        compiler_params=_params("parallel"),
        name="in_proj",
    )(x2d, g, w_all, qconst)


def _compress_kernel(sub_ref, pos_ref, w1_ref, b1_ref, w2_ref, o_ref):
    sub = sub_ref[...]
    nc = sub.shape[0]
    half = sub.shape[1]
    top = _dot((sub + pos_ref[:, :half]).astype(MXU_DTYPE), w1_ref[:half, :])
    bot = _dot((sub + pos_ref[:, half:]).astype(MXU_DTYPE), w1_ref[half:, :])
    hid = _silu(top + pltpu.roll(bot, nc - 1, 0) + b1_ref[...])
    o_ref[...] = _dot(hid.astype(MXU_DTYPE), w2_ref[...]).astype(o_ref.dtype)


def _compress(sub, pos, w1, b1, w2):
    _, b, hk, nc, half = sub.shape
    kv = lambda s, bi, h: (s, 0, 0)
    return pl.pallas_call(
        _compress_kernel,
        grid=(2, b, hk),
        in_specs=[
            pl.BlockSpec((None, None, None, nc, half), lambda s, bi, h: (s, bi, h, 0, 0)),
            pl.BlockSpec((None, 1, 2 * half), kv),
            pl.BlockSpec((None, 2 * half, CMP_HIDDEN), kv),
            pl.BlockSpec((None, 1, CMP_HIDDEN), kv),
            pl.BlockSpec((None, CMP_HIDDEN, HEAD_DIM), kv),
        ],
        out_specs=pl.BlockSpec((None, None, None, nc, HEAD_DIM), lambda s, bi, h: (s, bi, h, 0, 0)),
        out_shape=jax.ShapeDtypeStruct((2, b, hk, nc, HEAD_DIM), MXU_DTYPE),
        compiler_params=_params("parallel", "parallel", "parallel"),
        name="compress_kv",
    )(sub, pos, w1, b1, w2)


def _nsa_kernel(*refs):
    c = pl.program_id(2)

    @pl.when(c >= 2)
    def _():
        _nsa_body(True, c, *refs)

    @pl.when(c < 2)
    def _():
        _nsa_body(False, c, *refs)


def _nsa_body(full, c, q_ref, ks_ref, vs_ref, kw_ref, vw_ref, kc_ref, vc_ref, gl_ref, ac_ref, bm_ref, agg_ref,
              o_ref, kcx_scr, qs_scr, m_scr, acc_scr, sa_scr, sb_scr):
    nc = kc_ref.shape[1]
    nb = agg_ref.shape[1]
    q = q_ref[...].reshape(ROWS, LANES)

    rowi = lax.broadcasted_iota(jnp.int32, (LANES, nc), 0)
    coli = lax.broadcasted_iota(jnp.int32, (LANES, nc), 1)
    near0 = c * (TQ // CMP_STRIDE) - NEAR_W // 2
    shift = (coli == near0 + (rowi & (NEAR_W - 1))) & (rowi < 2 * NEAR_W)
    future = (rowi == 2 * NEAR_W) & (coli >= near0 + NEAR_W)
    kcx_scr[0:LANES, :] = kc_ref[...]
    kcx_scr[LANES:2 * LANES, :] = jnp.where(shift | future, 1.0, 0.0).astype(MXU_DTYPE)
    s_c = _dot(jnp.concatenate([q, ac_ref[...]], axis=1), kcx_scr[...])
    m_c = jnp.maximum(jnp.max(s_c, axis=1, keepdims=True), 0.1 * NEG_BIG)
    e_c = jnp.exp(s_c - m_c)
    l_c = jnp.sum(e_c, axis=1, keepdims=True)
    p_c = e_c * (1.0 / jnp.maximum(l_c, 1e-30))
    acc_c = _dot(p_c.astype(MXU_DTYPE), vc_ref[...])

    imp = p_c[0:TQ] + p_c[TQ:2 * TQ] + p_c[2 * TQ:3 * TQ] + p_c[3 * TQ:4 * TQ]
    agg = agg_ref[...]
    i_hi, i_mid, i_lo = _split3(imp)
    imps = _dot(i_hi, agg) + _dot(i_mid, agg) + _dot(i_lo, agg)
    t_q = c * TQ + lax.broadcasted_iota(jnp.int32, (nb, TQ), 1)
    blk = lax.broadcasted_iota(jnp.int32, (nb, TQ), 0)
    cur = lax.shift_right_logical(t_q, int(math.log2(SLC_BLOCK)))
    causal = blk <= cur
    forced = (blk == 0) | (blk == cur) | (blk == cur - 1)
    neg_inf = -jnp.inf
    cand = jnp.where(causal & jnp.logical_not(forced), imps.T, neg_inf)
    sel = forced & causal
    blk_f = blk.astype(F32)
    for _ in range(N_SELECT - 3):
        best = jnp.max(cand, axis=0, keepdims=True)
        first = jnp.min(jnp.where(cand == best, blk_f, float(nb)), axis=0, keepdims=True)
        hit = (blk_f == first) & (best > neg_inf)
        sel = sel | hit
        cand = jnp.where(hit, neg_inf, cand)
    selneg = jnp.where(sel, 0.0, NEG_BIG).T.astype(MXU_DTYPE)
    qs_scr[:, 0:LANES] = q
    qs_scr[:, LANES:2 * LANES] = jnp.concatenate([selneg] * NSA_GROUP, axis=0)

    def flash_init():
        m_scr[...] = jnp.full(m_scr.shape, NEG_BIG, F32)
        acc_scr[...] = jnp.zeros(acc_scr.shape, F32)

    def scores(k_ref, kt):
        start = pl.multiple_of(kt * TQ, TQ)
        return _dot(qs_scr[:, 0:k_ref.shape[0]], k_ref[:, pl.ds(start, TQ)])

    def flash_update(s, v_ref, kt, ntiles=1):
        start = pl.multiple_of(kt * TQ, TQ)
        m_prev = m_scr[...]
        m_new = jnp.maximum(m_prev, jnp.max(s, axis=1, keepdims=True))
        p = jnp.exp(s - jnp.concatenate([m_new] * (ntiles * TQ // LANES), axis=1))
        acc_scr[...] = jnp.exp(m_prev - m_new) * acc_scr[...] \
            + _dot(p.astype(MXU_DTYPE), v_ref[pl.ds(start, ntiles * TQ), :])
        m_scr[...] = m_new

    def flash_step(k_ref, v_ref, kt, bias_tile):
        s = scores(k_ref, kt)
        if bias_tile is not None:
            s = s + bm_ref[:, bias_tile * TQ:(bias_tile + 1) * TQ]
        flash_update(s, v_ref, kt)

    def flash_out():
        acc = acc_scr[...]
        return acc * (1.0 / acc[:, HEAD_DIM:HEAD_DIM + 1])

    if full:
        w0 = pl.multiple_of((c - 2) * TQ, TQ)
        s_w = _dot(q, kw_ref[:, pl.ds(w0, 3 * TQ)]) + bm_ref[...]
        p_w = jnp.exp(s_w - jnp.max(s_w, axis=1, keepdims=True))
        acc_w = _dot(p_w.astype(MXU_DTYPE), vw_ref[pl.ds(w0, 3 * TQ), :])
        o_w = acc_w * (1.0 / acc_w[:, HEAD_DIM:HEAD_DIM + 1])
    else:
        flash_init()

        @pl.when(c >= 1)
        def _():
            flash_step(kw_ref, vw_ref, c - 1, 1)

        flash_step(kw_ref, vw_ref, c, 2)
        o_w = flash_out()

    flash_init()
    if full:
        n_far = c - 1
        n_pairs = lax.shift_right_logical(n_far, 1)

        @pl.when(n_pairs > 0)
        def _():
            sa_scr[...] = scores(ks_ref, 0)

        def far_pair(j, carry):
            sb_scr[...] = scores(ks_ref, 2 * j + 1)
            flash_update(sa_scr[...], vs_ref, 2 * j)
            sa_scr[...] = scores(ks_ref, 2 * j + 2)
            flash_update(sb_scr[...], vs_ref, 2 * j + 1)
            return carry

        lax.fori_loop(0, n_pairs, far_pair, 0)

        @pl.when(n_far > 2 * n_pairs)
        def _():
            flash_step(ks_ref, vs_ref, n_far - 1, None)

        n0 = pl.multiple_of((c - 1) * TQ, TQ)
        s_n = _dot(qs_scr[...], ks_ref[:, pl.ds(n0, 2 * TQ)]) + bm_ref[:, TQ:3 * TQ]
        flash_update(s_n, vs_ref, c - 1, 2)
    else:
        @pl.when(c >= 1)
        def _():
            flash_step(ks_ref, vs_ref, c - 1, 1)

        flash_step(ks_ref, vs_ref, c, 2)
    o_s = flash_out()

    gates = _sigmoid(gl_ref[...])
    lane = lax.broadcasted_iota(jnp.int32, (TQ, LANES), 1)
    outs = []
    for g in range(NSA_GROUP):
        r = slice(g * TQ, (g + 1) * TQ)
        gc = gates[:, g * N_BRANCH + 0:g * N_BRANCH + 1]
        gs = gates[:, g * N_BRANCH + 1:g * N_BRANCH + 2]
        gw = gates[:, g * N_BRANCH + 2:g * N_BRANCH + 3]
        outs.append(gc * acc_c[r] + gs * o_s[r] + gw * o_w[r])
    for pair in range(NSA_GROUP // 2):
        both = jnp.where(lane < HEAD_DIM, outs[2 * pair], pltpu.roll(outs[2 * pair + 1], HEAD_DIM, 1))
        o_ref[:, pair * LANES:(pair + 1) * LANES] = both.astype(o_ref.dtype)


def _nsa(q8, ks, vs, kw, vw, kc, vc, gl, ac, bm, agg):
    _, b, s, _ = q8.shape
    nc = kc.shape[-1]
    nb = agg.shape[1]
    kvmap = lambda bi, h, c: (bi, h, 0, 0)
    hmap = lambda bi, h, c: (h, 0, 0)
    return pl.pallas_call(
        _nsa_kernel,
        grid=(b, NSA_KV_HEADS, s // TQ),
        in_specs=[
            pl.BlockSpec((NSA_GROUP, None, TQ, LANES), lambda bi, h, c: (h, bi, c, 0)),
            pl.BlockSpec((None, None, 2 * LANES, s), kvmap),
            pl.BlockSpec((None, None, s, LANES), kvmap),
            pl.BlockSpec((None, None, LANES, s), kvmap),
            pl.BlockSpec((None, None, s, LANES), kvmap),
            pl.BlockSpec((None, None, LANES, nc), kvmap),
            pl.BlockSpec((None, None, nc, LANES), kvmap),
            pl.BlockSpec((None, None, TQ, LANES), lambda bi, h, c: (h, bi, c, 0)),
            pl.BlockSpec((None, ROWS, LANES), hmap),
            pl.BlockSpec((None, ROWS, 3 * TQ), hmap),
            _full((nc, nb)),
        ],
        out_specs=pl.BlockSpec((None, TQ, NSA_GROUP * HEAD_DIM), lambda bi, h, c: (bi, c, h)),
        out_shape=jax.ShapeDtypeStruct((b, s, NSA_HEADS * HEAD_DIM), MXU_DTYPE),
        scratch_shapes=[
            pltpu.VMEM((2 * LANES, nc), MXU_DTYPE),
            pltpu.VMEM((ROWS, 2 * LANES), MXU_DTYPE),
            pltpu.VMEM((ROWS, LANES), F32),
            pltpu.VMEM((ROWS, LANES), F32),
            pltpu.VMEM((ROWS, TQ), F32),
            pltpu.VMEM((ROWS, TQ), F32),
        ],
        compiler_params=_params("parallel", "parallel", "arbitrary"),
        name="nsa_attention",
    )(q8, ks, vs, kw, vw, kc, vc, gl, ac, bm, agg)


def _conv_kernel(cu_ref, halo_ref, w_ref, b_ref, g_ref, bb_ref, o_ref, hh_scr):
    ts = cu_ref.shape[0]
    ch = CONV_CHANNELS

    def glu(u):
        return u[:, :ch] * _sigmoid(u[:, ch:])

    first = pl.program_id(1) == 0
    hh_scr[0:CONV_HALO, :] = jnp.where(first, 0.0, glu(halo_ref[...]))
    hh_scr[CONV_HALO:CONV_HALO + ts, :] = glu(cu_ref[...])
    acc = jnp.broadcast_to(b_ref[...], (ts, ch))
    for w in range(CONV_WIDTH):
        acc = acc + hh_scr[pl.ds(CONV_HALO - (CONV_WIDTH - 1) + w, ts), :] * w_ref[w:w + 1, :]
    mu = jnp.mean(acc, axis=-1, keepdims=True)
    xc = acc - mu
    y = xc * lax.rsqrt(jnp.mean(xc * xc, axis=-1, keepdims=True) + EPS) * g_ref[...] + bb_ref[...]
    o_ref[...] = _silu(y).astype(o_ref.dtype)


def _conv(cu, w, bias, ln_g, ln_b):
    b, s, _ = cu.shape
    ts = min(TS_CONV, s)
    per = ts // CONV_HALO
    return pl.pallas_call(
        _conv_kernel,
        grid=(b, s // ts),
        in_specs=[
            pl.BlockSpec((None, ts, C_CU), lambda bi, i: (bi, i, 0)),
            pl.BlockSpec((None, CONV_HALO, C_CU), lambda bi, i: (bi, jnp.maximum(i * per - 1, 0), 0)),
            _full((CONV_HALO, CONV_CHANNELS)), _full((1, CONV_CHANNELS)),
            _full((1, CONV_CHANNELS)), _full((1, CONV_CHANNELS)),
        ],
        out_specs=pl.BlockSpec((None, ts, CONV_CHANNELS), lambda bi, i: (bi, i, 0)),
        out_shape=jax.ShapeDtypeStruct((b, s, CONV_CHANNELS), MXU_DTYPE),
        scratch_shapes=[pltpu.VMEM((CONV_HALO + ts, CONV_CHANNELS), F32)],
        compiler_params=_params("parallel", "parallel"),
        name="conformer_conv",
    )(cu, cu, w, bias, ln_g, ln_b)


def _gla_kernel(q_ref, k_ref, v_ref, r_ref, ga_ref, wa_ref, ba_ref, g_ref, tri_ref, hm_ref, vm_ref, bmk_ref,
                gm_ref, o_ref, st_scr, sb_scr, b_scr, u_scr):
    ts = q_ref.shape[0]
    nch = ts // GLA_CHUNK
    dk = GLA_KEY_DIM // GLA_HEADS
    cs = GLA_CHUNK

    @pl.when(pl.program_id(1) == 0)
    def _():
        st_scr[...] = jnp.zeros(st_scr.shape, F32)

    x = _dot(ga_ref[...].astype(MXU_DTYPE), wa_ref[...]) + ba_ref[...]
    log_a = (jnp.minimum(x, 0.0) - jnp.log(1.0 + jnp.exp(-jnp.abs(x)))) * (1.0 / GLA_TAU)
    l_hi, l_mid, l_lo = _split3(log_a)
    tri = tri_ref[...]
    for n in range(nch):
        r = slice(n * cs, (n + 1) * cs)
        b_scr[r, :] = _dot(tri, l_hi[r]) + _dot(tri, l_mid[r]) + _dot(tri, l_lo[r])
    b = b_scr[...]
    eb = jnp.exp(b)
    q_t = (q_ref[...] * (dk ** -0.5) * eb).astype(MXU_DTYPE)
    k_all = k_ref[...]
    k_t = (k_all * jnp.exp(-b)).astype(MXU_DTYPE)
    v_all = v_ref[...].astype(MXU_DTYPE)
    causal = tri_ref[...] > 0
    blockmask = bmk_ref[...]

    for n in range(nch):
        r = slice(n * cs, (n + 1) * cs)
        b_last = b[(n + 1) * cs - 1:(n + 1) * cs, :]
        k_d = (k_all[r] * jnp.exp(b_last - b[r])).astype(MXU_DTYPE)
        u_scr[n] = _dot_tn(v_all[r], k_d) * blockmask
    state = st_scr[...]
    for n in range(nch):
        sb_scr[n] = state.astype(MXU_DTYPE)
        b_last = b[(n + 1) * cs - 1:(n + 1) * cs, :]
        state = state * jnp.exp(b_last) + u_scr[n]
    st_scr[...] = state

    for n in range(nch):
        r = slice(n * cs, (n + 1) * cs)
        qn = q_t[r]
        o = _dot_nt(qn, sb_scr[n])
        for h in range(GLA_HEADS):
            attn = jnp.where(causal, _dot_nt(qn * hm_ref[h:h + 1, :], k_t[r]), 0.0)
            o = o + _dot(attn.astype(MXU_DTYPE), v_all[r]) * vm_ref[h:h + 1, :]
        o2 = o * o
        o_hi, o_mid, o_lo = _split3(o2)
        gm = gm_ref[...]
        ms = _dot(o_hi, gm) + _dot(o_mid, gm) + _dot(o_lo, gm)
        y = o * lax.rsqrt(ms + EPS) * g_ref[...]
        o_ref[r, :] = (y * _silu(r_ref[r, :])).astype(o_ref.dtype)


def _gla(gla, misc, wa_pad, ba, g):
    b, s, _ = gla.shape
    ts = min(TS_GLA, s)
    nch = ts // GLA_CHUNK
    dk, dv = GLA_KEY_DIM // GLA_HEADS, GLA_VALUE_DIM // GLA_HEADS
    tri = np.tril(np.ones((GLA_CHUNK, GLA_CHUNK), np.float32))
    hm = (np.arange(GLA_KEY_DIM)[None, :] // dk == np.arange(GLA_HEADS)[:, None]).astype(np.float32)
    vm = (np.arange(GLA_VALUE_DIM)[None, :] // dv == np.arange(GLA_HEADS)[:, None]).astype(np.float32)
    bmk = (np.arange(GLA_VALUE_DIM)[:, None] // dv == np.arange(GLA_KEY_DIM)[None, :] // dk).astype(np.float32)
    gm = (np.arange(GLA_VALUE_DIM)[:, None] // dv == np.arange(GLA_VALUE_DIM)[None, :] // dv).astype(np.float32) / dv
    return pl.pallas_call(
        _gla_kernel,
        grid=(b, s // ts),
        in_specs=[
            pl.BlockSpec((None, ts, GLA_KEY_DIM), lambda bi, i: (bi, i, 0)),
            pl.BlockSpec((None, ts, GLA_KEY_DIM), lambda bi, i: (bi, i, 1)),
            pl.BlockSpec((None, ts, GLA_VALUE_DIM), lambda bi, i: (bi, i, 1)),
            pl.BlockSpec((None, ts, GLA_VALUE_DIM), lambda bi, i: (bi, i, 2)),
            pl.BlockSpec((None, None, ts, LANES), lambda bi, i: (0, bi, i, 0)),
            _full((LANES, GLA_KEY_DIM)), _full((1, GLA_KEY_DIM)), _full((1, GLA_VALUE_DIM)),
            _full(tri.shape), _full(hm.shape), _full(vm.shape), _full(bmk.shape), _full(gm.shape),
        ],
        out_specs=pl.BlockSpec((None, ts, GLA_VALUE_DIM), lambda bi, i: (bi, i, 0)),
        out_shape=jax.ShapeDtypeStruct((b, s, GLA_VALUE_DIM), MXU_DTYPE),
        scratch_shapes=[
            pltpu.VMEM((GLA_VALUE_DIM, GLA_KEY_DIM), F32),
            pltpu.VMEM((nch, GLA_VALUE_DIM, GLA_KEY_DIM), MXU_DTYPE),
            pltpu.VMEM((ts, GLA_KEY_DIM), F32),
            pltpu.VMEM((nch, GLA_VALUE_DIM, GLA_KEY_DIM), F32),
        ],
        compiler_params=_params("parallel", "arbitrary"),
        name="gla",
    )(gla, gla, gla, gla, misc, wa_pad, ba, g,
      jnp.asarray(tri, MXU_DTYPE), jnp.asarray(hm, MXU_DTYPE), jnp.asarray(vm), jnp.asarray(bmk),
      jnp.asarray(gm, MXU_DTYPE))


def _post_kernel(x_ref, yn_ref, yc_ref, yg_ref, p_ref, wo_ref, gf_ref, wg_ref, wu_ref, wd_ref,
                 gp_ref, wpg_ref, wpp_ref, gfin_ref, o_ref, *, final):
    dn = yn_ref.shape[1]
    dc = yc_ref.shape[1]
    mix = _dot(yn_ref[...], wo_ref[0:dn, :]) + _dot(yc_ref[...], wo_ref[dn:dn + dc, :]) \
        + _dot(yg_ref[...], wo_ref[dn + dc:, :])
    x = x_ref[...] + mix
    h = _rms(x, gf_ref[...]).astype(MXU_DTYPE)
    dff = wg_ref.shape[1]
    step = dff // FF_CHUNKS
    ff = None
    for j in range(FF_CHUNKS):
        cols = slice(j * step, (j + 1) * step)
        act = _silu(_dot(h, wg_ref[:, cols])) * _dot(h, wu_ref[:, cols])
        down = _dot(act.astype(MXU_DTYPE), wd_ref[cols, :])
        ff = down if ff is None else ff + down
    x = x + ff
    gate = _sigmoid(_dot(_rms(x, gp_ref[...]).astype(MXU_DTYPE), wpg_ref[...]))
    x = x + _dot(p_ref[...].astype(MXU_DTYPE), wpp_ref[...]) * gate
    if final:
        x = _rms(x, gfin_ref[...])
    o_ref[...] = x


def _post(x2d, yn, yc, yg, p2d, wo, gf, wg, wu, wd, gp, wpg, wpp, gfin, final):
    t, d = x2d.shape
    tm = min(TM_PROJ, t)
    row = lambda i: (i, 0)
    once = dict(pipeline_mode=pl.Buffered(1))

    def const(a):
        return pl.BlockSpec(a.shape, lambda i: (0,) * a.ndim, **once)

    return pl.pallas_call(
        functools.partial(_post_kernel, final=final),
        grid=(t // tm,),
        in_specs=[
            pl.BlockSpec((tm, d), row),
            pl.BlockSpec((tm, yn.shape[1]), row), pl.BlockSpec((tm, yc.shape[1]), row),
            pl.BlockSpec((tm, yg.shape[1]), row), pl.BlockSpec((tm, p2d.shape[1]), row),
            const(wo), const(gf), const(wg), const(wu), const(wd), const(gp), const(wpg), const(wpp),
            const(gfin),
        ],
        out_specs=pl.BlockSpec((tm, d), row),
        out_shape=jax.ShapeDtypeStruct((t, d), F32),
        compiler_params=_params("parallel"),
        name="post_mixer",
    )(x2d, yn, yc, yg, p2d, wo, gf, wg, wu, wd, gp, wpg, wpp, gfin)


def _layer(x2d, p2d, bsz, seq, rel_tabs, agg, lw, final_g, final):
    (mix_g, w_in, w_out, pos_kv, w1_kv, b1_kv, w2_kv, conv_w, conv_b, conv_ln_g, conv_ln_b,
     w_alpha, b_alpha, gla_g, ffn_g, w_gate, w_up, w_down, ple_g, w_pg, w_pp, qconst) = lw
    bm, ac = rel_tabs
    t = bsz * seq
    hk = NSA_KV_HEADS
    kvw = hk * HEAD_DIM

    w_all = (jnp.where(jnp.asarray(_W_SRC >= 0)[None, :], jnp.take(w_in, jnp.asarray(np.maximum(_W_SRC, 0)), axis=1), 0.0)
             * jnp.asarray(_W_SCALE)[None, :]).astype(MXU_DTYPE)
    q8, kv4, kvc, cu, gla, misc = _in_proj(x2d, mix_g[None, :], w_all, qconst)

    nsub = seq // CMP_STRIDE
    sub = kvc.reshape(bsz, nsub, CMP_STRIDE, 2, hk, HEAD_DIM).transpose(3, 0, 4, 1, 2, 5)
    sub = sub.reshape(2, bsz, hk, nsub, CMP_STRIDE * HEAD_DIM)
    cmp = _compress(sub, pos_kv, w1_kv, b1_kv, w2_kv)
    ones_row = jnp.ones((bsz, hk, 2, nsub), MXU_DTYPE)
    kc = jnp.concatenate([cmp[0].transpose(0, 1, 3, 2), ones_row,
                          jnp.zeros((bsz, hk, LANES - HEAD_DIM - 2, nsub), MXU_DTYPE)], axis=2)
    ones_col = jnp.ones((bsz, hk, nsub, 1), MXU_DTYPE)
    vc = jnp.concatenate([cmp[1], ones_col, jnp.zeros((bsz, hk, nsub, LANES - HEAD_DIM - 1), MXU_DTYPE)], axis=3)

    kv4 = kv4.reshape(bsz, seq, 4, hk, HEAD_DIM)
    def keys_t(i):
        return kv4[:, :, i].transpose(0, 2, 3, 1)
    def vals(i):
        v = kv4[:, :, i].transpose(0, 2, 1, 3)
        return jnp.concatenate([v, jnp.ones((bsz, hk, seq, 1), MXU_DTYPE),
                                jnp.zeros((bsz, hk, seq, LANES - HEAD_DIM - 1), MXU_DTYPE)], axis=3)
    ones2 = jnp.ones((bsz, hk, 2, seq), MXU_DTYPE)
    zpad = jnp.zeros((bsz, hk, LANES - HEAD_DIM - 2, seq), MXU_DTYPE)
    nbp = agg.shape[1]
    blk_rows = (np.arange(seq)[None, :] // SLC_BLOCK == np.arange(nbp)[:, None])
    blk_rows = jnp.broadcast_to(jnp.asarray(blk_rows, MXU_DTYPE)[None, None], (bsz, hk, nbp, seq))
    ks = jnp.concatenate([keys_t(0), ones2, zpad, blk_rows], axis=2)
    kw = jnp.concatenate([keys_t(2), ones2, zpad], axis=2)
    y_nsa = _nsa(q8.reshape(NSA_HEADS, bsz, seq, LANES), ks, vals(1), kw, vals(3), kc, vc,
                 misc.reshape(hk, bsz, seq, LANES), ac, bm, agg)

    y_conv = _conv(cu.reshape(bsz, seq, C_CU), conv_w, conv_b[None, :], conv_ln_g[None, :], conv_ln_b[None, :])
    y_gla = _gla(gla.reshape(bsz, seq, C_GLA), misc.reshape(hk, bsz, seq, LANES), w_alpha, b_alpha[None, :],
                 gla_g[None, :])

    return _post(x2d, y_nsa.reshape(t, -1), y_conv.reshape(t, -1), y_gla.reshape(t, -1), p2d,
                 w_out.astype(MXU_DTYPE), ffn_g[None, :], w_gate.astype(MXU_DTYPE), w_up.astype(MXU_DTYPE),
                 w_down.astype(MXU_DTYPE), ple_g[None, :], w_pg.astype(MXU_DTYPE), w_pp.astype(MXU_DTYPE),
                 final_g[None, :], final)


def kernel(x, p, rel_bias, mix_norm_g, w_in, w_out, cmp_pos_k, cmp_w1_k, cmp_b1_k, cmp_w2_k, cmp_pos_v, cmp_w1_v, cmp_b1_v, cmp_w2_v, conv_w, conv_b, conv_ln_g, conv_ln_b, gla_w_alpha, gla_b_alpha, gla_norm_g, ffn_norm_g, ffn_w_gate, ffn_w_up, ffn_w_down, ple_norm_g, ple_w_gate, ple_w_proj, final_norm_g):
    bsz, seq, d = x.shape
    depth = w_in.shape[0]
    assert seq % TQ == 0 and seq // SLC_BLOCK <= LANES
    t = bsz * seq
    rel_tabs = _nsa_bias_tables(rel_bias.astype(F32))
    nbp = LANES
    agg = jnp.asarray(_agg_matrix(seq // CMP_STRIDE, nbp), MXU_DTYPE)

    far = rel_bias[REL_BUCKETS - 1].astype(F32)
    f_hi, f_lo = _split_hi_lo(far)
    qconst = jnp.zeros((NSA_HEADS, LANES), F32)
    qconst = qconst.at[:, FAR_LANE].set(f_hi.astype(F32)).at[:, FAR_LANE + 1].set(f_lo.astype(F32))
    qconst = qconst.reshape(1, C_Q)

    x2d = x.reshape(t, d)
    for i in range(depth):
        wa_pad = jnp.zeros((LANES, GLA_KEY_DIM), F32).at[GA_LANE0:GA_LANE0 + GLA_GATE_RANK].set(gla_w_alpha[i])
        conv_w_pad = jnp.zeros((CONV_HALO, CONV_CHANNELS), F32).at[:CONV_WIDTH].set(conv_w[i])
        lw = (mix_norm_g[i], w_in[i], w_out[i],
              jnp.stack([cmp_pos_k[i].reshape(1, -1), cmp_pos_v[i].reshape(1, -1)]),
              jnp.stack([cmp_w1_k[i], cmp_w1_v[i]]).astype(MXU_DTYPE),
              jnp.stack([cmp_b1_k[i][None, :], cmp_b1_v[i][None, :]]),
              jnp.stack([cmp_w2_k[i], cmp_w2_v[i]]).astype(MXU_DTYPE),
              conv_w_pad, conv_b[i], conv_ln_g[i], conv_ln_b[i],
              wa_pad.astype(MXU_DTYPE), gla_b_alpha[i], gla_norm_g[i],
              ffn_norm_g[i], ffn_w_gate[i], ffn_w_up[i], ffn_w_down[i],
              ple_norm_g[i], ple_w_gate[i], ple_w_proj[i], qconst)
        x2d = _layer(x2d, p[i].reshape(t, -1), bsz, seq, rel_tabs, agg, lw, final_norm_g, i == depth - 1)
    return x2d.reshape(bsz, seq, d)
```

```python
import functools
import math

import numpy as np
import jax
import jax.numpy as jnp
from jax import lax
from jax.experimental import pallas as pl
from jax.experimental.pallas import tpu as pltpu

F32 = jnp.float32
MXU_DTYPE = jnp.bfloat16

HEAD_DIM = 64
NSA_HEADS = 8
NSA_KV_HEADS = 2
NSA_GROUP = NSA_HEADS // NSA_KV_HEADS
CMP_BLOCK = 32
CMP_STRIDE = 16
CMP_HIDDEN = 256
SLC_BLOCK = 64
N_SELECT = 16
WINDOW = 512
N_BRANCH = 3
CONV_CHANNELS = 256
CONV_WIDTH = 31
GLA_HEADS = 4
GLA_KEY_DIM = 128
GLA_VALUE_DIM = 256
GLA_GATE_RANK = 16
GLA_TAU = 16.0
GLA_CHUNK = 64
REL_BUCKETS = 32
REL_MAX_DIST = 128
PLE_DIM = 256
EPS = 1e-6
NEG_BIG = -1e30

LANES = 128
VMEM_LIMIT_BYTES = 56 * 1024 * 1024

TM_PROJ = 512
TQ = 256
ROWS = NSA_GROUP * TQ
NEAR_W = 2 * TQ // CMP_STRIDE
TS_CONV = 512
CONV_HALO = 32
TS_GLA = 1024
FF_CHUNKS = 2

C_Q = NSA_HEADS * LANES
C_V = 2 * NSA_KV_HEADS * LANES
C_KT = 2 * NSA_KV_HEADS * HEAD_DIM
C_KVC = 2 * NSA_KV_HEADS * HEAD_DIM
C_CU = 2 * CONV_CHANNELS
C_GLA = 2 * GLA_KEY_DIM + 2 * GLA_VALUE_DIM
C_MISC = NSA_KV_HEADS * LANES
GA_LANE0 = 16
FAR_LANE = HEAD_DIM


def _split_hi_lo(x):
    hi = x.astype(MXU_DTYPE)
    lo = (x - hi.astype(F32)).astype(MXU_DTYPE)
    return hi, lo


def _split3(x):
    hi = x.astype(MXU_DTYPE)
    r = x - hi.astype(F32)
    mid = r.astype(MXU_DTYPE)
    lo = (r - mid.astype(F32)).astype(MXU_DTYPE)
    return hi, mid, lo


def _dot(a, b):
    return jnp.dot(a, b, preferred_element_type=F32)


def _dot_halves(a, b):
    half = a.shape[0] // 2
    return jnp.concatenate([_dot(a[:half], b), _dot(a[half:], b)], axis=0)


def _dot_nt(a, b):
    return lax.dot_general(a, b, (((1,), (1,)), ((), ())), preferred_element_type=F32)


def _dot_tn(a, b):
    return lax.dot_general(a, b, (((0,), (0,)), ((), ())), preferred_element_type=F32)


def _rms(x, g):
    return x * lax.rsqrt(jnp.mean(x * x, axis=-1, keepdims=True) + EPS) * g


def _sigmoid(x):
    return 1.0 / (1.0 + jnp.exp(-x))


def _silu(x):
    return x * _sigmoid(x)


def _params(*sem):
    return pltpu.CompilerParams(dimension_semantics=sem, vmem_limit_bytes=VMEM_LIMIT_BYTES)


def _full(shape):
    nd = len(shape)
    return pl.BlockSpec(shape, lambda *_: (0,) * nd)


def _t5_bucket_np(dist):
    n = np.maximum(dist, 0)
    max_exact = REL_BUCKETS // 2
    nf = np.maximum(n, 1).astype(np.float32)
    large = max_exact + (np.log(nf / np.float32(max_exact)) / np.float32(math.log(REL_MAX_DIST / max_exact))
                         * np.float32(REL_BUCKETS - max_exact)).astype(np.int32)
    large = np.minimum(large, REL_BUCKETS - 1)
    return np.where(n < max_exact, n, large).astype(np.int32)


def _in_proj_columns(in_splits):
    offs = np.concatenate([[0], np.cumsum(in_splits)])
    (o_nq, o_kc, o_vc, o_ks, o_vs, o_kw, o_vw, o_ng, o_cu, o_gq, o_gk, o_gv, o_ga, o_gr) = offs[:-1]
    src, scale = [], []

    def put(cols, s=1.0):
        src.extend(cols)
        scale.extend([s] * len(cols))

    for hd in range(NSA_HEADS):
        put(list(range(o_nq + hd * HEAD_DIM, o_nq + (hd + 1) * HEAD_DIM)), HEAD_DIM ** -0.5)
        put([-1] * (LANES - HEAD_DIM))
    kvw = NSA_KV_HEADS * HEAD_DIM
    for o in (o_vs, o_vw):
        for h in range(NSA_KV_HEADS):
            put(list(range(o + h * HEAD_DIM, o + (h + 1) * HEAD_DIM)))
            put([-1] * (LANES - HEAD_DIM))
    for o in (o_kc, o_vc):
        put(list(range(o, o + kvw)))
    put(list(range(o_cu, o_cu + C_CU)))
    put(list(range(o_gq, o_gq + GLA_KEY_DIM)))
    put(list(range(o_gk, o_gk + GLA_KEY_DIM)))
    put(list(range(o_gv, o_gv + GLA_VALUE_DIM)))
    put(list(range(o_gr, o_gr + GLA_VALUE_DIM)))
    per = NSA_GROUP * N_BRANCH
    for h in range(NSA_KV_HEADS):
        slab = [-1] * LANES
        slab[:per] = list(range(o_ng + h * per, o_ng + (h + 1) * per))
        if h == 0:
            slab[GA_LANE0:GA_LANE0 + GLA_GATE_RANK] = list(range(o_ga, o_ga + GLA_GATE_RANK))
        put(slab)
    key_src = list(range(o_ks, o_ks + kvw)) + list(range(o_kw, o_kw + kvw))
    return np.asarray(src, np.int32), np.asarray(scale, np.float32), np.asarray(key_src, np.int32)


def _column_runs(src, scale):
    runs, i = [], 0
    while i < len(src):
        j = i + 1
        while j < len(src) and scale[j] == scale[i] and (
                (src[i] < 0 and src[j] < 0) or (src[i] >= 0 and src[j] == src[i] + (j - i))):
            j += 1
        runs.append((int(src[i]), j - i, float(scale[i])))
        i = j
    return runs


IN_SPLITS = (
    NSA_HEADS * HEAD_DIM,
    NSA_KV_HEADS * HEAD_DIM, NSA_KV_HEADS * HEAD_DIM,
    NSA_KV_HEADS * HEAD_DIM, NSA_KV_HEADS * HEAD_DIM,
    NSA_KV_HEADS * HEAD_DIM, NSA_KV_HEADS * HEAD_DIM,
    NSA_HEADS * N_BRANCH,
    2 * CONV_CHANNELS,
    GLA_KEY_DIM, GLA_KEY_DIM, GLA_VALUE_DIM,
    GLA_GATE_RANK,
    GLA_VALUE_DIM,
)
_W_SRC, _W_SCALE, _WK_SRC = _in_proj_columns(IN_SPLITS)
_W_RUNS = _column_runs(_W_SRC, _W_SCALE)
_WK_RUNS = _column_runs(_WK_SRC, np.ones_like(_WK_SRC, np.float32))
C_ALL = C_Q + C_V + C_KVC + C_CU + C_GLA + C_MISC
assert _W_SRC.shape[0] == C_ALL and _WK_SRC.shape[0] == C_KT


def _gather_columns(w, runs):
    parts = []
    for start, width, scale in runs:
        if start < 0:
            parts.append(jnp.zeros((w.shape[0], width), w.dtype))
        else:
            part = w[:, start:start + width]
            parts.append(part if scale == 1.0 else part * scale)
    return jnp.concatenate(parts, axis=1)


def _nsa_bias_tables(rel_bias):
    i = np.arange(TQ)[:, None]
    j = np.arange(TQ)[None, :]
    far = rel_bias[REL_BUCKETS - 1]
    tab = rel_bias - far[None, :]

    def rel(dist):
        vals = jnp.take(tab, jnp.asarray(_t5_bucket_np(dist)), axis=0)
        vals = jnp.where(jnp.asarray(dist >= 0)[..., None], vals, NEG_BIG)
        vals = vals.transpose(2, 0, 1).reshape(NSA_KV_HEADS, ROWS, dist.shape[1])
        return vals

    tri = jnp.asarray(np.where(j <= i, NEG_BIG, 0.0).astype(np.float32))
    tri = jnp.broadcast_to(jnp.tile(tri, (NSA_GROUP, 1))[None], (NSA_KV_HEADS, ROWS, TQ))
    bm = jnp.concatenate([tri, rel(i + TQ - j), rel(i - j)], axis=-1)

    m = np.arange(NEAR_W)[None, :]
    near = rel(i - CMP_STRIDE * m + (NEAR_W // 2) * CMP_STRIDE - (CMP_BLOCK - 1))
    hi, lo = _split_hi_lo(near)
    pad = jnp.zeros((NSA_KV_HEADS, ROWS, LANES - 2 * NEAR_W - 1), MXU_DTYPE)
    big = jnp.full((NSA_KV_HEADS, ROWS, 1), NEG_BIG, MXU_DTYPE)
    ac = jnp.concatenate([hi, lo, big, pad], axis=-1)
    return bm, ac


def _agg_matrix(ncp, nbp):
    rs, rc = SLC_BLOCK // CMP_STRIDE, CMP_BLOCK // CMP_STRIDE
    agg = np.zeros((ncp, nbp), np.float32)
    ncmp = ncp - rc + 1
    for jb in range(nbp):
        for mm in range(rs):
            for nn in range(rc):
                idx = jb * rs + mm - nn
                if 0 <= idx < ncmp:
                    agg[idx, jb] += 1.0
    return agg


def _in_proj_kernel(x_ref, g_ref, w_ref, wk_ref, qc_ref, q_ref, ks_ref, kw_ref, vs_ref, vw_ref,
                    kvc_ref, cu_ref, gla_ref, misc_ref, *, tiles_per_seq):
    h = _rms(x_ref[...], g_ref[...]).astype(MXU_DTYPE)
    tm = h.shape[0]
    o = 0
    zq = _dot(h, w_ref[:, o:o + C_Q]) + qc_ref[:, o:o + C_Q]
    for hd in range(NSA_HEADS):
        q_ref[hd] = zq[:, hd * LANES:(hd + 1) * LANES].astype(q_ref.dtype)
    o += C_Q
    zv = _dot(h, w_ref[:, o:o + C_V]) + qc_ref[:, o:o + C_V]
    for hh in range(NSA_KV_HEADS):
        vs_ref[hh] = zv[:, hh * LANES:(hh + 1) * LANES].astype(vs_ref.dtype)
        vw_ref[hh] = zv[:, (NSA_KV_HEADS + hh) * LANES:(NSA_KV_HEADS + hh + 1) * LANES].astype(vw_ref.dtype)
    o += C_V
    kt = _dot_nt(wk_ref[...], h)
    rows = lax.broadcasted_iota(jnp.int32, (LANES - HEAD_DIM, tm), 0)
    ones_rows = jnp.where(rows < 2, 1.0, 0.0).astype(ks_ref.dtype)
    tok = (pl.program_id(0) % tiles_per_seq) * tm + lax.broadcasted_iota(jnp.int32, (LANES, tm), 1)
    blk = lax.broadcasted_iota(jnp.int32, (LANES, tm), 0)
    blk_rows = jnp.where(lax.shift_right_logical(tok, int(math.log2(SLC_BLOCK))) == blk, 1.0, 0.0).astype(ks_ref.dtype)
    for hh in range(NSA_KV_HEADS):
        ks_ref[hh, 0:HEAD_DIM, :] = kt[hh * HEAD_DIM:(hh + 1) * HEAD_DIM].astype(ks_ref.dtype)
        ks_ref[hh, HEAD_DIM:LANES, :] = ones_rows
        ks_ref[hh, LANES:2 * LANES, :] = blk_rows
        kw_ref[hh, 0:HEAD_DIM, :] = kt[(NSA_KV_HEADS + hh) * HEAD_DIM:(NSA_KV_HEADS + hh + 1) * HEAD_DIM].astype(kw_ref.dtype)
        kw_ref[hh, HEAD_DIM:LANES, :] = ones_rows
    kvc_ref[...] = _dot(h, w_ref[:, o:o + C_KVC])
    o += C_KVC
    cu_ref[...] = _dot(h, w_ref[:, o:o + C_CU])
    o += C_CU
    gla_ref[...] = _dot(h, w_ref[:, o:o + C_GLA])
    o += C_GLA
    zm = _dot(h, w_ref[:, o:o + C_MISC])
    for s in range(NSA_KV_HEADS):
        misc_ref[s] = zm[:, s * LANES:(s + 1) * LANES]


def _in_proj(x2d, g, w_all, wk_t, qconst, bsz, seq):
    t, d = x2d.shape
    tm = min(TM_PROJ, seq)
    tps = seq // tm
    hk = NSA_KV_HEADS
    row = lambda i: (i, 0)
    kmap = lambda i: (i // tps, 0, 0, i % tps)
    vmap = lambda i: (i // tps, 0, i % tps, 0)
    return pl.pallas_call(
        functools.partial(_in_proj_kernel, tiles_per_seq=tps),
        grid=(t // tm,),
        in_specs=[pl.BlockSpec((tm, d), row), _full((1, d)), _full((d, C_ALL)), _full((C_KT, d)),
                  _full((1, C_Q + C_V))],
        out_specs=[
            pl.BlockSpec((NSA_HEADS, tm, LANES), lambda i: (0, i, 0)),
            pl.BlockSpec((None, hk, 2 * LANES, tm), kmap),
            pl.BlockSpec((None, hk, LANES, tm), kmap),
            pl.BlockSpec((None, hk, tm, LANES), vmap),
            pl.BlockSpec((None, hk, tm, LANES), vmap),
            pl.BlockSpec((tm, C_KVC), row),
            pl.BlockSpec((tm, C_CU), row),
            pl.BlockSpec((tm, C_GLA), row),
            pl.BlockSpec((NSA_KV_HEADS, tm, LANES), lambda i: (0, i, 0)),
        ],
        out_shape=[
            jax.ShapeDtypeStruct((NSA_HEADS, t, LANES), MXU_DTYPE),
            jax.ShapeDtypeStruct((bsz, hk, 2 * LANES, seq), MXU_DTYPE),
            jax.ShapeDtypeStruct((bsz, hk, LANES, seq), MXU_DTYPE),
            jax.ShapeDtypeStruct((bsz, hk, seq, LANES), MXU_DTYPE),
            jax.ShapeDtypeStruct((bsz, hk, seq, LANES), MXU_DTYPE),
            jax.ShapeDtypeStruct((t, C_KVC), F32),
            jax.ShapeDtypeStruct((t, C_CU), F32),
            jax.ShapeDtypeStruct((t, C_GLA), F32),
            jax.ShapeDtypeStruct((NSA_KV_HEADS, t, LANES), F32),
        ],
        compiler_params=_params("parallel"),
        name="in_proj",
    )(x2d, g, w_all, wk_t, qconst)


def _compress_kernel(sub_ref, pos_ref, w1_ref, b1_ref, w2_ref, o_ref):
    sub = sub_ref[...]
    nc = sub.shape[0]
    half = sub.shape[1]
    top = _dot((sub + pos_ref[:, :half]).astype(MXU_DTYPE), w1_ref[:half, :])
    bot = _dot((sub + pos_ref[:, half:]).astype(MXU_DTYPE), w1_ref[half:, :])
    hid = _silu(top + pltpu.roll(bot, nc - 1, 0) + b1_ref[...])
    o_ref[...] = _dot(hid.astype(MXU_DTYPE), w2_ref[...]).astype(o_ref.dtype)


def _compress(sub, pos, w1, b1, w2):
    _, b, hk, nc, half = sub.shape
    kv = lambda s, bi, h: (s, 0, 0)
    return pl.pallas_call(
        _compress_kernel,
        grid=(2, b, hk),
        in_specs=[
            pl.BlockSpec((None, None, None, nc, half), lambda s, bi, h: (s, bi, h, 0, 0)),
            pl.BlockSpec((None, 1, 2 * half), kv),
            pl.BlockSpec((None, 2 * half, CMP_HIDDEN), kv),
            pl.BlockSpec((None, 1, CMP_HIDDEN), kv),
            pl.BlockSpec((None, CMP_HIDDEN, HEAD_DIM), kv),
        ],
        out_specs=pl.BlockSpec((None, None, None, nc, HEAD_DIM), lambda s, bi, h: (s, bi, h, 0, 0)),
        out_shape=jax.ShapeDtypeStruct((2, b, hk, nc, HEAD_DIM), MXU_DTYPE),
        compiler_params=_params("parallel", "parallel", "parallel"),
        name="compress_kv",
    )(sub, pos, w1, b1, w2)


def _nsa_kernel(*refs):
    c = pl.program_id(2)

    @pl.when(c >= 2)
    def _():
        _nsa_body(True, c, *refs)

    @pl.when(c < 2)
    def _():
        _nsa_body(False, c, *refs)


def _nsa_body(full, c, q_ref, ks_ref, vs_ref, kw_ref, vw_ref, kc_ref, vc_ref, gl_ref, ac_ref, bm_ref, agg_ref,
              o_ref, kcx_scr, qs_scr, m_scr, acc_scr, sa_scr, sb_scr):
    nc = kc_ref.shape[1]
    nb = agg_ref.shape[1]
    q = q_ref[...].reshape(ROWS, LANES)

    rowi = lax.broadcasted_iota(jnp.int32, (LANES, nc), 0)
    coli = lax.broadcasted_iota(jnp.int32, (LANES, nc), 1)
    near0 = c * (TQ // CMP_STRIDE) - NEAR_W // 2
    shift = (coli == near0 + (rowi & (NEAR_W - 1))) & (rowi < 2 * NEAR_W)
    future = (rowi == 2 * NEAR_W) & (coli >= near0 + NEAR_W)
    kcx_scr[0:LANES, :] = kc_ref[...]
    kcx_scr[LANES:2 * LANES, :] = jnp.where(shift | future, 1.0, 0.0).astype(MXU_DTYPE)
    s_c = _dot_halves(jnp.concatenate([q, ac_ref[...]], axis=1), kcx_scr[...])
    m_c = jnp.maximum(jnp.max(s_c, axis=1, keepdims=True), 0.1 * NEG_BIG)
    e_c = jnp.exp(s_c - m_c)
    l_c = jnp.sum(e_c, axis=1, keepdims=True)
    p_c = e_c * (1.0 / jnp.maximum(l_c, 1e-30))
    acc_c = _dot_halves(p_c.astype(MXU_DTYPE), vc_ref[...])

    imp = p_c[0:TQ] + p_c[TQ:2 * TQ] + p_c[2 * TQ:3 * TQ] + p_c[3 * TQ:4 * TQ]
    agg = agg_ref[...]
    i_hi, i_mid, i_lo = _split3(imp)
    imps = _dot(i_hi, agg) + _dot(i_mid, agg) + _dot(i_lo, agg)
    t_q = c * TQ + lax.broadcasted_iota(jnp.int32, (nb, TQ), 1)
    blk = lax.broadcasted_iota(jnp.int32, (nb, TQ), 0)
    cur = lax.shift_right_logical(t_q, int(math.log2(SLC_BLOCK)))
    causal = blk <= cur
    forced = (blk == 0) | (blk == cur) | (blk == cur - 1)
    neg_inf = -jnp.inf
    cand = jnp.where(causal & jnp.logical_not(forced), imps.T, neg_inf)
    sel = forced & causal
    blk_f = blk.astype(F32)
    for _ in range(N_SELECT - 3):
        best = jnp.max(cand, axis=0, keepdims=True)
        first = jnp.min(jnp.where(cand == best, blk_f, float(nb)), axis=0, keepdims=True)
        hit = (blk_f == first) & (best > neg_inf)
        sel = sel | hit
        cand = jnp.where(hit, neg_inf, cand)
    selneg = jnp.where(sel, 0.0, NEG_BIG).T.astype(MXU_DTYPE)
    qs_scr[:, 0:LANES] = q
    qs_scr[:, LANES:2 * LANES] = jnp.concatenate([selneg] * NSA_GROUP, axis=0)

    def flash_init():
        m_scr[...] = jnp.full(m_scr.shape, NEG_BIG, F32)
        acc_scr[...] = jnp.zeros(acc_scr.shape, F32)

    def scores(k_ref, kt):
        start = pl.multiple_of(kt * TQ, TQ)
        return _dot(qs_scr[:, 0:k_ref.shape[0]], k_ref[:, pl.ds(start, TQ)])

    def flash_update(s, v_ref, kt, ntiles=1, dot=_dot):
        start = pl.multiple_of(kt * TQ, TQ)
        m_prev = m_scr[...]
        m_new = jnp.maximum(m_prev, jnp.max(s, axis=1, keepdims=True))
        p = jnp.exp(s - jnp.concatenate([m_new] * (ntiles * TQ // LANES), axis=1))
        acc_scr[...] = jnp.exp(m_prev - m_new) * acc_scr[...] \
            + dot(p.astype(MXU_DTYPE), v_ref[pl.ds(start, ntiles * TQ), :])
        m_scr[...] = m_new

    def flash_step(k_ref, v_ref, kt, bias_tile):
        s = scores(k_ref, kt)
        if bias_tile is not None:
            s = s + bm_ref[:, bias_tile * TQ:(bias_tile + 1) * TQ]
        flash_update(s, v_ref, kt)

    def flash_out():
        acc = acc_scr[...]
        return acc * (1.0 / acc[:, HEAD_DIM:HEAD_DIM + 1])

    if full:
        w0 = pl.multiple_of((c - 2) * TQ, TQ)
        s_w = _dot_halves(q, kw_ref[:, pl.ds(w0, 3 * TQ)]) + bm_ref[...]
        p_w = jnp.exp(s_w - jnp.max(s_w, axis=1, keepdims=True))
        acc_w = _dot_halves(p_w.astype(MXU_DTYPE), vw_ref[pl.ds(w0, 3 * TQ), :])
        o_w = acc_w * (1.0 / acc_w[:, HEAD_DIM:HEAD_DIM + 1])
    else:
        flash_init()

        @pl.when(c >= 1)
        def _():
            flash_step(kw_ref, vw_ref, c - 1, 1)

        flash_step(kw_ref, vw_ref, c, 2)
        o_w = flash_out()

    flash_init()
    if full:
        n_far = c - 1
        n_pairs = lax.shift_right_logical(n_far, 1)

        @pl.when(n_pairs > 0)
        def _():
            sa_scr[...] = scores(ks_ref, 0)

        def far_pair(j, carry):
            sb_scr[...] = scores(ks_ref, 2 * j + 1)
            flash_update(sa_scr[...], vs_ref, 2 * j)
            sa_scr[...] = scores(ks_ref, 2 * j + 2)
            flash_update(sb_scr[...], vs_ref, 2 * j + 1)
            return carry

        lax.fori_loop(0, n_pairs, far_pair, 0)

        @pl.when(n_far > 2 * n_pairs)
        def _():
            flash_step(ks_ref, vs_ref, n_far - 1, None)

        sa_scr[...] = scores(ks_ref, c - 1) + bm_ref[:, TQ:2 * TQ]
        sb_scr[...] = scores(ks_ref, c) + bm_ref[:, 2 * TQ:3 * TQ]
        flash_update(sa_scr[...], vs_ref, c - 1)
        flash_update(sb_scr[...], vs_ref, c)
    else:
        @pl.when(c >= 1)
        def _():
            flash_step(ks_ref, vs_ref, c - 1, 1)

        flash_step(ks_ref, vs_ref, c, 2)
    o_s = flash_out()

    gates = _sigmoid(gl_ref[...])
    lane = lax.broadcasted_iota(jnp.int32, (TQ, LANES), 1)
    outs = []
    for g in range(NSA_GROUP):
        r = slice(g * TQ, (g + 1) * TQ)
        gc = gates[:, g * N_BRANCH + 0:g * N_BRANCH + 1]
        gs = gates[:, g * N_BRANCH + 1:g * N_BRANCH + 2]
        gw = gates[:, g * N_BRANCH + 2:g * N_BRANCH + 3]
        outs.append(gc * acc_c[r] + gs * o_s[r] + gw * o_w[r])
    for pair in range(NSA_GROUP // 2):
        both = jnp.where(lane < HEAD_DIM, outs[2 * pair], pltpu.roll(outs[2 * pair + 1], HEAD_DIM, 1))
        o_ref[:, pair * LANES:(pair + 1) * LANES] = both.astype(o_ref.dtype)


def _nsa(q8, ks, vs, kw, vw, kc, vc, gl, ac, bm, agg):
    _, b, s, _ = q8.shape
    nc = kc.shape[-1]
    nb = agg.shape[1]
    kvmap = lambda bi, h, c: (bi, h, 0, 0)
    hmap = lambda bi, h, c: (h, 0, 0)
    return pl.pallas_call(
        _nsa_kernel,
        grid=(b, NSA_KV_HEADS, s // TQ),
        in_specs=[
            pl.BlockSpec((NSA_GROUP, None, TQ, LANES), lambda bi, h, c: (h, bi, c, 0)),
            pl.BlockSpec((None, None, 2 * LANES, s), kvmap),
            pl.BlockSpec((None, None, s, LANES), kvmap),
            pl.BlockSpec((None, None, LANES, s), kvmap),
            pl.BlockSpec((None, None, s, LANES), kvmap),
            pl.BlockSpec((None, None, LANES, nc), kvmap),
            pl.BlockSpec((None, None, nc, LANES), kvmap),
            pl.BlockSpec((None, None, TQ, LANES), lambda bi, h, c: (h, bi, c, 0)),
            pl.BlockSpec((None, ROWS, LANES), hmap),
            pl.BlockSpec((None, ROWS, 3 * TQ), hmap),
            _full((nc, nb)),
        ],
        out_specs=pl.BlockSpec((None, TQ, NSA_GROUP * HEAD_DIM), lambda bi, h, c: (bi, c, h)),
        out_shape=jax.ShapeDtypeStruct((b, s, NSA_HEADS * HEAD_DIM), MXU_DTYPE),
        scratch_shapes=[
            pltpu.VMEM((2 * LANES, nc), MXU_DTYPE),
            pltpu.VMEM((ROWS, 2 * LANES), MXU_DTYPE),
            pltpu.VMEM((ROWS, LANES), F32),
            pltpu.VMEM((ROWS, LANES), F32),
            pltpu.VMEM((ROWS, TQ), F32),
            pltpu.VMEM((ROWS, TQ), F32),
        ],
        compiler_params=_params("parallel", "parallel", "arbitrary"),
        name="nsa_attention",
    )(q8, ks, vs, kw, vw, kc, vc, gl, ac, bm, agg)


def _conv_kernel(cu_ref, halo_ref, w_ref, b_ref, g_ref, bb_ref, o_ref, hh_scr):
    ts = cu_ref.shape[0]
    ch = CONV_CHANNELS

    def glu(u):
        return u[:, :ch] * _sigmoid(u[:, ch:])

    first = pl.program_id(1) == 0
    hh_scr[0:CONV_HALO, :] = jnp.where(first, 0.0, glu(halo_ref[...]))
    hh_scr[CONV_HALO:CONV_HALO + ts, :] = glu(cu_ref[...])
    acc = jnp.broadcast_to(b_ref[...], (ts, ch))
    for w in range(CONV_WIDTH):
        acc = acc + hh_scr[pl.ds(CONV_HALO - (CONV_WIDTH - 1) + w, ts), :] * w_ref[w:w + 1, :]
    mu = jnp.mean(acc, axis=-1, keepdims=True)
    xc = acc - mu
    y = xc * lax.rsqrt(jnp.mean(xc * xc, axis=-1, keepdims=True) + EPS) * g_ref[...] + bb_ref[...]
    o_ref[...] = _silu(y).astype(o_ref.dtype)


def _conv(cu, w, bias, ln_g, ln_b):
    b, s, _ = cu.shape
    ts = min(TS_CONV, s)
    per = ts // CONV_HALO
    return pl.pallas_call(
        _conv_kernel,
        grid=(b, s // ts),
        in_specs=[
            pl.BlockSpec((None, ts, C_CU), lambda bi, i: (bi, i, 0)),
            pl.BlockSpec((None, CONV_HALO, C_CU), lambda bi, i: (bi, jnp.maximum(i * per - 1, 0), 0)),
            _full((CONV_HALO, CONV_CHANNELS)), _full((1, CONV_CHANNELS)),
            _full((1, CONV_CHANNELS)), _full((1, CONV_CHANNELS)),
        ],
        out_specs=pl.BlockSpec((None, ts, CONV_CHANNELS), lambda bi, i: (bi, i, 0)),
        out_shape=jax.ShapeDtypeStruct((b, s, CONV_CHANNELS), MXU_DTYPE),
        scratch_shapes=[pltpu.VMEM((CONV_HALO + ts, CONV_CHANNELS), F32)],
        compiler_params=_params("parallel", "parallel"),
        name="conformer_conv",
    )(cu, cu, w, bias, ln_g, ln_b)


def _gla_kernel(q_ref, k_ref, v_ref, r_ref, ga_ref, wa_ref, ba_ref, g_ref, tri_ref, hm_ref, vm_ref, bmk_ref,
                gm_ref, o_ref, st_scr, sb_scr, b_scr, u_scr):
    ts = q_ref.shape[0]
    nch = ts // GLA_CHUNK
    dk = GLA_KEY_DIM // GLA_HEADS
    cs = GLA_CHUNK

    @pl.when(pl.program_id(1) == 0)
    def _():
        st_scr[...] = jnp.zeros(st_scr.shape, F32)

    x = _dot(ga_ref[...].astype(MXU_DTYPE), wa_ref[...]) + ba_ref[...]
    log_a = (jnp.minimum(x, 0.0) - jnp.log(1.0 + jnp.exp(-jnp.abs(x)))) * (1.0 / GLA_TAU)
    l_hi, l_mid, l_lo = _split3(log_a)
    tri = tri_ref[...]
    for n in range(nch):
        r = slice(n * cs, (n + 1) * cs)
        b_scr[r, :] = _dot(tri, l_hi[r]) + _dot(tri, l_mid[r]) + _dot(tri, l_lo[r])
    b = b_scr[...]
    eb = jnp.exp(b)
    q_t = (q_ref[...] * (dk ** -0.5) * eb).astype(MXU_DTYPE)
    k_all = k_ref[...]
    k_t = (k_all * jnp.exp(-b)).astype(MXU_DTYPE)
    v_all = v_ref[...].astype(MXU_DTYPE)
    causal = tri_ref[...] > 0
    blockmask = bmk_ref[...]

    for n in range(nch):
        r = slice(n * cs, (n + 1) * cs)
        b_last = b[(n + 1) * cs - 1:(n + 1) * cs, :]
        k_d = (k_all[r] * jnp.exp(b_last - b[r])).astype(MXU_DTYPE)
        u_scr[n] = _dot_tn(v_all[r], k_d) * blockmask
    state = st_scr[...]
    for n in range(nch):
        sb_scr[n] = state.astype(MXU_DTYPE)
        b_last = b[(n + 1) * cs - 1:(n + 1) * cs, :]
        state = state * jnp.exp(b_last) + u_scr[n]
    st_scr[...] = state

    for n in range(nch):
        r = slice(n * cs, (n + 1) * cs)
        qn = q_t[r]
        o = _dot_nt(qn, sb_scr[n])
        for h in range(GLA_HEADS):
            attn = jnp.where(causal, _dot_nt(qn * hm_ref[h:h + 1, :], k_t[r]), 0.0)
            o = o + _dot(attn.astype(MXU_DTYPE), v_all[r]) * vm_ref[h:h + 1, :]
        o2 = o * o
        o_hi, o_mid, o_lo = _split3(o2)
        gm = gm_ref[...]
        ms = _dot(o_hi, gm) + _dot(o_mid, gm) + _dot(o_lo, gm)
        y = o * lax.rsqrt(ms + EPS) * g_ref[...]
        o_ref[r, :] = (y * _silu(r_ref[r, :])).astype(o_ref.dtype)


def _gla(gla, misc, wa_pad, ba, g):
    b, s, _ = gla.shape
    ts = min(TS_GLA, s)
    nch = ts // GLA_CHUNK
    dk, dv = GLA_KEY_DIM // GLA_HEADS, GLA_VALUE_DIM // GLA_HEADS
    tri = np.tril(np.ones((GLA_CHUNK, GLA_CHUNK), np.float32))
    hm = (np.arange(GLA_KEY_DIM)[None, :] // dk == np.arange(GLA_HEADS)[:, None]).astype(np.float32)
    vm = (np.arange(GLA_VALUE_DIM)[None, :] // dv == np.arange(GLA_HEADS)[:, None]).astype(np.float32)
    bmk = (np.arange(GLA_VALUE_DIM)[:, None] // dv == np.arange(GLA_KEY_DIM)[None, :] // dk).astype(np.float32)
    gm = (np.arange(GLA_VALUE_DIM)[:, None] // dv == np.arange(GLA_VALUE_DIM)[None, :] // dv).astype(np.float32) / dv
    return pl.pallas_call(
        _gla_kernel,
        grid=(b, s // ts),
        in_specs=[
            pl.BlockSpec((None, ts, GLA_KEY_DIM), lambda bi, i: (bi, i, 0)),
            pl.BlockSpec((None, ts, GLA_KEY_DIM), lambda bi, i: (bi, i, 1)),
            pl.BlockSpec((None, ts, GLA_VALUE_DIM), lambda bi, i: (bi, i, 1)),
            pl.BlockSpec((None, ts, GLA_VALUE_DIM), lambda bi, i: (bi, i, 2)),
            pl.BlockSpec((None, None, ts, LANES), lambda bi, i: (0, bi, i, 0)),
            _full((LANES, GLA_KEY_DIM)), _full((1, GLA_KEY_DIM)), _full((1, GLA_VALUE_DIM)),
            _full(tri.shape), _full(hm.shape), _full(vm.shape), _full(bmk.shape), _full(gm.shape),
        ],
        out_specs=pl.BlockSpec((None, ts, GLA_VALUE_DIM), lambda bi, i: (bi, i, 0)),
        out_shape=jax.ShapeDtypeStruct((b, s, GLA_VALUE_DIM), MXU_DTYPE),
        scratch_shapes=[
            pltpu.VMEM((GLA_VALUE_DIM, GLA_KEY_DIM), F32),
            pltpu.VMEM((nch, GLA_VALUE_DIM, GLA_KEY_DIM), MXU_DTYPE),
            pltpu.VMEM((ts, GLA_KEY_DIM), F32),
            pltpu.VMEM((nch, GLA_VALUE_DIM, GLA_KEY_DIM), F32),
        ],
        compiler_params=_params("parallel", "arbitrary"),
        name="gla",
    )(gla, gla, gla, gla, misc, wa_pad, ba, g,
      jnp.asarray(tri, MXU_DTYPE), jnp.asarray(hm, MXU_DTYPE), jnp.asarray(vm), jnp.asarray(bmk),
      jnp.asarray(gm, MXU_DTYPE))


def _post_kernel(x_ref, yn_ref, yc_ref, yg_ref, p_ref, wo_ref, gf_ref, wg_ref, wu_ref, wd_ref,
                 gp_ref, wpg_ref, wpp_ref, gfin_ref, o_ref, *, final):
    dn = yn_ref.shape[1]
    dc = yc_ref.shape[1]
    mix = _dot(yn_ref[...], wo_ref[0:dn, :]) + _dot(yc_ref[...], wo_ref[dn:dn + dc, :]) \
        + _dot(yg_ref[...], wo_ref[dn + dc:, :])
    x = x_ref[...] + mix
    h = _rms(x, gf_ref[...]).astype(MXU_DTYPE)
    dff = wg_ref.shape[1]
    step = dff // FF_CHUNKS
    ff = None
    for j in range(FF_CHUNKS):
        cols = slice(j * step, (j + 1) * step)
        act = _silu(_dot(h, wg_ref[:, cols])) * _dot(h, wu_ref[:, cols])
        down = _dot(act.astype(MXU_DTYPE), wd_ref[cols, :])
        ff = down if ff is None else ff + down
    x = x + ff
    gate = _sigmoid(_dot(_rms(x, gp_ref[...]).astype(MXU_DTYPE), wpg_ref[...]))
    x = x + _dot(p_ref[...].astype(MXU_DTYPE), wpp_ref[...]) * gate
    if final:
        x = _rms(x, gfin_ref[...])
    o_ref[...] = x


def _post(x2d, yn, yc, yg, p2d, wo, gf, wg, wu, wd, gp, wpg, wpp, gfin, final):
    t, d = x2d.shape
    tm = min(TM_PROJ, t)
    row = lambda i: (i, 0)
    once = dict(pipeline_mode=pl.Buffered(1))

    def const(a):
        return pl.BlockSpec(a.shape, lambda i: (0,) * a.ndim, **once)

    return pl.pallas_call(
        functools.partial(_post_kernel, final=final),
        grid=(t // tm,),
        in_specs=[
            pl.BlockSpec((tm, d), row),
            pl.BlockSpec((tm, yn.shape[1]), row), pl.BlockSpec((tm, yc.shape[1]), row),
            pl.BlockSpec((tm, yg.shape[1]), row), pl.BlockSpec((tm, p2d.shape[1]), row),
            const(wo), const(gf), const(wg), const(wu), const(wd), const(gp), const(wpg), const(wpp),
            const(gfin),
        ],
        out_specs=pl.BlockSpec((tm, d), row),
        out_shape=jax.ShapeDtypeStruct((t, d), F32),
        compiler_params=_params("parallel"),
        name="post_mixer",
    )(x2d, yn, yc, yg, p2d, wo, gf, wg, wu, wd, gp, wpg, wpp, gfin)


def _layer(x2d, p2d, bsz, seq, rel_tabs, agg, lw, final_g, final):
    (mix_g, w_in, w_out, pos_kv, w1_kv, b1_kv, w2_kv, conv_w, conv_b, conv_ln_g, conv_ln_b,
     w_alpha, b_alpha, gla_g, ffn_g, w_gate, w_up, w_down, ple_g, w_pg, w_pp, qconst) = lw
    bm, ac = rel_tabs
    t = bsz * seq
    hk = NSA_KV_HEADS

    w_all = _gather_columns(w_in, _W_RUNS).astype(MXU_DTYPE)
    wk_t = _gather_columns(w_in, _WK_RUNS).T.astype(MXU_DTYPE)
    q8, ks, kw, vs, vw, kvc, cu, gla, misc = _in_proj(x2d, mix_g[None, :], w_all, wk_t, qconst, bsz, seq)

    nsub = seq // CMP_STRIDE
    sub = kvc.reshape(bsz, nsub, CMP_STRIDE, 2, hk, HEAD_DIM).transpose(3, 0, 4, 1, 2, 5)
    sub = sub.reshape(2, bsz, hk, nsub, CMP_STRIDE * HEAD_DIM)
    cmp = _compress(sub, pos_kv, w1_kv, b1_kv, w2_kv)
    ones_row = jnp.ones((bsz, hk, 2, nsub), MXU_DTYPE)
    kc = jnp.concatenate([cmp[0].transpose(0, 1, 3, 2), ones_row,
                          jnp.zeros((bsz, hk, LANES - HEAD_DIM - 2, nsub), MXU_DTYPE)], axis=2)
    ones_col = jnp.ones((bsz, hk, nsub, 1), MXU_DTYPE)
    vc = jnp.concatenate([cmp[1], ones_col, jnp.zeros((bsz, hk, nsub, LANES - HEAD_DIM - 1), MXU_DTYPE)], axis=3)

    y_nsa = _nsa(q8.reshape(NSA_HEADS, bsz, seq, LANES), ks, vs, kw, vw, kc, vc,
                 misc.reshape(hk, bsz, seq, LANES), ac, bm, agg)

    y_conv = _conv(cu.reshape(bsz, seq, C_CU), conv_w, conv_b[None, :], conv_ln_g[None, :], conv_ln_b[None, :])
    y_gla = _gla(gla.reshape(bsz, seq, C_GLA), misc.reshape(hk, bsz, seq, LANES), w_alpha, b_alpha[None, :],
                 gla_g[None, :])

    return _post(x2d, y_nsa.reshape(t, -1), y_conv.reshape(t, -1), y_gla.reshape(t, -1), p2d,
                 w_out.astype(MXU_DTYPE), ffn_g[None, :], w_gate.astype(MXU_DTYPE), w_up.astype(MXU_DTYPE),
                 w_down.astype(MXU_DTYPE), ple_g[None, :], w_pg.astype(MXU_DTYPE), w_pp.astype(MXU_DTYPE),
                 final_g[None, :], final)


def kernel(x, p, rel_bias, mix_norm_g, w_in, w_out, cmp_pos_k, cmp_w1_k, cmp_b1_k, cmp_w2_k, cmp_pos_v, cmp_w1_v, cmp_b1_v, cmp_w2_v, conv_w, conv_b, conv_ln_g, conv_ln_b, gla_w_alpha, gla_b_alpha, gla_norm_g, ffn_norm_g, ffn_w_gate, ffn_w_up, ffn_w_down, ple_norm_g, ple_w_gate, ple_w_proj, final_norm_g):
    bsz, seq, d = x.shape
    depth = w_in.shape[0]
    assert seq % TQ == 0 and seq // SLC_BLOCK <= LANES
    t = bsz * seq
    rel_tabs = _nsa_bias_tables(rel_bias.astype(F32))
    nbp = LANES
    agg = jnp.asarray(_agg_matrix(seq // CMP_STRIDE, nbp), MXU_DTYPE)

    far = rel_bias[REL_BUCKETS - 1].astype(F32)
    f_hi, f_lo = _split_hi_lo(far)
    qconst = jnp.zeros((NSA_HEADS, LANES), F32)
    qconst = qconst.at[:, FAR_LANE].set(f_hi.astype(F32)).at[:, FAR_LANE + 1].set(f_lo.astype(F32))
    vconst = jnp.zeros((C_V // LANES, LANES), F32).at[:, HEAD_DIM].set(1.0)
    qconst = jnp.concatenate([qconst.reshape(1, C_Q), vconst.reshape(1, C_V)], axis=1)

    x2d = x.reshape(t, d)
    for i in range(depth):
        wa_pad = jnp.zeros((LANES, GLA_KEY_DIM), F32).at[GA_LANE0:GA_LANE0 + GLA_GATE_RANK].set(gla_w_alpha[i])
        conv_w_pad = jnp.zeros((CONV_HALO, CONV_CHANNELS), F32).at[:CONV_WIDTH].set(conv_w[i])
        lw = (mix_norm_g[i], w_in[i], w_out[i],
              jnp.stack([cmp_pos_k[i].reshape(1, -1), cmp_pos_v[i].reshape(1, -1)]),
              jnp.stack([cmp_w1_k[i], cmp_w1_v[i]]).astype(MXU_DTYPE),
              jnp.stack([cmp_b1_k[i][None, :], cmp_b1_v[i][None, :]]),
              jnp.stack([cmp_w2_k[i], cmp_w2_v[i]]).astype(MXU_DTYPE),
              conv_w_pad, conv_b[i], conv_ln_g[i], conv_ln_b[i],
              wa_pad.astype(MXU_DTYPE), gla_b_alpha[i], gla_norm_g[i],
              ffn_norm_g[i], ffn_w_gate[i], ffn_w_up[i], ffn_w_down[i],
              ple_norm_g[i], ple_w_gate[i], ple_w_proj[i], qconst)
        x2d = _layer(x2d, p[i].reshape(t, -1), bsz, seq, rel_tabs, agg, lw, final_norm_g, i == depth - 1)
    return x2d.reshape(bsz, seq, d)
```

```python
import functools
import math

import numpy as np
import jax
import jax.numpy as jnp
from jax import lax
from jax.experimental import pallas as pl
from jax.experimental.pallas import tpu as pltpu

F32 = jnp.float32
MXU_DTYPE = jnp.bfloat16

HEAD_DIM = 64
NSA_HEADS = 8
NSA_KV_HEADS = 2
NSA_GROUP = NSA_HEADS // NSA_KV_HEADS
CMP_BLOCK = 32
CMP_STRIDE = 16
CMP_HIDDEN = 256
SLC_BLOCK = 64
N_SELECT = 16
WINDOW = 512
N_BRANCH = 3
CONV_CHANNELS = 256
CONV_WIDTH = 31
GLA_HEADS = 4
GLA_KEY_DIM = 128
GLA_VALUE_DIM = 256
GLA_GATE_RANK = 16
GLA_TAU = 16.0
GLA_CHUNK = 64
REL_BUCKETS = 32
REL_MAX_DIST = 128
PLE_DIM = 256
EPS = 1e-6
NEG_BIG = -1e30

LANES = 128
SUBLANES = 8
VMEM_LIMIT_BYTES = 56 * 1024 * 1024

TM_PROJ = 512
TQ = 256
ROWS = NSA_GROUP * TQ
NEAR_W = 2 * TQ // CMP_STRIDE
TS_CONV = 512
CONV_HALO = 32
TS_GLA = 1024
FF_CHUNKS = 2

C_Q = NSA_HEADS * LANES
C_V = 2 * NSA_KV_HEADS * LANES
C_KT = 2 * NSA_KV_HEADS * HEAD_DIM
C_KVC = 2 * NSA_KV_HEADS * HEAD_DIM
C_CU = 2 * CONV_CHANNELS
C_GLA = 2 * GLA_KEY_DIM + 2 * GLA_VALUE_DIM
C_MISC = NSA_KV_HEADS * LANES
GA_LANE0 = 16
FAR_LANE = HEAD_DIM


def _split_hi_lo(x):
    hi = x.astype(MXU_DTYPE)
    lo = (x - hi.astype(F32)).astype(MXU_DTYPE)
    return hi, lo


def _split3(x):
    hi = x.astype(MXU_DTYPE)
    r = x - hi.astype(F32)
    mid = r.astype(MXU_DTYPE)
    lo = (r - mid.astype(F32)).astype(MXU_DTYPE)
    return hi, mid, lo


def _dot(a, b):
    return jnp.dot(a, b, preferred_element_type=F32)


def _dot_halves(a, b):
    half = a.shape[0] // 2
    return jnp.concatenate([_dot(a[:half], b), _dot(a[half:], b)], axis=0)


def _dot_nt(a, b):
    return lax.dot_general(a, b, (((1,), (1,)), ((), ())), preferred_element_type=F32)


def _dot_tn(a, b):
    return lax.dot_general(a, b, (((0,), (0,)), ((), ())), preferred_element_type=F32)


def _rms(x, g):
    return x * lax.rsqrt(jnp.mean(x * x, axis=-1, keepdims=True) + EPS) * g


def _sigmoid(x):
    return 1.0 / (1.0 + jnp.exp(-x))


def _silu(x):
    return x * _sigmoid(x)


def _params(*sem):
    return pltpu.CompilerParams(dimension_semantics=sem, vmem_limit_bytes=VMEM_LIMIT_BYTES)


def _full(shape):
    nd = len(shape)
    return pl.BlockSpec(shape, lambda *_: (0,) * nd)


def _t5_bucket_np(dist):
    n = np.maximum(dist, 0)
    max_exact = REL_BUCKETS // 2
    nf = np.maximum(n, 1).astype(np.float32)
    large = max_exact + (np.log(nf / np.float32(max_exact)) / np.float32(math.log(REL_MAX_DIST / max_exact))
                         * np.float32(REL_BUCKETS - max_exact)).astype(np.int32)
    large = np.minimum(large, REL_BUCKETS - 1)
    return np.where(n < max_exact, n, large).astype(np.int32)


def _in_proj_columns(in_splits):
    offs = np.concatenate([[0], np.cumsum(in_splits)])
    (o_nq, o_kc, o_vc, o_ks, o_vs, o_kw, o_vw, o_ng, o_cu, o_gq, o_gk, o_gv, o_ga, o_gr) = offs[:-1]
    src, scale = [], []

    def put(cols, s=1.0):
        src.extend(cols)
        scale.extend([s] * len(cols))

    for hd in range(NSA_HEADS):
        put(list(range(o_nq + hd * HEAD_DIM, o_nq + (hd + 1) * HEAD_DIM)), HEAD_DIM ** -0.5)
        put([-1] * (LANES - HEAD_DIM))
    kvw = NSA_KV_HEADS * HEAD_DIM
    for o in (o_vs, o_vw):
        for h in range(NSA_KV_HEADS):
            put(list(range(o + h * HEAD_DIM, o + (h + 1) * HEAD_DIM)))
            put([-1] * (LANES - HEAD_DIM))
    for o in (o_kc, o_vc):
        put(list(range(o, o + kvw)))
    put(list(range(o_cu, o_cu + C_CU)))
    put(list(range(o_gq, o_gq + GLA_KEY_DIM)))
    put(list(range(o_gk, o_gk + GLA_KEY_DIM)))
    put(list(range(o_gv, o_gv + GLA_VALUE_DIM)))
    put(list(range(o_gr, o_gr + GLA_VALUE_DIM)))
    per = NSA_GROUP * N_BRANCH
    for h in range(NSA_KV_HEADS):
        slab = [-1] * LANES
        slab[:per] = list(range(o_ng + h * per, o_ng + (h + 1) * per))
        if h == 0:
            slab[GA_LANE0:GA_LANE0 + GLA_GATE_RANK] = list(range(o_ga, o_ga + GLA_GATE_RANK))
        put(slab)
    key_src = list(range(o_ks, o_ks + kvw)) + list(range(o_kw, o_kw + kvw))
    return np.asarray(src, np.int32), np.asarray(scale, np.float32), np.asarray(key_src, np.int32)


def _column_runs(src, scale):
    runs, i = [], 0
    while i < len(src):
        j = i + 1
        while j < len(src) and scale[j] == scale[i] and (
                (src[i] < 0 and src[j] < 0) or (src[i] >= 0 and src[j] == src[i] + (j - i))):
            j += 1
        runs.append((int(src[i]), j - i, float(scale[i])))
        i = j
    return runs


IN_SPLITS = (
    NSA_HEADS * HEAD_DIM,
    NSA_KV_HEADS * HEAD_DIM, NSA_KV_HEADS * HEAD_DIM,
    NSA_KV_HEADS * HEAD_DIM, NSA_KV_HEADS * HEAD_DIM,
    NSA_KV_HEADS * HEAD_DIM, NSA_KV_HEADS * HEAD_DIM,
    NSA_HEADS * N_BRANCH,
    2 * CONV_CHANNELS,
    GLA_KEY_DIM, GLA_KEY_DIM, GLA_VALUE_DIM,
    GLA_GATE_RANK,
    GLA_VALUE_DIM,
)
_W_SRC, _W_SCALE, _WK_SRC = _in_proj_columns(IN_SPLITS)
_W_RUNS = _column_runs(_W_SRC, _W_SCALE)
_WK_RUNS = _column_runs(_WK_SRC, np.ones_like(_WK_SRC, np.float32))
C_ALL = C_Q + C_V + C_KVC + C_CU + C_GLA + C_MISC
assert _W_SRC.shape[0] == C_ALL and _WK_SRC.shape[0] == C_KT


def _gather_columns(w, runs):
    parts = []
    for start, width, scale in runs:
        if start < 0:
            parts.append(jnp.zeros((w.shape[0], width), w.dtype))
        else:
            part = w[:, start:start + width]
            parts.append(part if scale == 1.0 else part * scale)
    return jnp.concatenate(parts, axis=1)


def _nsa_bias_tables(rel_bias):
    i = np.arange(TQ)[:, None]
    j = np.arange(TQ)[None, :]
    far = rel_bias[REL_BUCKETS - 1]
    tab = rel_bias - far[None, :]

    d = np.arange(2 * TQ - 1, -2 * TQ - 1, -1)
    by_dist = jnp.where(jnp.asarray(d >= 0)[:, None], jnp.take(tab, jnp.asarray(_t5_bucket_np(d)), axis=0), NEG_BIG).T

    def toeplitz(dist0, step, width):
        rows = [lax.slice_in_dim(by_dist, 2 * TQ - 1 - (dist0 + q), 2 * TQ - 1 - (dist0 + q) + step * (width - 1) + 1,
                                 stride=step, axis=1) for q in range(TQ)]
        vals = jnp.stack(rows, axis=1)
        return vals.reshape(NSA_KV_HEADS, ROWS, width)

    tri = jnp.asarray(np.where(j <= i, NEG_BIG, 0.0).astype(np.float32))
    tri = jnp.broadcast_to(jnp.tile(tri, (NSA_GROUP, 1))[None], (NSA_KV_HEADS, ROWS, TQ))
    bm = jnp.concatenate([tri, toeplitz(TQ, 1, 2 * TQ)], axis=-1)

    near = toeplitz((NEAR_W // 2) * CMP_STRIDE - (CMP_BLOCK - 1), CMP_STRIDE, NEAR_W)
    hi, lo = _split_hi_lo(near)
    pad = jnp.zeros((NSA_KV_HEADS, ROWS, LANES - 2 * NEAR_W - 1), MXU_DTYPE)
    big = jnp.full((NSA_KV_HEADS, ROWS, 1), NEG_BIG, MXU_DTYPE)
    ac = jnp.concatenate([hi, lo, big, pad], axis=-1)
    return bm, ac


def _agg_matrix(ncp, nbp):
    rs, rc = SLC_BLOCK // CMP_STRIDE, CMP_BLOCK // CMP_STRIDE
    agg = np.zeros((ncp, nbp), np.float32)
    ncmp = ncp - rc + 1
    for jb in range(nbp):
        for mm in range(rs):
            for nn in range(rc):
                idx = jb * rs + mm - nn
                if 0 <= idx < ncmp:
                    agg[idx, jb] += 1.0
    return agg


def _in_proj_kernel(x_ref, g_ref, w_ref, wk_ref, qc_ref, q_ref, ks_ref, kw_ref, vs_ref, vw_ref,
                    kvc_ref, cu_ref, gla_ref, misc_ref, *, tiles_per_seq):
    h = _rms(x_ref[...], g_ref[...]).astype(MXU_DTYPE)
    tm = h.shape[0]
    o = 0
    zq = _dot(h, w_ref[:, o:o + C_Q]) + qc_ref[:, o:o + C_Q]
    for hd in range(NSA_HEADS):
        q_ref[hd] = zq[:, hd * LANES:(hd + 1) * LANES].astype(q_ref.dtype)
    o += C_Q
    zv = _dot(h, w_ref[:, o:o + C_V]) + qc_ref[:, o:o + C_V]
    for hh in range(NSA_KV_HEADS):
        vs_ref[hh] = zv[:, hh * LANES:(hh + 1) * LANES].astype(vs_ref.dtype)
        vw_ref[hh] = zv[:, (NSA_KV_HEADS + hh) * LANES:(NSA_KV_HEADS + hh + 1) * LANES].astype(vw_ref.dtype)
    o += C_V
    kt = _dot_nt(wk_ref[...], h)
    rows = lax.broadcasted_iota(jnp.int32, (LANES - HEAD_DIM, tm), 0)
    ones_rows = jnp.where(rows < 2, 1.0, 0.0).astype(ks_ref.dtype)
    tok = (pl.program_id(0) % tiles_per_seq) * tm + lax.broadcasted_iota(jnp.int32, (LANES, tm), 1)
    blk = lax.broadcasted_iota(jnp.int32, (LANES, tm), 0)
    blk_rows = jnp.where(lax.shift_right_logical(tok, int(math.log2(SLC_BLOCK))) == blk, 1.0, 0.0).astype(ks_ref.dtype)
    for hh in range(NSA_KV_HEADS):
        ks_ref[hh, 0:HEAD_DIM, :] = kt[hh * HEAD_DIM:(hh + 1) * HEAD_DIM].astype(ks_ref.dtype)
        ks_ref[hh, HEAD_DIM:LANES, :] = ones_rows
        ks_ref[hh, LANES:2 * LANES, :] = blk_rows
        kw_ref[hh, 0:HEAD_DIM, :] = kt[(NSA_KV_HEADS + hh) * HEAD_DIM:(NSA_KV_HEADS + hh + 1) * HEAD_DIM].astype(kw_ref.dtype)
        kw_ref[hh, HEAD_DIM:LANES, :] = ones_rows
    kvc_ref[...] = _dot(h, w_ref[:, o:o + C_KVC])
    o += C_KVC
    cu_ref[...] = _dot(h, w_ref[:, o:o + C_CU])
    o += C_CU
    gla_ref[...] = _dot(h, w_ref[:, o:o + C_GLA])
    o += C_GLA
    zm = _dot(h, w_ref[:, o:o + C_MISC])
    for s in range(NSA_KV_HEADS):
        misc_ref[s] = zm[:, s * LANES:(s + 1) * LANES]


def _in_proj(x2d, g, w_all, wk_t, qconst, bsz, seq):
    t, d = x2d.shape
    tm = min(TM_PROJ, seq)
    tps = seq // tm
    hk = NSA_KV_HEADS
    row = lambda i: (i, 0)
    kmap = lambda i: (i // tps, 0, 0, i % tps)
    vmap = lambda i: (i // tps, 0, i % tps, 0)
    return pl.pallas_call(
        functools.partial(_in_proj_kernel, tiles_per_seq=tps),
        grid=(t // tm,),
        in_specs=[pl.BlockSpec((tm, d), row), _full((1, d)), _full((d, C_ALL)), _full((C_KT, d)),
                  _full((1, C_Q + C_V))],
        out_specs=[
            pl.BlockSpec((NSA_HEADS, tm, LANES), lambda i: (0, i, 0)),
            pl.BlockSpec((None, hk, 2 * LANES, tm), kmap),
            pl.BlockSpec((None, hk, LANES, tm), kmap),
            pl.BlockSpec((None, hk, tm, LANES), vmap),
            pl.BlockSpec((None, hk, tm, LANES), vmap),
            pl.BlockSpec((tm, C_KVC), row),
            pl.BlockSpec((tm, C_CU), row),
            pl.BlockSpec((tm, C_GLA), row),
            pl.BlockSpec((NSA_KV_HEADS, tm, LANES), lambda i: (0, i, 0)),
        ],
        out_shape=[
            jax.ShapeDtypeStruct((NSA_HEADS, t, LANES), MXU_DTYPE),
            jax.ShapeDtypeStruct((bsz, hk, 2 * LANES, seq), MXU_DTYPE),
            jax.ShapeDtypeStruct((bsz, hk, LANES, seq), MXU_DTYPE),
            jax.ShapeDtypeStruct((bsz, hk, seq, LANES), MXU_DTYPE),
            jax.ShapeDtypeStruct((bsz, hk, seq, LANES), MXU_DTYPE),
            jax.ShapeDtypeStruct((t, C_KVC), F32),
            jax.ShapeDtypeStruct((t, C_CU), F32),
            jax.ShapeDtypeStruct((t, C_GLA), F32),
            jax.ShapeDtypeStruct((NSA_KV_HEADS, t, LANES), F32),
        ],
        compiler_params=_params("parallel"),
        name="in_proj",
    )(x2d, g, w_all, wk_t, qconst)


def _compress_kernel(x_ref, pos_ref, w1_ref, b1_ref, w2_ref, o_ref):
    nc = o_ref.shape[0]
    top = bot = None
    for l in range(CMP_STRIDE):
        x = x_ref[pl.ds(l, nc, stride=CMP_STRIDE), :]
        t = _dot((x + pos_ref[l:l + 1, :]).astype(MXU_DTYPE), w1_ref[l])
        u = _dot((x + pos_ref[CMP_STRIDE + l:CMP_STRIDE + l + 1, :]).astype(MXU_DTYPE), w1_ref[CMP_STRIDE + l])
        top = t if top is None else top + t
        bot = u if bot is None else bot + u
    hid = _silu(top + pltpu.roll(bot, nc - 1, 0) + b1_ref[...])
    o_ref[...] = _dot(hid.astype(MXU_DTYPE), w2_ref[...]).astype(o_ref.dtype)


def _compress(kvc, pos, w1, b1, w2):
    b, s, _ = kvc.shape
    nc = s // CMP_STRIDE
    hw = NSA_KV_HEADS * HEAD_DIM
    hid = NSA_KV_HEADS * CMP_HIDDEN
    return pl.pallas_call(
        _compress_kernel,
        grid=(2, b),
        in_specs=[
            pl.BlockSpec((None, s, hw), lambda kv, bi: (bi, 0, kv)),
            pl.BlockSpec((None, CMP_BLOCK, hw), lambda kv, bi: (kv, 0, 0)),
            pl.BlockSpec((None, CMP_BLOCK, hw, hid), lambda kv, bi: (kv, 0, 0, 0)),
            pl.BlockSpec((None, 1, hid), lambda kv, bi: (kv, 0, 0)),
            pl.BlockSpec((None, hid, hw), lambda kv, bi: (kv, 0, 0)),
        ],
        out_specs=pl.BlockSpec((None, None, nc, hw), lambda kv, bi: (kv, bi, 0, 0)),
        out_shape=jax.ShapeDtypeStruct((2, b, nc, hw), MXU_DTYPE),
        compiler_params=_params("parallel", "parallel"),
        name="compress_kv",
    )(kvc, pos, w1, b1, w2)


def _block_diag2(w):
    z = jnp.zeros_like(w)
    return jnp.concatenate([jnp.concatenate([w, z], axis=-1), jnp.concatenate([z, w], axis=-1)], axis=-2)


def _nsa_kernel(*refs):
    c = pl.program_id(2)

    @pl.when(c >= 2)
    def _():
        _nsa_body(True, c, *refs)

    @pl.when(c < 2)
    def _():
        _nsa_body(False, c, *refs)


def _nsa_body(full, c, q_ref, ks_ref, vs_ref, kw_ref, vw_ref, kc_ref, vc_ref, gl_ref, ac_ref, bm_ref, agg_ref,
              o_ref, kcx_scr, qs_scr, m_scr, acc_scr, sa_scr, sb_scr):
    nc = kc_ref.shape[1]
    nb = agg_ref.shape[1]
    q = q_ref[...].reshape(ROWS, LANES)

    rowi = lax.broadcasted_iota(jnp.int32, (LANES, nc), 0)
    coli = lax.broadcasted_iota(jnp.int32, (LANES, nc), 1)
    near0 = c * (TQ // CMP_STRIDE) - NEAR_W // 2
    shift = (coli == near0 + (rowi & (NEAR_W - 1))) & (rowi < 2 * NEAR_W)
    future = (rowi == 2 * NEAR_W) & (coli >= near0 + NEAR_W)
    kcx_scr[0:LANES, :] = kc_ref[...]
    kcx_scr[LANES:2 * LANES, :] = jnp.where(shift | future, 1.0, 0.0).astype(MXU_DTYPE)
    s_c = _dot_halves(jnp.concatenate([q, ac_ref[...]], axis=1), kcx_scr[...])
    m_c = jnp.maximum(jnp.max(s_c, axis=1, keepdims=True), 0.1 * NEG_BIG)
    e_c = jnp.exp(s_c - m_c)
    l_c = jnp.sum(e_c, axis=1, keepdims=True)
    p_c = e_c * (1.0 / jnp.maximum(l_c, 1e-30))
    acc_c = _dot_halves(p_c.astype(MXU_DTYPE), vc_ref[...])

    imp = p_c[0:TQ] + p_c[TQ:2 * TQ] + p_c[2 * TQ:3 * TQ] + p_c[3 * TQ:4 * TQ]
    agg = agg_ref[...]
    i_hi, i_mid, i_lo = _split3(imp)
    imps = _dot(i_hi, agg) + _dot(i_mid, agg) + _dot(i_lo, agg)
    t_q = c * TQ + lax.broadcasted_iota(jnp.int32, (nb, TQ), 1)
    blk = lax.broadcasted_iota(jnp.int32, (nb, TQ), 0)
    cur = lax.shift_right_logical(t_q, int(math.log2(SLC_BLOCK)))
    causal = blk <= cur
    forced = (blk == 0) | (blk == cur) | (blk == cur - 1)
    neg_inf = -jnp.inf
    cand = jnp.where(causal & jnp.logical_not(forced), imps.T, neg_inf)
    sel = forced & causal
    blk_f = blk.astype(F32)
    for _ in range(N_SELECT - 3):
        best = jnp.max(cand, axis=0, keepdims=True)
        first = jnp.min(jnp.where(cand == best, blk_f, float(nb)), axis=0, keepdims=True)
        hit = (blk_f == first) & (best > neg_inf)
        sel = sel | hit
        cand = jnp.where(hit, neg_inf, cand)
    selneg = jnp.where(sel, 0.0, NEG_BIG).T.astype(MXU_DTYPE)
    qs_scr[:, 0:LANES] = q
    qs_scr[:, LANES:2 * LANES] = jnp.concatenate([selneg] * NSA_GROUP, axis=0)

    def flash_init():
        m_scr[...] = jnp.full(m_scr.shape, NEG_BIG, F32)
        acc_scr[...] = jnp.zeros(acc_scr.shape, F32)

    def scores(k_ref, kt):
        start = pl.multiple_of(kt * TQ, TQ)
        return _dot(qs_scr[:, 0:k_ref.shape[0]], k_ref[:, pl.ds(start, TQ)])

    def flash_update(s, v_ref, kt, ntiles=1, dot=_dot):
        start = pl.multiple_of(kt * TQ, TQ)
        m_prev = m_scr[...]
        m_new = jnp.maximum(m_prev, jnp.max(s, axis=1, keepdims=True))
        p = jnp.exp(s - jnp.concatenate([m_new] * (ntiles * TQ // LANES), axis=1))
        acc_scr[...] = jnp.exp(m_prev - m_new) * acc_scr[...] \
            + dot(p.astype(MXU_DTYPE), v_ref[pl.ds(start, ntiles * TQ), :])
        m_scr[...] = m_new

    def flash_step(k_ref, v_ref, kt, bias_tile):
        s = scores(k_ref, kt)
        if bias_tile is not None:
            s = s + bm_ref[:, bias_tile * TQ:(bias_tile + 1) * TQ]
        flash_update(s, v_ref, kt)

    def flash_out():
        acc = acc_scr[...]
        return acc * (1.0 / acc[:, HEAD_DIM:HEAD_DIM + 1])

    if full:
        w0 = pl.multiple_of((c - 2) * TQ, TQ)
        s_w = _dot_halves(q, kw_ref[:, pl.ds(w0, 3 * TQ)]) + bm_ref[...]
        p_w = jnp.exp(s_w - jnp.max(s_w, axis=1, keepdims=True))
        acc_w = _dot_halves(p_w.astype(MXU_DTYPE), vw_ref[pl.ds(w0, 3 * TQ), :])
        o_w = acc_w * (1.0 / acc_w[:, HEAD_DIM:HEAD_DIM + 1])
    else:
        flash_init()

        @pl.when(c >= 1)
        def _():
            flash_step(kw_ref, vw_ref, c - 1, 1)

        flash_step(kw_ref, vw_ref, c, 2)
        o_w = flash_out()

    flash_init()
    if full:
        n_far = c - 1
        n_pairs = lax.shift_right_logical(n_far, 1)

        @pl.when(n_pairs > 0)
        def _():
            sa_scr[...] = scores(ks_ref, 0)

        def far_pair(j, carry):
            sb_scr[...] = scores(ks_ref, 2 * j + 1)
            flash_update(sa_scr[...], vs_ref, 2 * j)
            sa_scr[...] = scores(ks_ref, 2 * j + 2)
            flash_update(sb_scr[...], vs_ref, 2 * j + 1)
            return carry

        lax.fori_loop(0, n_pairs, far_pair, 0)

        @pl.when(n_far > 2 * n_pairs)
        def _():
            flash_step(ks_ref, vs_ref, n_far - 1, None)

        sa_scr[...] = scores(ks_ref, c - 1) + bm_ref[:, TQ:2 * TQ]
        sb_scr[...] = scores(ks_ref, c) + bm_ref[:, 2 * TQ:3 * TQ]
        flash_update(sa_scr[...], vs_ref, c - 1)
        flash_update(sb_scr[...], vs_ref, c)
    else:
        @pl.when(c >= 1)
        def _():
            flash_step(ks_ref, vs_ref, c - 1, 1)

        flash_step(ks_ref, vs_ref, c, 2)
    o_s = flash_out()

    gates = _sigmoid(gl_ref[...])
    lane = lax.broadcasted_iota(jnp.int32, (TQ, LANES), 1)
    outs = []
    for g in range(NSA_GROUP):
        r = slice(g * TQ, (g + 1) * TQ)
        gc = gates[:, g * N_BRANCH + 0:g * N_BRANCH + 1]
        gs = gates[:, g * N_BRANCH + 1:g * N_BRANCH + 2]
        gw = gates[:, g * N_BRANCH + 2:g * N_BRANCH + 3]
        outs.append(gc * acc_c[r] + gs * o_s[r] + gw * o_w[r])
    for pair in range(NSA_GROUP // 2):
        both = jnp.where(lane < HEAD_DIM, outs[2 * pair], pltpu.roll(outs[2 * pair + 1], HEAD_DIM, 1))
        o_ref[:, pair * LANES:(pair + 1) * LANES] = both.astype(o_ref.dtype)


def _nsa(q8, ks, vs, kw, vw, kc, vc, gl, ac, bm, agg):
    _, b, s, _ = q8.shape
    nc = kc.shape[-1]
    nb = agg.shape[1]
    kvmap = lambda bi, h, c: (bi, h, 0, 0)
    hmap = lambda bi, h, c: (h, 0, 0)
    return pl.pallas_call(
        _nsa_kernel,
        grid=(b, NSA_KV_HEADS, s // TQ),
        in_specs=[
            pl.BlockSpec((NSA_GROUP, None, TQ, LANES), lambda bi, h, c: (h, bi, c, 0)),
            pl.BlockSpec((None, None, 2 * LANES, s), kvmap),
            pl.BlockSpec((None, None, s, LANES), kvmap),
            pl.BlockSpec((None, None, LANES, s), kvmap),
            pl.BlockSpec((None, None, s, LANES), kvmap),
            pl.BlockSpec((None, None, LANES, nc), kvmap),
            pl.BlockSpec((None, None, nc, LANES), kvmap),
            pl.BlockSpec((None, None, TQ, LANES), lambda bi, h, c: (h, bi, c, 0)),
            pl.BlockSpec((None, ROWS, LANES), hmap),
            pl.BlockSpec((None, ROWS, 3 * TQ), hmap),
            _full((nc, nb)),
        ],
        out_specs=pl.BlockSpec((None, TQ, NSA_GROUP * HEAD_DIM), lambda bi, h, c: (bi, c, h)),
        out_shape=jax.ShapeDtypeStruct((b, s, NSA_HEADS * HEAD_DIM), MXU_DTYPE),
        scratch_shapes=[
            pltpu.VMEM((2 * LANES, nc), MXU_DTYPE),
            pltpu.VMEM((ROWS, 2 * LANES), MXU_DTYPE),
            pltpu.VMEM((ROWS, LANES), F32),
            pltpu.VMEM((ROWS, LANES), F32),
            pltpu.VMEM((ROWS, TQ), F32),
            pltpu.VMEM((ROWS, TQ), F32),
        ],
        compiler_params=_params("parallel", "parallel", "arbitrary"),
        name="nsa_attention",
    )(q8, ks, vs, kw, vw, kc, vc, gl, ac, bm, agg)


def _conv_kernel(cu_ref, halo_ref, w_ref, b_ref, g_ref, bb_ref, o_ref, hh_scr, sh_scr):
    ts = cu_ref.shape[0]
    ch = CONV_CHANNELS

    def glu(u):
        return u[:, :ch] * _sigmoid(u[:, ch:])

    first = pl.program_id(1) == 0
    hh_scr[0:CONV_HALO, :] = jnp.where(first, 0.0, glu(halo_ref[...]))
    hh_scr[CONV_HALO:CONV_HALO + ts, :] = glu(cu_ref[...])
    hh_scr[CONV_HALO + ts:, :] = jnp.zeros((SUBLANES, ch), F32)
    acc = jnp.broadcast_to(b_ref[...], (ts, ch))
    lead = CONV_HALO - (CONV_WIDTH - 1)
    for shift in range(SUBLANES):
        sh_scr[...] = hh_scr[pl.ds(shift, CONV_HALO + ts), :]
        for start in range(shift, lead + CONV_WIDTH, SUBLANES):
            w = start - lead
            if 0 <= w < CONV_WIDTH:
                acc = acc + sh_scr[pl.ds(start - shift, ts), :] * w_ref[w:w + 1, :]
    mu = jnp.mean(acc, axis=-1, keepdims=True)
    xc = acc - mu
    y = xc * lax.rsqrt(jnp.mean(xc * xc, axis=-1, keepdims=True) + EPS) * g_ref[...] + bb_ref[...]
    o_ref[...] = _silu(y).astype(o_ref.dtype)


def _conv(cu, w, bias, ln_g, ln_b):
    b, s, _ = cu.shape
    ts = min(TS_CONV, s)
    per = ts // CONV_HALO
    return pl.pallas_call(
        _conv_kernel,
        grid=(b, s // ts),
        in_specs=[
            pl.BlockSpec((None, ts, C_CU), lambda bi, i: (bi, i, 0)),
            pl.BlockSpec((None, CONV_HALO, C_CU), lambda bi, i: (bi, jnp.maximum(i * per - 1, 0), 0)),
            _full((CONV_HALO, CONV_CHANNELS)), _full((1, CONV_CHANNELS)),
            _full((1, CONV_CHANNELS)), _full((1, CONV_CHANNELS)),
        ],
        out_specs=pl.BlockSpec((None, ts, CONV_CHANNELS), lambda bi, i: (bi, i, 0)),
        out_shape=jax.ShapeDtypeStruct((b, s, CONV_CHANNELS), MXU_DTYPE),
        scratch_shapes=[pltpu.VMEM((CONV_HALO + ts + SUBLANES, CONV_CHANNELS), F32),
                        pltpu.VMEM((CONV_HALO + ts, CONV_CHANNELS), F32)],
        compiler_params=_params("parallel", "parallel"),
        name="conformer_conv",
    )(cu, cu, w, bias, ln_g, ln_b)


def _gla_kernel(q_ref, k_ref, v_ref, r_ref, ga_ref, wa_ref, ba_ref, g_ref, tri_ref, hm_ref, vm_ref, bmk_ref,
                gm_ref, o_ref, st_scr, sb_scr, o_scr, u_scr):
    ts = q_ref.shape[0]
    nch = ts // GLA_CHUNK
    dk = GLA_KEY_DIM // GLA_HEADS
    cs = GLA_CHUNK

    @pl.when(pl.program_id(1) == 0)
    def _():
        st_scr[...] = jnp.zeros(st_scr.shape, F32)

    x = _dot(ga_ref[...].astype(MXU_DTYPE), wa_ref[...]) + ba_ref[...]
    log_a = (jnp.minimum(x, 0.0) - jnp.log(1.0 + jnp.exp(-jnp.abs(x)))) * (1.0 / GLA_TAU)
    wide = jnp.concatenate([log_a[n * cs:(n + 1) * cs] for n in range(nch)], axis=1)
    l_hi, l_mid, l_lo = _split3(wide)
    tri = tri_ref[...]
    b_wide = _dot(tri, l_hi) + _dot(tri, l_mid) + _dot(tri, l_lo)
    b = jnp.concatenate([b_wide[:, n * GLA_KEY_DIM:(n + 1) * GLA_KEY_DIM] for n in range(nch)], axis=0)
    eb = jnp.exp(b)
    q_t = (q_ref[...] * (dk ** -0.5) * eb).astype(MXU_DTYPE)
    k_all = k_ref[...]
    k_t = (k_all * jnp.exp(-b)).astype(MXU_DTYPE)
    v_all = v_ref[...].astype(MXU_DTYPE)
    causal = tri_ref[...] > 0
    blockmask = bmk_ref[...]

    for n in range(nch):
        r = slice(n * cs, (n + 1) * cs)
        b_last = b[(n + 1) * cs - 1:(n + 1) * cs, :]
        k_d = (k_all[r] * jnp.exp(b_last - b[r])).astype(MXU_DTYPE)
        u_scr[n] = _dot_tn(v_all[r], k_d) * blockmask
    state = st_scr[...]
    for n in range(nch):
        sb_scr[n] = state.astype(MXU_DTYPE)
        b_last = b[(n + 1) * cs - 1:(n + 1) * cs, :]
        state = state * jnp.exp(b_last) + u_scr[n]
    st_scr[...] = state

    causal_h = jnp.concatenate([causal] * GLA_HEADS, axis=0)
    for n in range(nch):
        r = slice(n * cs, (n + 1) * cs)
        qn = q_t[r]
        o = _dot_nt(qn, sb_scr[n])
        q_heads = jnp.concatenate([qn * hm_ref[h:h + 1, :] for h in range(GLA_HEADS)], axis=0)
        attn = jnp.where(causal_h, _dot_nt(q_heads, k_t[r]), 0.0).astype(MXU_DTYPE)
        pv = _dot(attn, v_all[r])
        for h in range(GLA_HEADS):
            o = o + pv[h * cs:(h + 1) * cs] * vm_ref[h:h + 1, :]
        o_scr[r, :] = o
    o = o_scr[...]
    o_hi, o_mid, o_lo = _split3(o * o)
    gm = gm_ref[...]
    ms = _dot(o_hi, gm) + _dot(o_mid, gm) + _dot(o_lo, gm)
    y = o * lax.rsqrt(ms + EPS) * g_ref[...]
    o_ref[...] = (y * _silu(r_ref[...])).astype(o_ref.dtype)


def _gla(gla, misc, wa_pad, ba, g):
    b, s, _ = gla.shape
    ts = min(TS_GLA, s)
    nch = ts // GLA_CHUNK
    dk, dv = GLA_KEY_DIM // GLA_HEADS, GLA_VALUE_DIM // GLA_HEADS
    tri = np.tril(np.ones((GLA_CHUNK, GLA_CHUNK), np.float32))
    hm = (np.arange(GLA_KEY_DIM)[None, :] // dk == np.arange(GLA_HEADS)[:, None]).astype(np.float32)
    vm = (np.arange(GLA_VALUE_DIM)[None, :] // dv == np.arange(GLA_HEADS)[:, None]).astype(np.float32)
    bmk = (np.arange(GLA_VALUE_DIM)[:, None] // dv == np.arange(GLA_KEY_DIM)[None, :] // dk).astype(np.float32)
    gm = (np.arange(GLA_VALUE_DIM)[:, None] // dv == np.arange(GLA_VALUE_DIM)[None, :] // dv).astype(np.float32) / dv
    return pl.pallas_call(
        _gla_kernel,
        grid=(b, s // ts),
        in_specs=[
            pl.BlockSpec((None, ts, GLA_KEY_DIM), lambda bi, i: (bi, i, 0)),
            pl.BlockSpec((None, ts, GLA_KEY_DIM), lambda bi, i: (bi, i, 1)),
            pl.BlockSpec((None, ts, GLA_VALUE_DIM), lambda bi, i: (bi, i, 1)),
            pl.BlockSpec((None, ts, GLA_VALUE_DIM), lambda bi, i: (bi, i, 2)),
            pl.BlockSpec((None, None, ts, LANES), lambda bi, i: (0, bi, i, 0)),
            _full((LANES, GLA_KEY_DIM)), _full((1, GLA_KEY_DIM)), _full((1, GLA_VALUE_DIM)),
            _full(tri.shape), _full(hm.shape), _full(vm.shape), _full(bmk.shape), _full(gm.shape),
        ],
        out_specs=pl.BlockSpec((None, ts, GLA_VALUE_DIM), lambda bi, i: (bi, i, 0)),
        out_shape=jax.ShapeDtypeStruct((b, s, GLA_VALUE_DIM), MXU_DTYPE),
        scratch_shapes=[
            pltpu.VMEM((GLA_VALUE_DIM, GLA_KEY_DIM), F32),
            pltpu.VMEM((nch, GLA_VALUE_DIM, GLA_KEY_DIM), MXU_DTYPE),
            pltpu.VMEM((ts, GLA_VALUE_DIM), F32),
            pltpu.VMEM((nch, GLA_VALUE_DIM, GLA_KEY_DIM), F32),
        ],
        compiler_params=_params("parallel", "arbitrary"),
        name="gla",
    )(gla, gla, gla, gla, misc, wa_pad, ba, g,
      jnp.asarray(tri, MXU_DTYPE), jnp.asarray(hm, MXU_DTYPE), jnp.asarray(vm), jnp.asarray(bmk),
      jnp.asarray(gm, MXU_DTYPE))


def _post_kernel(x_ref, yn_ref, yc_ref, yg_ref, p_ref, wo_ref, gf_ref, wg_ref, wu_ref, wd_ref,
                 gp_ref, wpg_ref, wpp_ref, gfin_ref, o_ref, *, final):
    dn = yn_ref.shape[1]
    dc = yc_ref.shape[1]
    mix = _dot(yn_ref[...], wo_ref[0:dn, :]) + _dot(yc_ref[...], wo_ref[dn:dn + dc, :]) \
        + _dot(yg_ref[...], wo_ref[dn + dc:, :])
    x = x_ref[...] + mix
    h = _rms(x, gf_ref[...]).astype(MXU_DTYPE)
    dff = wg_ref.shape[1]
    step = dff // FF_CHUNKS
    ff = None
    for j in range(FF_CHUNKS):
        cols = slice(j * step, (j + 1) * step)
        act = _silu(_dot(h, wg_ref[:, cols])) * _dot(h, wu_ref[:, cols])
        down = _dot(act.astype(MXU_DTYPE), wd_ref[cols, :])
        ff = down if ff is None else ff + down
    x = x + ff
    gate = _sigmoid(_dot(_rms(x, gp_ref[...]).astype(MXU_DTYPE), wpg_ref[...]))
    x = x + _dot(p_ref[...].astype(MXU_DTYPE), wpp_ref[...]) * gate
    if final:
        x = _rms(x, gfin_ref[...])
    o_ref[...] = x


def _post(x2d, yn, yc, yg, p2d, wo, gf, wg, wu, wd, gp, wpg, wpp, gfin, final):
    t, d = x2d.shape
    tm = min(TM_PROJ, t)
    row = lambda i: (i, 0)
    once = dict(pipeline_mode=pl.Buffered(1))

    def const(a):
        return pl.BlockSpec(a.shape, lambda i: (0,) * a.ndim, **once)

    return pl.pallas_call(
        functools.partial(_post_kernel, final=final),
        grid=(t // tm,),
        in_specs=[
            pl.BlockSpec((tm, d), row),
            pl.BlockSpec((tm, yn.shape[1]), row), pl.BlockSpec((tm, yc.shape[1]), row),
            pl.BlockSpec((tm, yg.shape[1]), row), pl.BlockSpec((tm, p2d.shape[1]), row),
            const(wo), const(gf), const(wg), const(wu), const(wd), const(gp), const(wpg), const(wpp),
            const(gfin),
        ],
        out_specs=pl.BlockSpec((tm, d), row),
        out_shape=jax.ShapeDtypeStruct((t, d), F32),
        compiler_params=_params("parallel"),
        name="post_mixer",
    )(x2d, yn, yc, yg, p2d, wo, gf, wg, wu, wd, gp, wpg, wpp, gfin)


def _layer(x2d, p2d, bsz, seq, rel_tabs, agg, lw, final_g, final):
    (mix_g, w_in, w_out, pos_kv, w1_kv, b1_kv, w2_kv, conv_w, conv_b, conv_ln_g, conv_ln_b,
     w_alpha, b_alpha, gla_g, ffn_g, w_gate, w_up, w_down, ple_g, w_pg, w_pp, qconst) = lw
    bm, ac = rel_tabs
    t = bsz * seq
    hk = NSA_KV_HEADS

    w_all = _gather_columns(w_in, _W_RUNS).astype(MXU_DTYPE)
    wk_t = _gather_columns(w_in, _WK_RUNS).T.astype(MXU_DTYPE)
    q8, ks, kw, vs, vw, kvc, cu, gla, misc = _in_proj(x2d, mix_g[None, :], w_all, wk_t, qconst, bsz, seq)

    nsub = seq // CMP_STRIDE
    cmp = _compress(kvc.reshape(bsz, seq, C_KVC), pos_kv, w1_kv, b1_kv, w2_kv)
    cmp = cmp.reshape(2, bsz, nsub, hk, HEAD_DIM)
    ones_row = jnp.ones((bsz, hk, 2, nsub), MXU_DTYPE)
    kc = jnp.concatenate([cmp[0].transpose(0, 2, 3, 1), ones_row,
                          jnp.zeros((bsz, hk, LANES - HEAD_DIM - 2, nsub), MXU_DTYPE)], axis=2)
    ones_col = jnp.ones((bsz, hk, nsub, 1), MXU_DTYPE)
    vc = jnp.concatenate([cmp[1].transpose(0, 2, 1, 3), ones_col,
                          jnp.zeros((bsz, hk, nsub, LANES - HEAD_DIM - 1), MXU_DTYPE)], axis=3)

    y_nsa = _nsa(q8.reshape(NSA_HEADS, bsz, seq, LANES), ks, vs, kw, vw, kc, vc,
                 misc.reshape(hk, bsz, seq, LANES), ac, bm, agg)

    y_conv = _conv(cu.reshape(bsz, seq, C_CU), conv_w, conv_b[None, :], conv_ln_g[None, :], conv_ln_b[None, :])
    y_gla = _gla(gla.reshape(bsz, seq, C_GLA), misc.reshape(hk, bsz, seq, LANES), w_alpha, b_alpha[None, :],
                 gla_g[None, :])

    return _post(x2d, y_nsa.reshape(t, -1), y_conv.reshape(t, -1), y_gla.reshape(t, -1), p2d,
                 w_out.astype(MXU_DTYPE), ffn_g[None, :], w_gate.astype(MXU_DTYPE), w_up.astype(MXU_DTYPE),
                 w_down.astype(MXU_DTYPE), ple_g[None, :], w_pg.astype(MXU_DTYPE), w_pp.astype(MXU_DTYPE),
                 final_g[None, :], final)


def kernel(x, p, rel_bias, mix_norm_g, w_in, w_out, cmp_pos_k, cmp_w1_k, cmp_b1_k, cmp_w2_k, cmp_pos_v, cmp_w1_v, cmp_b1_v, cmp_w2_v, conv_w, conv_b, conv_ln_g, conv_ln_b, gla_w_alpha, gla_b_alpha, gla_norm_g, ffn_norm_g, ffn_w_gate, ffn_w_up, ffn_w_down, ple_norm_g, ple_w_gate, ple_w_proj, final_norm_g):
    bsz, seq, d = x.shape
    depth = w_in.shape[0]
    assert seq % TQ == 0 and seq // SLC_BLOCK <= LANES
    t = bsz * seq
    rel_tabs = _nsa_bias_tables(rel_bias.astype(F32))
    nbp = LANES
    agg = jnp.asarray(_agg_matrix(seq // CMP_STRIDE, nbp), MXU_DTYPE)

    far = rel_bias[REL_BUCKETS - 1].astype(F32)
    f_hi, f_lo = _split_hi_lo(far)
    qconst = jnp.zeros((NSA_HEADS, LANES), F32)
    qconst = qconst.at[:, FAR_LANE].set(f_hi.astype(F32)).at[:, FAR_LANE + 1].set(f_lo.astype(F32))
    vconst = jnp.zeros((C_V // LANES, LANES), F32).at[:, HEAD_DIM].set(1.0)
    qconst = jnp.concatenate([qconst.reshape(1, C_Q), vconst.reshape(1, C_V)], axis=1)

    x2d = x.reshape(t, d)
    for i in range(depth):
        wa_pad = jnp.zeros((LANES, GLA_KEY_DIM), F32).at[GA_LANE0:GA_LANE0 + GLA_GATE_RANK].set(gla_w_alpha[i])
        conv_w_pad = jnp.zeros((CONV_HALO, CONV_CHANNELS), F32).at[:CONV_WIDTH].set(conv_w[i])
        both = lambda a: jnp.concatenate([a] * NSA_KV_HEADS, axis=-1)
        w1_blocks = lambda w: _block_diag2(w.reshape(CMP_BLOCK, HEAD_DIM, CMP_HIDDEN))
        lw = (mix_norm_g[i], w_in[i], w_out[i],
              jnp.stack([both(cmp_pos_k[i]), both(cmp_pos_v[i])]),
              jnp.stack([w1_blocks(cmp_w1_k[i]), w1_blocks(cmp_w1_v[i])]).astype(MXU_DTYPE),
              jnp.stack([both(cmp_b1_k[i])[None, :], both(cmp_b1_v[i])[None, :]]),
              jnp.stack([_block_diag2(cmp_w2_k[i]), _block_diag2(cmp_w2_v[i])]).astype(MXU_DTYPE),
              conv_w_pad, conv_b[i], conv_ln_g[i], conv_ln_b[i],
              wa_pad.astype(MXU_DTYPE), gla_b_alpha[i], gla_norm_g[i],
              ffn_norm_g[i], ffn_w_gate[i], ffn_w_up[i], ffn_w_down[i],
              ple_norm_g[i], ple_w_gate[i], ple_w_proj[i], qconst)
        x2d = _layer(x2d, p[i].reshape(t, -1), bsz, seq, rel_tabs, agg, lw, final_norm_g, i == depth - 1)
    return x2d.reshape(bsz, seq, d)
```

```python
import functools
import math

import numpy as np
import jax
import jax.numpy as jnp
from jax import lax
from jax.experimental import pallas as pl
from jax.experimental.pallas import tpu as pltpu

F32 = jnp.float32
MXU_DTYPE = jnp.bfloat16

HEAD_DIM = 64
NSA_HEADS = 8
NSA_KV_HEADS = 2
NSA_GROUP = NSA_HEADS // NSA_KV_HEADS
CMP_BLOCK = 32
CMP_STRIDE = 16
CMP_HIDDEN = 256
SLC_BLOCK = 64
N_SELECT = 16
WINDOW = 512
N_BRANCH = 3
CONV_CHANNELS = 256
CONV_WIDTH = 31
GLA_HEADS = 4
GLA_KEY_DIM = 128
GLA_VALUE_DIM = 256
GLA_GATE_RANK = 16
GLA_TAU = 16.0
GLA_CHUNK = 64
REL_BUCKETS = 32
REL_MAX_DIST = 128
PLE_DIM = 256
EPS = 1e-6
NEG_BIG = -1e30

LANES = 128
SUBLANES = 8
VMEM_LIMIT_BYTES = 56 * 1024 * 1024

TM_PROJ = 512
TQ = 256
ROWS = NSA_GROUP * TQ
NEAR_W = 2 * TQ // CMP_STRIDE
TS_CONV = 512
CONV_HALO = 32
TS_GLA = 1024
FF_CHUNKS = 2

C_Q = NSA_HEADS * LANES
C_V = 2 * NSA_KV_HEADS * LANES
C_KT = 2 * NSA_KV_HEADS * HEAD_DIM
C_KVC = 2 * NSA_KV_HEADS * HEAD_DIM
C_CU = 2 * CONV_CHANNELS
C_GLA = 2 * GLA_KEY_DIM + 2 * GLA_VALUE_DIM
C_MISC = NSA_KV_HEADS * LANES
GA_LANE0 = 16
FAR_LANE = HEAD_DIM


def _split_hi_lo(x):
    hi = x.astype(MXU_DTYPE)
    lo = (x - hi.astype(F32)).astype(MXU_DTYPE)
    return hi, lo


def _split3(x):
    hi = x.astype(MXU_DTYPE)
    r = x - hi.astype(F32)
    mid = r.astype(MXU_DTYPE)
    lo = (r - mid.astype(F32)).astype(MXU_DTYPE)
    return hi, mid, lo


def _dot(a, b):
    return jnp.dot(a, b, preferred_element_type=F32)


def _dot_halves(a, b):
    half = a.shape[0] // 2
    return jnp.concatenate([_dot(a[:half], b), _dot(a[half:], b)], axis=0)


def _dot_nt(a, b):
    return lax.dot_general(a, b, (((1,), (1,)), ((), ())), preferred_element_type=F32)


def _dot_tn(a, b):
    return lax.dot_general(a, b, (((0,), (0,)), ((), ())), preferred_element_type=F32)


def _rms(x, g):
    return x * lax.rsqrt(jnp.mean(x * x, axis=-1, keepdims=True) + EPS) * g


def _sigmoid(x):
    return 1.0 / (1.0 + jnp.exp(-x))


def _silu(x):
    return x * _sigmoid(x)


def _params(*sem):
    return pltpu.CompilerParams(dimension_semantics=sem, vmem_limit_bytes=VMEM_LIMIT_BYTES)


def _full(shape):
    nd = len(shape)
    return pl.BlockSpec(shape, lambda *_: (0,) * nd)


def _t5_bucket_np(dist):
    n = np.maximum(dist, 0)
    max_exact = REL_BUCKETS // 2
    nf = np.maximum(n, 1).astype(np.float32)
    large = max_exact + (np.log(nf / np.float32(max_exact)) / np.float32(math.log(REL_MAX_DIST / max_exact))
                         * np.float32(REL_BUCKETS - max_exact)).astype(np.int32)
    large = np.minimum(large, REL_BUCKETS - 1)
    return np.where(n < max_exact, n, large).astype(np.int32)


def _in_proj_columns(in_splits):
    offs = np.concatenate([[0], np.cumsum(in_splits)])
    (o_nq, o_kc, o_vc, o_ks, o_vs, o_kw, o_vw, o_ng, o_cu, o_gq, o_gk, o_gv, o_ga, o_gr) = offs[:-1]
    src, scale = [], []

    def put(cols, s=1.0):
        src.extend(cols)
        scale.extend([s] * len(cols))

    for hd in range(NSA_HEADS):
        put(list(range(o_nq + hd * HEAD_DIM, o_nq + (hd + 1) * HEAD_DIM)), HEAD_DIM ** -0.5)
        put([-1] * (LANES - HEAD_DIM))
    kvw = NSA_KV_HEADS * HEAD_DIM
    for o in (o_vs, o_vw):
        for h in range(NSA_KV_HEADS):
            put(list(range(o + h * HEAD_DIM, o + (h + 1) * HEAD_DIM)))
            put([-1] * (LANES - HEAD_DIM))
    for o in (o_kc, o_vc):
        put(list(range(o, o + kvw)))
    put(list(range(o_cu, o_cu + C_CU)))
    put(list(range(o_gq, o_gq + GLA_KEY_DIM)))
    put(list(range(o_gk, o_gk + GLA_KEY_DIM)))
    put(list(range(o_gv, o_gv + GLA_VALUE_DIM)))
    put(list(range(o_gr, o_gr + GLA_VALUE_DIM)))
    per = NSA_GROUP * N_BRANCH
    for h in range(NSA_KV_HEADS):
        slab = [-1] * LANES
        slab[:per] = list(range(o_ng + h * per, o_ng + (h + 1) * per))
        if h == 0:
            slab[GA_LANE0:GA_LANE0 + GLA_GATE_RANK] = list(range(o_ga, o_ga + GLA_GATE_RANK))
        put(slab)
    key_src = list(range(o_ks, o_ks + kvw)) + list(range(o_kw, o_kw + kvw))
    return np.asarray(src, np.int32), np.asarray(scale, np.float32), np.asarray(key_src, np.int32)


def _column_runs(src, scale):
    runs, i = [], 0
    while i < len(src):
        j = i + 1
        while j < len(src) and scale[j] == scale[i] and (
                (src[i] < 0 and src[j] < 0) or (src[i] >= 0 and src[j] == src[i] + (j - i))):
            j += 1
        runs.append((int(src[i]), j - i, float(scale[i])))
        i = j
    return runs


IN_SPLITS = (
    NSA_HEADS * HEAD_DIM,
    NSA_KV_HEADS * HEAD_DIM, NSA_KV_HEADS * HEAD_DIM,
    NSA_KV_HEADS * HEAD_DIM, NSA_KV_HEADS * HEAD_DIM,
    NSA_KV_HEADS * HEAD_DIM, NSA_KV_HEADS * HEAD_DIM,
    NSA_HEADS * N_BRANCH,
    2 * CONV_CHANNELS,
    GLA_KEY_DIM, GLA_KEY_DIM, GLA_VALUE_DIM,
    GLA_GATE_RANK,
    GLA_VALUE_DIM,
)
_W_SRC, _W_SCALE, _WK_SRC = _in_proj_columns(IN_SPLITS)
_W_RUNS = _column_runs(_W_SRC, _W_SCALE)
_WK_RUNS = _column_runs(_WK_SRC, np.ones_like(_WK_SRC, np.float32))
C_ALL = C_Q + C_V + C_KVC + C_CU + C_GLA + C_MISC
assert _W_SRC.shape[0] == C_ALL and _WK_SRC.shape[0] == C_KT


def _gather_columns(w, runs):
    parts = []
    for start, width, scale in runs:
        if start < 0:
            parts.append(jnp.zeros((w.shape[0], width), w.dtype))
        else:
            part = w[:, start:start + width]
            parts.append(part if scale == 1.0 else part * scale)
    return jnp.concatenate(parts, axis=1)


def _nsa_bias_tables(rel_bias):
    i = np.arange(TQ)[:, None]
    j = np.arange(TQ)[None, :]
    far = rel_bias[REL_BUCKETS - 1]
    tab = rel_bias - far[None, :]

    d = np.arange(2 * TQ - 1, -2 * TQ - 1, -1)
    by_dist = jnp.where(jnp.asarray(d >= 0)[:, None], jnp.take(tab, jnp.asarray(_t5_bucket_np(d)), axis=0), NEG_BIG).T

    n = by_dist.shape[1]
    skew = jnp.tile(by_dist, (1, TQ))[:, :TQ * (n - 1)].reshape(NSA_HEADS, TQ, n - 1)

    def toeplitz(dist0, step, width):
        first = 2 * TQ - 1 - dist0
        assert first >= TQ - 1 and first + step * (width - 1) < n - 1
        vals = lax.slice_in_dim(skew, first, first + step * (width - 1) + 1, stride=step, axis=2)
        return vals.reshape(NSA_KV_HEADS, ROWS, width)

    tri = jnp.asarray(np.where(j <= i, NEG_BIG, 0.0).astype(np.float32))
    tri = jnp.broadcast_to(jnp.tile(tri, (NSA_GROUP, 1))[None], (NSA_KV_HEADS, ROWS, TQ))
    bm = jnp.concatenate([tri, toeplitz(TQ, 1, 2 * TQ)], axis=-1)

    near = toeplitz((NEAR_W // 2) * CMP_STRIDE - (CMP_BLOCK - 1), CMP_STRIDE, NEAR_W)
    hi, lo = _split_hi_lo(near)
    pad = jnp.zeros((NSA_KV_HEADS, ROWS, LANES - 2 * NEAR_W - 1), MXU_DTYPE)
    big = jnp.full((NSA_KV_HEADS, ROWS, 1), NEG_BIG, MXU_DTYPE)
    ac = jnp.concatenate([hi, lo, big, pad], axis=-1)
    return bm, ac


def _agg_matrix(ncp, nbp):
    rs, rc = SLC_BLOCK // CMP_STRIDE, CMP_BLOCK // CMP_STRIDE
    agg = np.zeros((ncp, nbp), np.float32)
    ncmp = ncp - rc + 1
    for jb in range(nbp):
        for mm in range(rs):
            for nn in range(rc):
                idx = jb * rs + mm - nn
                if 0 <= idx < ncmp:
                    agg[idx, jb] += 1.0
    return agg


def _in_proj_kernel(x_ref, g_ref, w_ref, wk_ref, qc_ref, q_ref, ks_ref, kw_ref, vs_ref, vw_ref,
                    kvc_ref, cu_ref, gla_ref, misc_ref, *, tiles_per_seq):
    h = _rms(x_ref[...], g_ref[...]).astype(MXU_DTYPE)
    tm = h.shape[0]
    o = 0
    zq = _dot(h, w_ref[:, o:o + C_Q]) + qc_ref[:, o:o + C_Q]
    for hd in range(NSA_HEADS):
        q_ref[hd] = zq[:, hd * LANES:(hd + 1) * LANES].astype(q_ref.dtype)
    o += C_Q
    zv = _dot(h, w_ref[:, o:o + C_V]) + qc_ref[:, o:o + C_V]
    for hh in range(NSA_KV_HEADS):
        vs_ref[hh] = zv[:, hh * LANES:(hh + 1) * LANES].astype(vs_ref.dtype)
        vw_ref[hh] = zv[:, (NSA_KV_HEADS + hh) * LANES:(NSA_KV_HEADS + hh + 1) * LANES].astype(vw_ref.dtype)
    o += C_V
    kt = _dot_nt(wk_ref[...], h)
    rows = lax.broadcasted_iota(jnp.int32, (LANES - HEAD_DIM, tm), 0)
    ones_rows = jnp.where(rows < 2, 1.0, 0.0).astype(ks_ref.dtype)
    tok = (pl.program_id(0) % tiles_per_seq) * tm + lax.broadcasted_iota(jnp.int32, (LANES, tm), 1)
    blk = lax.broadcasted_iota(jnp.int32, (LANES, tm), 0)
    blk_rows = jnp.where(lax.shift_right_logical(tok, int(math.log2(SLC_BLOCK))) == blk, 1.0, 0.0).astype(ks_ref.dtype)
    for hh in range(NSA_KV_HEADS):
        ks_ref[hh, 0:HEAD_DIM, :] = kt[hh * HEAD_DIM:(hh + 1) * HEAD_DIM].astype(ks_ref.dtype)
        ks_ref[hh, HEAD_DIM:LANES, :] = ones_rows
        ks_ref[hh, LANES:2 * LANES, :] = blk_rows
        kw_ref[hh, 0:HEAD_DIM, :] = kt[(NSA_KV_HEADS + hh) * HEAD_DIM:(NSA_KV_HEADS + hh + 1) * HEAD_DIM].astype(kw_ref.dtype)
        kw_ref[hh, HEAD_DIM:LANES, :] = ones_rows
    kvc_ref[...] = _dot(h, w_ref[:, o:o + C_KVC])
    o += C_KVC
    cu_ref[...] = _dot(h, w_ref[:, o:o + C_CU])
    o += C_CU
    gla_ref[...] = _dot(h, w_ref[:, o:o + C_GLA])
    o += C_GLA
    zm = _dot(h, w_ref[:, o:o + C_MISC])
    for s in range(NSA_KV_HEADS):
        misc_ref[s] = zm[:, s * LANES:(s + 1) * LANES]


def _in_proj(x2d, g, w_all, wk_t, qconst, bsz, seq):
    t, d = x2d.shape
    tm = min(TM_PROJ, seq)
    tps = seq // tm
    hk = NSA_KV_HEADS
    row = lambda i: (i, 0)
    kmap = lambda i: (i // tps, 0, 0, i % tps)
    vmap = lambda i: (i // tps, 0, i % tps, 0)
    return pl.pallas_call(
        functools.partial(_in_proj_kernel, tiles_per_seq=tps),
        grid=(t // tm,),
        in_specs=[pl.BlockSpec((tm, d), row), _full((1, d)), _full((d, C_ALL)), _full((C_KT, d)),
                  _full((1, C_Q + C_V))],
        out_specs=[
            pl.BlockSpec((NSA_HEADS, tm, LANES), lambda i: (0, i, 0)),
            pl.BlockSpec((None, hk, 2 * LANES, tm), kmap),
            pl.BlockSpec((None, hk, LANES, tm), kmap),
            pl.BlockSpec((None, hk, tm, LANES), vmap),
            pl.BlockSpec((None, hk, tm, LANES), vmap),
            pl.BlockSpec((tm, C_KVC), row),
            pl.BlockSpec((tm, C_CU), row),
            pl.BlockSpec((tm, C_GLA), row),
            pl.BlockSpec((NSA_KV_HEADS, tm, LANES), lambda i: (0, i, 0)),
        ],
        out_shape=[
            jax.ShapeDtypeStruct((NSA_HEADS, t, LANES), MXU_DTYPE),
            jax.ShapeDtypeStruct((bsz, hk, 2 * LANES, seq), MXU_DTYPE),
            jax.ShapeDtypeStruct((bsz, hk, LANES, seq), MXU_DTYPE),
            jax.ShapeDtypeStruct((bsz, hk, seq, LANES), MXU_DTYPE),
            jax.ShapeDtypeStruct((bsz, hk, seq, LANES), MXU_DTYPE),
            jax.ShapeDtypeStruct((t, C_KVC), F32),
            jax.ShapeDtypeStruct((t, C_CU), F32),
            jax.ShapeDtypeStruct((t, C_GLA), F32),
            jax.ShapeDtypeStruct((NSA_KV_HEADS, t, LANES), F32),
        ],
        compiler_params=_params("parallel"),
        name="in_proj",
    )(x2d, g, w_all, wk_t, qconst)


def _compress_kernel(x_ref, pos_ref, w1_ref, b1_ref, w2_ref, o_ref):
    nc = o_ref.shape[0]
    top = bot = None
    for l in range(CMP_STRIDE):
        x = x_ref[pl.ds(l, nc, stride=CMP_STRIDE), :]
        t = _dot((x + pos_ref[l:l + 1, :]).astype(MXU_DTYPE), w1_ref[l])
        u = _dot((x + pos_ref[CMP_STRIDE + l:CMP_STRIDE + l + 1, :]).astype(MXU_DTYPE), w1_ref[CMP_STRIDE + l])
        top = t if top is None else top + t
        bot = u if bot is None else bot + u
    hid = _silu(top + pltpu.roll(bot, nc - 1, 0) + b1_ref[...])
    o_ref[...] = _dot(hid.astype(MXU_DTYPE), w2_ref[...]).astype(o_ref.dtype)


def _compress(kvc, pos, w1, b1, w2):
    b, s, _ = kvc.shape
    nc = s // CMP_STRIDE
    hw = NSA_KV_HEADS * HEAD_DIM
    hid = NSA_KV_HEADS * CMP_HIDDEN
    return pl.pallas_call(
        _compress_kernel,
        grid=(2, b),
        in_specs=[
            pl.BlockSpec((None, s, hw), lambda kv, bi: (bi, 0, kv)),
            pl.BlockSpec((None, CMP_BLOCK, hw), lambda kv, bi: (kv, 0, 0)),
            pl.BlockSpec((None, CMP_BLOCK, hw, hid), lambda kv, bi: (kv, 0, 0, 0)),
            pl.BlockSpec((None, 1, hid), lambda kv, bi: (kv, 0, 0)),
            pl.BlockSpec((None, hid, hw), lambda kv, bi: (kv, 0, 0)),
        ],
        out_specs=pl.BlockSpec((None, None, nc, hw), lambda kv, bi: (kv, bi, 0, 0)),
        out_shape=jax.ShapeDtypeStruct((2, b, nc, hw), MXU_DTYPE),
        compiler_params=_params("parallel", "parallel"),
        name="compress_kv",
    )(kvc, pos, w1, b1, w2)


def _block_diag2(w):
    z = jnp.zeros_like(w)
    return jnp.concatenate([jnp.concatenate([w, z], axis=-1), jnp.concatenate([z, w], axis=-1)], axis=-2)


def _nsa_kernel(*refs):
    c = pl.program_id(2)

    @pl.when(c >= 2)
    def _():
        _nsa_body(True, c, *refs)

    @pl.when(c < 2)
    def _():
        _nsa_body(False, c, *refs)


def _nsa_body(full, c, q_ref, ks_ref, vs_ref, kw_ref, vw_ref, kc_ref, vc_ref, gl_ref, ac_ref, bm_ref, agg_ref,
              kp_ref, vt_ref, o_ref, kcx_scr, qs_scr, m_scr, acc_scr, sa_scr, sb_scr,
              mt_scr, acct_scr, sat_scr, sbt_scr):
    nc = kc_ref.shape[1]
    nb = agg_ref.shape[1]
    q = q_ref[...].reshape(ROWS, LANES)

    rowi = lax.broadcasted_iota(jnp.int32, (LANES, nc), 0)
    coli = lax.broadcasted_iota(jnp.int32, (LANES, nc), 1)
    near0 = c * (TQ // CMP_STRIDE) - NEAR_W // 2
    shift = (coli == near0 + (rowi & (NEAR_W - 1))) & (rowi < 2 * NEAR_W)
    future = (rowi == 2 * NEAR_W) & (coli >= near0 + NEAR_W)
    kcx_scr[0:LANES, :] = kc_ref[...]
    kcx_scr[LANES:2 * LANES, :] = jnp.where(shift | future, 1.0, 0.0).astype(MXU_DTYPE)
    s_c = _dot_halves(jnp.concatenate([q, ac_ref[...]], axis=1), kcx_scr[...])
    m_c = jnp.maximum(jnp.max(s_c, axis=1, keepdims=True), 0.1 * NEG_BIG)
    e_c = jnp.exp(s_c - m_c)
    l_c = jnp.sum(e_c, axis=1, keepdims=True)
    p_c = e_c * (1.0 / jnp.maximum(l_c, 1e-30))
    acc_c = _dot_halves(p_c.astype(MXU_DTYPE), vc_ref[...])

    imp = p_c[0:TQ] + p_c[TQ:2 * TQ] + p_c[2 * TQ:3 * TQ] + p_c[3 * TQ:4 * TQ]
    agg = agg_ref[...]
    i_hi, i_mid, i_lo = _split3(imp)
    imps = _dot(i_hi, agg) + _dot(i_mid, agg) + _dot(i_lo, agg)
    t_q = c * TQ + lax.broadcasted_iota(jnp.int32, (nb, TQ), 1)
    blk = lax.broadcasted_iota(jnp.int32, (nb, TQ), 0)
    cur = lax.shift_right_logical(t_q, int(math.log2(SLC_BLOCK)))
    causal = blk <= cur
    forced = (blk == 0) | (blk == cur) | (blk == cur - 1)
    neg_inf = -jnp.inf
    cand = jnp.where(causal & jnp.logical_not(forced), imps.T, neg_inf)
    sel = forced & causal
    blk_f = blk.astype(F32)
    for _ in range(N_SELECT - 3):
        best = jnp.max(cand, axis=0, keepdims=True)
        first = jnp.min(jnp.where(cand == best, blk_f, float(nb)), axis=0, keepdims=True)
        hit = (blk_f == first) & (best > neg_inf)
        sel = sel | hit
        cand = jnp.where(hit, neg_inf, cand)
    selneg = jnp.where(sel, 0.0, NEG_BIG).T.astype(MXU_DTYPE)
    qs_scr[:, 0:LANES] = q
    qs_scr[:, LANES:2 * LANES] = jnp.concatenate([selneg] * NSA_GROUP, axis=0)

    def flash_init():
        m_scr[...] = jnp.full(m_scr.shape, NEG_BIG, F32)
        acc_scr[...] = jnp.zeros(acc_scr.shape, F32)

    def scores(k_ref, kt):
        start = pl.multiple_of(kt * TQ, TQ)
        return _dot(qs_scr[:, 0:k_ref.shape[0]], k_ref[:, pl.ds(start, TQ)])

    def flash_update(s, v_ref, kt, ntiles=1, dot=_dot):
        start = pl.multiple_of(kt * TQ, TQ)
        m_prev = m_scr[...]
        m_new = jnp.maximum(m_prev, jnp.max(s, axis=1, keepdims=True))
        p = jnp.exp(s - jnp.concatenate([m_new] * (ntiles * TQ // LANES), axis=1))
        acc_scr[...] = jnp.exp(m_prev - m_new) * acc_scr[...] \
            + dot(p.astype(MXU_DTYPE), v_ref[pl.ds(start, ntiles * TQ), :])
        m_scr[...] = m_new

    def flash_step(k_ref, v_ref, kt, bias_tile):
        s = scores(k_ref, kt)
        if bias_tile is not None:
            s = s + bm_ref[:, bias_tile * TQ:(bias_tile + 1) * TQ]
        flash_update(s, v_ref, kt)

    def flash_out():
        acc = acc_scr[...]
        return acc * (1.0 / acc[:, HEAD_DIM:HEAD_DIM + 1])

    if full:
        w0 = pl.multiple_of((c - 2) * TQ, TQ)
        s_w = _dot_halves(q, kw_ref[:, pl.ds(w0, 3 * TQ)]) + bm_ref[...]
        p_w = jnp.exp(s_w - jnp.max(s_w, axis=1, keepdims=True))
        acc_w = _dot_halves(p_w.astype(MXU_DTYPE), vw_ref[pl.ds(w0, 3 * TQ), :])
        o_w = acc_w * (1.0 / acc_w[:, HEAD_DIM:HEAD_DIM + 1])
    else:
        flash_init()

        @pl.when(c >= 1)
        def _():
            flash_step(kw_ref, vw_ref, c - 1, 1)

        flash_step(kw_ref, vw_ref, c, 2)
        o_w = flash_out()

    if full:
        n_far = c - 1
        n_pairs = lax.shift_right_logical(n_far, 1)

        def scores_t(kt):
            start = pl.multiple_of(kt * TQ, TQ)
            return _dot_nt(kp_ref[pl.ds(start, TQ), :], qs_scr[...])

        def update_t(s, kt):
            start = pl.multiple_of(kt * TQ, TQ)
            m_prev = mt_scr[...]
            m_new = jnp.maximum(m_prev, jnp.max(s, axis=0, keepdims=True))
            p = jnp.exp(s - m_new)
            acct_scr[...] = jnp.exp(m_prev - m_new) * acct_scr[...] \
                + _dot(vt_ref[:, pl.ds(start, TQ)], p.astype(MXU_DTYPE))
            mt_scr[...] = m_new

        mt_scr[...] = jnp.full(mt_scr.shape, NEG_BIG, F32)
        acct_scr[...] = jnp.zeros(acct_scr.shape, F32)

        @pl.when(n_pairs > 0)
        def _():
            sat_scr[...] = scores_t(0)

        def far_pair(j, carry):
            sbt_scr[...] = scores_t(2 * j + 1)
            update_t(sat_scr[...], 2 * j)
            sat_scr[...] = scores_t(2 * j + 2)
            update_t(sbt_scr[...], 2 * j + 1)
            return carry

        lax.fori_loop(0, n_pairs, far_pair, 0)

        @pl.when(n_far > 2 * n_pairs)
        def _():
            update_t(scores_t(n_far - 1), n_far - 1)

        acc_scr[...] = acct_scr[...].T
        m_scr[...] = jnp.broadcast_to(mt_scr[...], (LANES, ROWS)).T
        sa_scr[...] = scores(ks_ref, c - 1) + bm_ref[:, TQ:2 * TQ]
        sb_scr[...] = scores(ks_ref, c) + bm_ref[:, 2 * TQ:3 * TQ]
        flash_update(sa_scr[...], vs_ref, c - 1)
        flash_update(sb_scr[...], vs_ref, c)
    else:
        flash_init()

        @pl.when(c >= 1)
        def _():
            flash_step(ks_ref, vs_ref, c - 1, 1)

        flash_step(ks_ref, vs_ref, c, 2)
    o_s = flash_out()

    gates = _sigmoid(gl_ref[...])
    lane = lax.broadcasted_iota(jnp.int32, (TQ, LANES), 1)
    outs = []
    for g in range(NSA_GROUP):
        r = slice(g * TQ, (g + 1) * TQ)
        gc = gates[:, g * N_BRANCH + 0:g * N_BRANCH + 1]
        gs = gates[:, g * N_BRANCH + 1:g * N_BRANCH + 2]
        gw = gates[:, g * N_BRANCH + 2:g * N_BRANCH + 3]
        outs.append(gc * acc_c[r] + gs * o_s[r] + gw * o_w[r])
    for pair in range(NSA_GROUP // 2):
        both = jnp.where(lane < HEAD_DIM, outs[2 * pair], pltpu.roll(outs[2 * pair + 1], HEAD_DIM, 1))
        o_ref[:, pair * LANES:(pair + 1) * LANES] = both.astype(o_ref.dtype)


def _nsa(q8, ks, vs, kw, vw, kc, vc, gl, ac, bm, agg):
    _, b, s, _ = q8.shape
    nc = kc.shape[-1]
    nb = agg.shape[1]
    kvmap = lambda bi, h, c: (bi, h, 0, 0)
    hmap = lambda bi, h, c: (h, 0, 0)
    return pl.pallas_call(
        _nsa_kernel,
        grid=(b, NSA_KV_HEADS, s // TQ),
        in_specs=[
            pl.BlockSpec((NSA_GROUP, None, TQ, LANES), lambda bi, h, c: (h, bi, c, 0)),
            pl.BlockSpec((None, None, 2 * LANES, s), kvmap),
            pl.BlockSpec((None, None, s, LANES), kvmap),
            pl.BlockSpec((None, None, LANES, s), kvmap),
            pl.BlockSpec((None, None, s, LANES), kvmap),
            pl.BlockSpec((None, None, LANES, nc), kvmap),
            pl.BlockSpec((None, None, nc, LANES), kvmap),
            pl.BlockSpec((None, None, TQ, LANES), lambda bi, h, c: (h, bi, c, 0)),
            pl.BlockSpec((None, ROWS, LANES), hmap),
            pl.BlockSpec((None, ROWS, 3 * TQ), hmap),
            _full((nc, nb)),
            pl.BlockSpec((None, None, s, 2 * LANES), kvmap, pipeline_mode=pl.Buffered(1)),
            pl.BlockSpec((None, None, LANES, s), kvmap, pipeline_mode=pl.Buffered(1)),
        ],
        out_specs=pl.BlockSpec((None, TQ, NSA_GROUP * HEAD_DIM), lambda bi, h, c: (bi, c, h)),
        out_shape=jax.ShapeDtypeStruct((b, s, NSA_HEADS * HEAD_DIM), MXU_DTYPE),
        scratch_shapes=[
            pltpu.VMEM((2 * LANES, nc), MXU_DTYPE),
            pltpu.VMEM((ROWS, 2 * LANES), MXU_DTYPE),
            pltpu.VMEM((ROWS, LANES), F32),
            pltpu.VMEM((ROWS, LANES), F32),
            pltpu.VMEM((ROWS, TQ), F32),
            pltpu.VMEM((ROWS, TQ), F32),
            pltpu.VMEM((1, ROWS), F32),
            pltpu.VMEM((LANES, ROWS), F32),
            pltpu.VMEM((TQ, ROWS), F32),
            pltpu.VMEM((TQ, ROWS), F32),
        ],
        compiler_params=_params("parallel", "parallel", "arbitrary"),
        name="nsa_attention",
    )(q8, ks, vs, kw, vw, kc, vc, gl, ac, bm, agg, jnp.swapaxes(ks, 2, 3), jnp.swapaxes(vs, 2, 3))


def _nsa2_kernel(*refs):
    c = pl.program_id(1)

    @pl.when(c >= 2)
    def _():
        _nsa2_body(True, c, *refs)

    @pl.when(c < 2)
    def _():
        _nsa2_body(False, c, *refs)


def _nsa2_body(full, c, q_ref, ks_ref, vs_ref, kw_ref, vw_ref, kc_ref, vc_ref, gl_ref, ac_ref, bm_ref, agg_ref,
               o_ref, kcx_scr, qs_scr, m_scr, acc_scr, sa_scr, sb_scr):
    heads = range(NSA_KV_HEADS)
    nc = kc_ref.shape[2]
    nb = agg_ref.shape[1]
    q = [q_ref[h * NSA_GROUP:(h + 1) * NSA_GROUP].reshape(ROWS, LANES) for h in heads]

    rowi = lax.broadcasted_iota(jnp.int32, (LANES, nc), 0)
    coli = lax.broadcasted_iota(jnp.int32, (LANES, nc), 1)
    near0 = c * (TQ // CMP_STRIDE) - NEAR_W // 2
    shift = (coli == near0 + (rowi & (NEAR_W - 1))) & (rowi < 2 * NEAR_W)
    future = (rowi == 2 * NEAR_W) & (coli >= near0 + NEAR_W)
    near_rows = jnp.where(shift | future, 1.0, 0.0).astype(MXU_DTYPE)
    acc_c, imp = [], []
    for h in heads:
        kcx_scr[h, 0:LANES, :] = kc_ref[h]
        kcx_scr[h, LANES:2 * LANES, :] = near_rows
        s_c = _dot_halves(jnp.concatenate([q[h], ac_ref[h]], axis=1), kcx_scr[h])
        m_c = jnp.maximum(jnp.max(s_c, axis=1, keepdims=True), 0.1 * NEG_BIG)
        e_c = jnp.exp(s_c - m_c)
        l_c = jnp.sum(e_c, axis=1, keepdims=True)
        p_c = e_c * (1.0 / jnp.maximum(l_c, 1e-30))
        acc_c.append(_dot_halves(p_c.astype(MXU_DTYPE), vc_ref[h]))
        imp.append(p_c[0:TQ] + p_c[TQ:2 * TQ] + p_c[2 * TQ:3 * TQ] + p_c[3 * TQ:4 * TQ])

    agg = agg_ref[...]
    t_q = c * TQ + lax.broadcasted_iota(jnp.int32, (nb, TQ), 1)
    blk = lax.broadcasted_iota(jnp.int32, (nb, TQ), 0)
    cur = lax.shift_right_logical(t_q, int(math.log2(SLC_BLOCK)))
    causal = blk <= cur
    forced = (blk == 0) | (blk == cur) | (blk == cur - 1)
    neg_inf = -jnp.inf
    blk_f = blk.astype(F32)
    cand, sel = [], []
    for h in heads:
        i_hi, i_mid, i_lo = _split3(imp[h])
        imps = _dot(i_hi, agg) + _dot(i_mid, agg) + _dot(i_lo, agg)
        cand.append(jnp.where(causal & jnp.logical_not(forced), imps.T, neg_inf))
        sel.append(forced & causal)
    for _ in range(N_SELECT - 3):
        for h in heads:
            best = jnp.max(cand[h], axis=0, keepdims=True)
            first = jnp.min(jnp.where(cand[h] == best, blk_f, float(nb)), axis=0, keepdims=True)
            hit = (blk_f == first) & (best > neg_inf)
            sel[h] = sel[h] | hit
            cand[h] = jnp.where(hit, neg_inf, cand[h])
    for h in heads:
        selneg = jnp.where(sel[h], 0.0, NEG_BIG).T.astype(MXU_DTYPE)
        qs_scr[h, :, 0:LANES] = q[h]
        qs_scr[h, :, LANES:2 * LANES] = jnp.concatenate([selneg] * NSA_GROUP, axis=0)

    def flash_init():
        m_scr[...] = jnp.full(m_scr.shape, NEG_BIG, F32)
        acc_scr[...] = jnp.zeros(acc_scr.shape, F32)

    def scores(h, k_ref, kt):
        start = pl.multiple_of(kt * TQ, TQ)
        return _dot(qs_scr[h, :, 0:k_ref.shape[1]], k_ref[h, :, pl.ds(start, TQ)])

    def flash_update(h, s, v_ref, kt):
        start = pl.multiple_of(kt * TQ, TQ)
        m_prev = m_scr[h]
        m_new = jnp.maximum(m_prev, jnp.max(s, axis=1, keepdims=True))
        p = jnp.exp(s - jnp.concatenate([m_new] * (TQ // LANES), axis=1))
        acc_scr[h] = jnp.exp(m_prev - m_new) * acc_scr[h] + _dot(p.astype(MXU_DTYPE), v_ref[h, pl.ds(start, TQ), :])
        m_scr[h] = m_new

    def flash_step(k_ref, v_ref, kt, bias_tile):
        for h in heads:
            s = scores(h, k_ref, kt)
            if bias_tile is not None:
                s = s + bm_ref[h, :, bias_tile * TQ:(bias_tile + 1) * TQ]
            flash_update(h, s, v_ref, kt)

    def flash_out(h):
        acc = acc_scr[h]
        return acc * (1.0 / acc[:, HEAD_DIM:HEAD_DIM + 1])

    if full:
        w0 = pl.multiple_of((c - 2) * TQ, TQ)
        o_w = []
        for h in heads:
            s_w = _dot_halves(q[h], kw_ref[h, :, pl.ds(w0, 3 * TQ)]) + bm_ref[h]
            p_w = jnp.exp(s_w - jnp.max(s_w, axis=1, keepdims=True))
            acc_w = _dot_halves(p_w.astype(MXU_DTYPE), vw_ref[h, pl.ds(w0, 3 * TQ), :])
            o_w.append(acc_w * (1.0 / acc_w[:, HEAD_DIM:HEAD_DIM + 1]))
    else:
        flash_init()

        @pl.when(c >= 1)
        def _():
            flash_step(kw_ref, vw_ref, c - 1, 1)

        flash_step(kw_ref, vw_ref, c, 2)
        o_w = [flash_out(h) for h in heads]

    flash_init()
    if full:
        n_far = c - 1
        n_pairs = lax.shift_right_logical(n_far, 1)

        @pl.when(n_pairs > 0)
        def _():
            for h in heads:
                sa_scr[h] = scores(h, ks_ref, 0)

        def far_pair(j, carry):
            for h in heads:
                sb_scr[h] = scores(h, ks_ref, 2 * j + 1)
            for h in heads:
                flash_update(h, sa_scr[h], vs_ref, 2 * j)
            for h in heads:
                sa_scr[h] = scores(h, ks_ref, 2 * j + 2)
            for h in heads:
                flash_update(h, sb_scr[h], vs_ref, 2 * j + 1)
            return carry

        lax.fori_loop(0, n_pairs, far_pair, 0)

        @pl.when(n_far > 2 * n_pairs)
        def _():
            flash_step(ks_ref, vs_ref, n_far - 1, None)

        for h in heads:
            sa_scr[h] = scores(h, ks_ref, c - 1) + bm_ref[h, :, TQ:2 * TQ]
        for h in heads:
            sb_scr[h] = scores(h, ks_ref, c) + bm_ref[h, :, 2 * TQ:3 * TQ]
        for h in heads:
            flash_update(h, sa_scr[h], vs_ref, c - 1)
        for h in heads:
            flash_update(h, sb_scr[h], vs_ref, c)
    else:
        @pl.when(c >= 1)
        def _():
            flash_step(ks_ref, vs_ref, c - 1, 1)

        flash_step(ks_ref, vs_ref, c, 2)

    lane = lax.broadcasted_iota(jnp.int32, (TQ, LANES), 1)
    for h in heads:
        o_s = flash_out(h)
        gates = _sigmoid(gl_ref[h])
        outs = []
        for g in range(NSA_GROUP):
            r = slice(g * TQ, (g + 1) * TQ)
            gc = gates[:, g * N_BRANCH + 0:g * N_BRANCH + 1]
            gs = gates[:, g * N_BRANCH + 1:g * N_BRANCH + 2]
            gw = gates[:, g * N_BRANCH + 2:g * N_BRANCH + 3]
            outs.append(gc * acc_c[h][r] + gs * o_s[r] + gw * o_w[h][r])
        for pair in range(NSA_GROUP // 2):
            both = jnp.where(lane < HEAD_DIM, outs[2 * pair], pltpu.roll(outs[2 * pair + 1], HEAD_DIM, 1))
            col = (h * (NSA_GROUP // 2) + pair) * LANES
            o_ref[:, col:col + LANES] = both.astype(o_ref.dtype)


def _nsa2(q8, ks, vs, kw, vw, kc, vc, gl, ac, bm, agg):
    _, b, s, _ = q8.shape
    nc = kc.shape[-1]
    nb = agg.shape[1]
    hk = NSA_KV_HEADS
    once = dict(pipeline_mode=pl.Buffered(1))
    per_batch = lambda bi, c: (bi, 0, 0, 0)
    const3 = lambda bi, c: (0, 0, 0)
    return pl.pallas_call(
        _nsa2_kernel,
        grid=(b, s // TQ),
        in_specs=[
            pl.BlockSpec((NSA_HEADS, None, TQ, LANES), lambda bi, c: (0, bi, c, 0)),
            pl.BlockSpec((None, hk, 2 * LANES, s), per_batch, **once),
            pl.BlockSpec((None, hk, s, LANES), per_batch, **once),
            pl.BlockSpec((None, hk, LANES, s), per_batch, **once),
            pl.BlockSpec((None, hk, s, LANES), per_batch, **once),
            pl.BlockSpec((None, hk, LANES, nc), per_batch, **once),
            pl.BlockSpec((None, hk, nc, LANES), per_batch, **once),
            pl.BlockSpec((hk, None, TQ, LANES), lambda bi, c: (0, bi, c, 0)),
            pl.BlockSpec((hk, ROWS, LANES), const3, **once),
            pl.BlockSpec((hk, ROWS, 3 * TQ), const3, **once),
            pl.BlockSpec((nc, nb), lambda bi, c: (0, 0), **once),
        ],
        out_specs=pl.BlockSpec((None, TQ, NSA_HEADS * HEAD_DIM), lambda bi, c: (bi, c, 0)),
        out_shape=jax.ShapeDtypeStruct((b, s, NSA_HEADS * HEAD_DIM), MXU_DTYPE),
        scratch_shapes=[
            pltpu.VMEM((hk, 2 * LANES, nc), MXU_DTYPE),
            pltpu.VMEM((hk, ROWS, 2 * LANES), MXU_DTYPE),
            pltpu.VMEM((hk, ROWS, LANES), F32),
            pltpu.VMEM((hk, ROWS, LANES), F32),
            pltpu.VMEM((hk, ROWS, TQ), F32),
            pltpu.VMEM((hk, ROWS, TQ), F32),
        ],
        compiler_params=_params("parallel", "arbitrary"),
        name="nsa_attention",
    )(q8, ks, vs, kw, vw, kc, vc, gl, ac, bm, agg)


def _conv_kernel(cu_ref, halo_ref, w_ref, b_ref, g_ref, bb_ref, o_ref, hh_scr, sh_scr):
    ts = cu_ref.shape[0]
    ch = CONV_CHANNELS

    def glu(u):
        return u[:, :ch] * _sigmoid(u[:, ch:])

    first = pl.program_id(1) == 0
    hh_scr[0:CONV_HALO, :] = jnp.where(first, 0.0, glu(halo_ref[...]))
    hh_scr[CONV_HALO:CONV_HALO + ts, :] = glu(cu_ref[...])
    hh_scr[CONV_HALO + ts:, :] = jnp.zeros((SUBLANES, ch), F32)
    acc = jnp.broadcast_to(b_ref[...], (ts, ch))
    lead = CONV_HALO - (CONV_WIDTH - 1)
    for shift in range(SUBLANES):
        sh_scr[...] = hh_scr[pl.ds(shift, CONV_HALO + ts), :]
        for start in range(shift, lead + CONV_WIDTH, SUBLANES):
            w = start - lead
            if 0 <= w < CONV_WIDTH:
                acc = acc + sh_scr[pl.ds(start - shift, ts), :] * w_ref[w:w + 1, :]
    mu = jnp.mean(acc, axis=-1, keepdims=True)
    xc = acc - mu
    y = xc * lax.rsqrt(jnp.mean(xc * xc, axis=-1, keepdims=True) + EPS) * g_ref[...] + bb_ref[...]
    o_ref[...] = _silu(y).astype(o_ref.dtype)


def _conv(cu, w, bias, ln_g, ln_b):
    b, s, _ = cu.shape
    ts = min(TS_CONV, s)
    per = ts // CONV_HALO
    return pl.pallas_call(
        _conv_kernel,
        grid=(b, s // ts),
        in_specs=[
            pl.BlockSpec((None, ts, C_CU), lambda bi, i: (bi, i, 0)),
            pl.BlockSpec((None, CONV_HALO, C_CU), lambda bi, i: (bi, jnp.maximum(i * per - 1, 0), 0)),
            _full((CONV_HALO, CONV_CHANNELS)), _full((1, CONV_CHANNELS)),
            _full((1, CONV_CHANNELS)), _full((1, CONV_CHANNELS)),
        ],
        out_specs=pl.BlockSpec((None, ts, CONV_CHANNELS), lambda bi, i: (bi, i, 0)),
        out_shape=jax.ShapeDtypeStruct((b, s, CONV_CHANNELS), MXU_DTYPE),
        scratch_shapes=[pltpu.VMEM((CONV_HALO + ts + SUBLANES, CONV_CHANNELS), F32),
                        pltpu.VMEM((CONV_HALO + ts, CONV_CHANNELS), F32)],
        compiler_params=_params("parallel", "parallel"),
        name="conformer_conv",
    )(cu, cu, w, bias, ln_g, ln_b)


def _gla_kernel(q_ref, k_ref, v_ref, r_ref, ga_ref, wa_ref, ba_ref, g_ref, tri_ref, hm_ref, vm_ref, bmk_ref,
                gm_ref, o_ref, st_scr, sb_scr, o_scr, u_scr):
    ts = q_ref.shape[0]
    nch = ts // GLA_CHUNK
    dk = GLA_KEY_DIM // GLA_HEADS
    cs = GLA_CHUNK

    @pl.when(pl.program_id(1) == 0)
    def _():
        st_scr[...] = jnp.zeros(st_scr.shape, F32)

    x = _dot(ga_ref[...].astype(MXU_DTYPE), wa_ref[...]) + ba_ref[...]
    log_a = (jnp.minimum(x, 0.0) - jnp.log(1.0 + jnp.exp(-jnp.abs(x)))) * (1.0 / GLA_TAU)
    wide = jnp.concatenate([log_a[n * cs:(n + 1) * cs] for n in range(nch)], axis=1)
    l_hi, l_mid, l_lo = _split3(wide)
    tri = tri_ref[...]
    b_wide = _dot(tri, l_hi) + _dot(tri, l_mid) + _dot(tri, l_lo)
    b = jnp.concatenate([b_wide[:, n * GLA_KEY_DIM:(n + 1) * GLA_KEY_DIM] for n in range(nch)], axis=0)
    eb = jnp.exp(b)
    q_t = (q_ref[...] * (dk ** -0.5) * eb).astype(MXU_DTYPE)
    k_all = k_ref[...]
    k_t = (k_all * jnp.exp(-b)).astype(MXU_DTYPE)
    v_all = v_ref[...].astype(MXU_DTYPE)
    causal = tri_ref[...] > 0
    blockmask = bmk_ref[...]

    for n in range(nch):
        r = slice(n * cs, (n + 1) * cs)
        b_last = b[(n + 1) * cs - 1:(n + 1) * cs, :]
        k_d = (k_all[r] * jnp.exp(b_last - b[r])).astype(MXU_DTYPE)
        u_scr[n] = _dot_tn(v_all[r], k_d) * blockmask
    state = st_scr[...]
    for n in range(nch):
        sb_scr[n] = state.astype(MXU_DTYPE)
        b_last = b[(n + 1) * cs - 1:(n + 1) * cs, :]
        state = state * jnp.exp(b_last) + u_scr[n]
    st_scr[...] = state

    causal_h = jnp.concatenate([causal] * GLA_HEADS, axis=0)
    for n in range(nch):
        r = slice(n * cs, (n + 1) * cs)
        qn = q_t[r]
        o = _dot_nt(qn, sb_scr[n])
        q_heads = jnp.concatenate([qn * hm_ref[h:h + 1, :] for h in range(GLA_HEADS)], axis=0)
        attn = jnp.where(causal_h, _dot_nt(q_heads, k_t[r]), 0.0).astype(MXU_DTYPE)
        pv = _dot(attn, v_all[r])
        for h in range(GLA_HEADS):
            o = o + pv[h * cs:(h + 1) * cs] * vm_ref[h:h + 1, :]
        o_scr[r, :] = o
    o = o_scr[...]
    o_hi, o_mid, o_lo = _split3(o * o)
    gm = gm_ref[...]
    ms = _dot(o_hi, gm) + _dot(o_mid, gm) + _dot(o_lo, gm)
    y = o * lax.rsqrt(ms + EPS) * g_ref[...]
    o_ref[...] = (y * _silu(r_ref[...])).astype(o_ref.dtype)


def _gla(gla, misc, wa_pad, ba, g):
    b, s, _ = gla.shape
    ts = min(TS_GLA, s)
    nch = ts // GLA_CHUNK
    dk, dv = GLA_KEY_DIM // GLA_HEADS, GLA_VALUE_DIM // GLA_HEADS
    tri = np.tril(np.ones((GLA_CHUNK, GLA_CHUNK), np.float32))
    hm = (np.arange(GLA_KEY_DIM)[None, :] // dk == np.arange(GLA_HEADS)[:, None]).astype(np.float32)
    vm = (np.arange(GLA_VALUE_DIM)[None, :] // dv == np.arange(GLA_HEADS)[:, None]).astype(np.float32)
    bmk = (np.arange(GLA_VALUE_DIM)[:, None] // dv == np.arange(GLA_KEY_DIM)[None, :] // dk).astype(np.float32)
    gm = (np.arange(GLA_VALUE_DIM)[:, None] // dv == np.arange(GLA_VALUE_DIM)[None, :] // dv).astype(np.float32) / dv
    return pl.pallas_call(
        _gla_kernel,
        grid=(b, s // ts),
        in_specs=[
            pl.BlockSpec((None, ts, GLA_KEY_DIM), lambda bi, i: (bi, i, 0)),
            pl.BlockSpec((None, ts, GLA_KEY_DIM), lambda bi, i: (bi, i, 1)),
            pl.BlockSpec((None, ts, GLA_VALUE_DIM), lambda bi, i: (bi, i, 1)),
            pl.BlockSpec((None, ts, GLA_VALUE_DIM), lambda bi, i: (bi, i, 2)),
            pl.BlockSpec((None, None, ts, LANES), lambda bi, i: (0, bi, i, 0)),
            _full((LANES, GLA_KEY_DIM)), _full((1, GLA_KEY_DIM)), _full((1, GLA_VALUE_DIM)),
            _full(tri.shape), _full(hm.shape), _full(vm.shape), _full(bmk.shape), _full(gm.shape),
        ],
        out_specs=pl.BlockSpec((None, ts, GLA_VALUE_DIM), lambda bi, i: (bi, i, 0)),
        out_shape=jax.ShapeDtypeStruct((b, s, GLA_VALUE_DIM), MXU_DTYPE),
        scratch_shapes=[
            pltpu.VMEM((GLA_VALUE_DIM, GLA_KEY_DIM), F32),
            pltpu.VMEM((nch, GLA_VALUE_DIM, GLA_KEY_DIM), MXU_DTYPE),
            pltpu.VMEM((ts, GLA_VALUE_DIM), F32),
            pltpu.VMEM((nch, GLA_VALUE_DIM, GLA_KEY_DIM), F32),
        ],
        compiler_params=_params("parallel", "arbitrary"),
        name="gla",
    )(gla, gla, gla, gla, misc, wa_pad, ba, g,
      jnp.asarray(tri, MXU_DTYPE), jnp.asarray(hm, MXU_DTYPE), jnp.asarray(vm), jnp.asarray(bmk),
      jnp.asarray(gm, MXU_DTYPE))


def _post_kernel(x_ref, yn_ref, yc_ref, yg_ref, p_ref, wo_ref, gf_ref, wg_ref, wu_ref, wd_ref,
                 gp_ref, wpg_ref, wpp_ref, gfin_ref, o_ref, *, final):
    dn = yn_ref.shape[1]
    dc = yc_ref.shape[1]
    mix = _dot(yn_ref[...], wo_ref[0:dn, :]) + _dot(yc_ref[...], wo_ref[dn:dn + dc, :]) \
        + _dot(yg_ref[...], wo_ref[dn + dc:, :])
    x = x_ref[...] + mix
    h = _rms(x, gf_ref[...]).astype(MXU_DTYPE)
    dff = wg_ref.shape[1]
    step = dff // FF_CHUNKS
    ff = None
    for j in range(FF_CHUNKS):
        cols = slice(j * step, (j + 1) * step)
        act = _silu(_dot(h, wg_ref[:, cols])) * _dot(h, wu_ref[:, cols])
        down = _dot(act.astype(MXU_DTYPE), wd_ref[cols, :])
        ff = down if ff is None else ff + down
    x = x + ff
    gate = _sigmoid(_dot(_rms(x, gp_ref[...]).astype(MXU_DTYPE), wpg_ref[...]))
    x = x + _dot(p_ref[...].astype(MXU_DTYPE), wpp_ref[...]) * gate
    if final:
        x = _rms(x, gfin_ref[...])
    o_ref[...] = x


def _post(x2d, yn, yc, yg, p2d, wo, gf, wg, wu, wd, gp, wpg, wpp, gfin, final):
    t, d = x2d.shape
    tm = min(TM_PROJ, t)
    row = lambda i: (i, 0)
    once = dict(pipeline_mode=pl.Buffered(1))

    def const(a):
        return pl.BlockSpec(a.shape, lambda i: (0,) * a.ndim, **once)

    return pl.pallas_call(
        functools.partial(_post_kernel, final=final),
        grid=(t // tm,),
        in_specs=[
            pl.BlockSpec((tm, d), row),
            pl.BlockSpec((tm, yn.shape[1]), row), pl.BlockSpec((tm, yc.shape[1]), row),
            pl.BlockSpec((tm, yg.shape[1]), row), pl.BlockSpec((tm, p2d.shape[1]), row),
            const(wo), const(gf), const(wg), const(wu), const(wd), const(gp), const(wpg), const(wpp),
            const(gfin),
        ],
        out_specs=pl.BlockSpec((tm, d), row),
        out_shape=jax.ShapeDtypeStruct((t, d), F32),
        compiler_params=_params("parallel"),
        name="post_mixer",
    )(x2d, yn, yc, yg, p2d, wo, gf, wg, wu, wd, gp, wpg, wpp, gfin)


def _layer(x2d, p2d, bsz, seq, rel_tabs, agg, lw, final_g, final):
    (mix_g, w_in, w_out, pos_kv, w1_kv, b1_kv, w2_kv, conv_w, conv_b, conv_ln_g, conv_ln_b,
     w_alpha, b_alpha, gla_g, ffn_g, w_gate, w_up, w_down, ple_g, w_pg, w_pp, qconst) = lw
    bm, ac = rel_tabs
    t = bsz * seq
    hk = NSA_KV_HEADS

    w_all = _gather_columns(w_in, _W_RUNS).astype(MXU_DTYPE)
    wk_t = _gather_columns(w_in, _WK_RUNS).T.astype(MXU_DTYPE)
    q8, ks, kw, vs, vw, kvc, cu, gla, misc = _in_proj(x2d, mix_g[None, :], w_all, wk_t, qconst, bsz, seq)

    nsub = seq // CMP_STRIDE
    cmp = _compress(kvc.reshape(bsz, seq, C_KVC), pos_kv, w1_kv, b1_kv, w2_kv)
    cmp = cmp.reshape(2, bsz, nsub, hk, HEAD_DIM)
    ones_row = jnp.ones((bsz, hk, 2, nsub), MXU_DTYPE)
    kc = jnp.concatenate([cmp[0].transpose(0, 2, 3, 1), ones_row,
                          jnp.zeros((bsz, hk, LANES - HEAD_DIM - 2, nsub), MXU_DTYPE)], axis=2)
    ones_col = jnp.ones((bsz, hk, nsub, 1), MXU_DTYPE)
    vc = jnp.concatenate([cmp[1].transpose(0, 2, 1, 3), ones_col,
                          jnp.zeros((bsz, hk, nsub, LANES - HEAD_DIM - 1), MXU_DTYPE)], axis=3)

    y_nsa = _nsa(q8.reshape(NSA_HEADS, bsz, seq, LANES), ks, vs, kw, vw, kc, vc,
                 misc.reshape(hk, bsz, seq, LANES), ac, bm, agg)

    y_conv = _conv(cu.reshape(bsz, seq, C_CU), conv_w, conv_b[None, :], conv_ln_g[None, :], conv_ln_b[None, :])
    y_gla = _gla(gla.reshape(bsz, seq, C_GLA), misc.reshape(hk, bsz, seq, LANES), w_alpha, b_alpha[None, :],
                 gla_g[None, :])

    return _post(x2d, y_nsa.reshape(t, -1), y_conv.reshape(t, -1), y_gla.reshape(t, -1), p2d,
                 w_out.astype(MXU_DTYPE), ffn_g[None, :], w_gate.astype(MXU_DTYPE), w_up.astype(MXU_DTYPE),
                 w_down.astype(MXU_DTYPE), ple_g[None, :], w_pg.astype(MXU_DTYPE), w_pp.astype(MXU_DTYPE),
                 final_g[None, :], final)


def kernel(x, p, rel_bias, mix_norm_g, w_in, w_out, cmp_pos_k, cmp_w1_k, cmp_b1_k, cmp_w2_k, cmp_pos_v, cmp_w1_v, cmp_b1_v, cmp_w2_v, conv_w, conv_b, conv_ln_g, conv_ln_b, gla_w_alpha, gla_b_alpha, gla_norm_g, ffn_norm_g, ffn_w_gate, ffn_w_up, ffn_w_down, ple_norm_g, ple_w_gate, ple_w_proj, final_norm_g):
    bsz, seq, d = x.shape
    depth = w_in.shape[0]
    assert seq % TQ == 0 and seq // SLC_BLOCK <= LANES
    t = bsz * seq
    rel_tabs = _nsa_bias_tables(rel_bias.astype(F32))
    nbp = LANES
    agg = jnp.asarray(_agg_matrix(seq // CMP_STRIDE, nbp), MXU_DTYPE)

    far = rel_bias[REL_BUCKETS - 1].astype(F32)
    f_hi, f_lo = _split_hi_lo(far)
    qconst = jnp.zeros((NSA_HEADS, LANES), F32)
    qconst = qconst.at[:, FAR_LANE].set(f_hi.astype(F32)).at[:, FAR_LANE + 1].set(f_lo.astype(F32))
    vconst = jnp.zeros((C_V // LANES, LANES), F32).at[:, HEAD_DIM].set(1.0)
    qconst = jnp.concatenate([qconst.reshape(1, C_Q), vconst.reshape(1, C_V)], axis=1)

    x2d = x.reshape(t, d)
    for i in range(depth):
        wa_pad = jnp.zeros((LANES, GLA_KEY_DIM), F32).at[GA_LANE0:GA_LANE0 + GLA_GATE_RANK].set(gla_w_alpha[i])
        conv_w_pad = jnp.zeros((CONV_HALO, CONV_CHANNELS), F32).at[:CONV_WIDTH].set(conv_w[i])
        both = lambda a: jnp.concatenate([a] * NSA_KV_HEADS, axis=-1)
        w1_blocks = lambda w: _block_diag2(w.reshape(CMP_BLOCK, HEAD_DIM, CMP_HIDDEN))
        lw = (mix_norm_g[i], w_in[i], w_out[i],
              jnp.stack([both(cmp_pos_k[i]), both(cmp_pos_v[i])]),
              jnp.stack([w1_blocks(cmp_w1_k[i]), w1_blocks(cmp_w1_v[i])]).astype(MXU_DTYPE),
              jnp.stack([both(cmp_b1_k[i])[None, :], both(cmp_b1_v[i])[None, :]]),
              jnp.stack([_block_diag2(cmp_w2_k[i]), _block_diag2(cmp_w2_v[i])]).astype(MXU_DTYPE),
              conv_w_pad, conv_b[i], conv_ln_g[i], conv_ln_b[i],
              wa_pad.astype(MXU_DTYPE), gla_b_alpha[i], gla_norm_g[i],
              ffn_norm_g[i], ffn_w_gate[i], ffn_w_up[i], ffn_w_down[i],
              ple_norm_g[i], ple_w_gate[i], ple_w_proj[i], qconst)
        x2d = _layer(x2d, p[i].reshape(t, -1), bsz, seq, rel_tabs, agg, lw, final_norm_g, i == depth - 1)
    return x2d.reshape(bsz, seq, d)
```

```python
import functools
import math

import numpy as np
import jax
import jax.numpy as jnp
from jax import lax
from jax.experimental import pallas as pl
from jax.experimental.pallas import tpu as pltpu

F32 = jnp.float32
MXU_DTYPE = jnp.bfloat16

HEAD_DIM = 64
NSA_HEADS = 8
NSA_KV_HEADS = 2
NSA_GROUP = NSA_HEADS // NSA_KV_HEADS
CMP_BLOCK = 32
CMP_STRIDE = 16
CMP_HIDDEN = 256
SLC_BLOCK = 64
N_SELECT = 16
WINDOW = 512
N_BRANCH = 3
CONV_CHANNELS = 256
CONV_WIDTH = 31
GLA_HEADS = 4
GLA_KEY_DIM = 128
GLA_VALUE_DIM = 256
GLA_GATE_RANK = 16
GLA_TAU = 16.0
GLA_CHUNK = 64
REL_BUCKETS = 32
REL_MAX_DIST = 128
PLE_DIM = 256
EPS = 1e-6
NEG_BIG = -1e30

LANES = 128
SUBLANES = 8
VMEM_LIMIT_BYTES = 56 * 1024 * 1024

TM_PROJ = 512
TQ = 256
ROWS = NSA_GROUP * TQ
NEAR_W = 2 * TQ // CMP_STRIDE
TS_CONV = 512
CONV_HALO = 32
TS_GLA = 1024
FF_CHUNKS = 2

C_Q = NSA_HEADS * HEAD_DIM
C_V = 2 * NSA_KV_HEADS * HEAD_DIM
C_KT = 2 * NSA_KV_HEADS * HEAD_DIM
C_KVC = 2 * NSA_KV_HEADS * HEAD_DIM
C_CU = 2 * CONV_CHANNELS
C_GLA = 2 * GLA_KEY_DIM + 2 * GLA_VALUE_DIM
C_MISC = NSA_KV_HEADS * LANES
GA_LANE0 = 16
FAR_LANE = HEAD_DIM


def _split_hi_lo(x):
    hi = x.astype(MXU_DTYPE)
    lo = (x - hi.astype(F32)).astype(MXU_DTYPE)
    return hi, lo


def _split3(x):
    hi = x.astype(MXU_DTYPE)
    r = x - hi.astype(F32)
    mid = r.astype(MXU_DTYPE)
    lo = (r - mid.astype(F32)).astype(MXU_DTYPE)
    return hi, mid, lo


def _dot(a, b):
    return jnp.dot(a, b, preferred_element_type=F32)


def _dot_halves(a, b):
    half = a.shape[0] // 2
    return jnp.concatenate([_dot(a[:half], b), _dot(a[half:], b)], axis=0)


def _dot_nt(a, b):
    return lax.dot_general(a, b, (((1,), (1,)), ((), ())), preferred_element_type=F32)


def _dot_tn(a, b):
    return lax.dot_general(a, b, (((0,), (0,)), ((), ())), preferred_element_type=F32)


def _rms(x, g):
    return x * lax.rsqrt(jnp.mean(x * x, axis=-1, keepdims=True) + EPS) * g


def _sigmoid(x):
    return 1.0 / (1.0 + jnp.exp(-x))


def _silu(x):
    return x * _sigmoid(x)


def _params(*sem):
    return pltpu.CompilerParams(dimension_semantics=sem, vmem_limit_bytes=VMEM_LIMIT_BYTES)


def _full(shape):
    nd = len(shape)
    return pl.BlockSpec(shape, lambda *_: (0,) * nd)


def _t5_bucket_np(dist):
    n = np.maximum(dist, 0)
    max_exact = REL_BUCKETS // 2
    nf = np.maximum(n, 1).astype(np.float32)
    large = max_exact + (np.log(nf / np.float32(max_exact)) / np.float32(math.log(REL_MAX_DIST / max_exact))
                         * np.float32(REL_BUCKETS - max_exact)).astype(np.int32)
    large = np.minimum(large, REL_BUCKETS - 1)
    return np.where(n < max_exact, n, large).astype(np.int32)


def _in_proj_columns(in_splits):
    offs = np.concatenate([[0], np.cumsum(in_splits)])
    (o_nq, o_kc, o_vc, o_ks, o_vs, o_kw, o_vw, o_ng, o_cu, o_gq, o_gk, o_gv, o_ga, o_gr) = offs[:-1]
    src, scale = [], []

    def put(cols, s=1.0):
        src.extend(cols)
        scale.extend([s] * len(cols))

    put(list(range(o_nq, o_nq + C_Q)), HEAD_DIM ** -0.5)
    kvw = NSA_KV_HEADS * HEAD_DIM
    for o in (o_vs, o_vw, o_kc, o_vc):
        put(list(range(o, o + kvw)))
    put(list(range(o_cu, o_cu + C_CU)))
    put(list(range(o_gq, o_gq + GLA_KEY_DIM)))
    put(list(range(o_gk, o_gk + GLA_KEY_DIM)))
    put(list(range(o_gv, o_gv + GLA_VALUE_DIM)))
    put(list(range(o_gr, o_gr + GLA_VALUE_DIM)))
    per = NSA_GROUP * N_BRANCH
    for h in range(NSA_KV_HEADS):
        slab = [-1] * LANES
        slab[:per] = list(range(o_ng + h * per, o_ng + (h + 1) * per))
        if h == 0:
            slab[GA_LANE0:GA_LANE0 + GLA_GATE_RANK] = list(range(o_ga, o_ga + GLA_GATE_RANK))
        put(slab)
    key_src = list(range(o_ks, o_ks + kvw)) + list(range(o_kw, o_kw + kvw))
    return np.asarray(src, np.int32), np.asarray(scale, np.float32), np.asarray(key_src, np.int32)


def _column_runs(src, scale):
    runs, i = [], 0
    while i < len(src):
        j = i + 1
        while j < len(src) and scale[j] == scale[i] and (
                (src[i] < 0 and src[j] < 0) or (src[i] >= 0 and src[j] == src[i] + (j - i))):
            j += 1
        runs.append((int(src[i]), j - i, float(scale[i])))
        i = j
    return runs


IN_SPLITS = (
    NSA_HEADS * HEAD_DIM,
    NSA_KV_HEADS * HEAD_DIM, NSA_KV_HEADS * HEAD_DIM,
    NSA_KV_HEADS * HEAD_DIM, NSA_KV_HEADS * HEAD_DIM,
    NSA_KV_HEADS * HEAD_DIM, NSA_KV_HEADS * HEAD_DIM,
    NSA_HEADS * N_BRANCH,
    2 * CONV_CHANNELS,
    GLA_KEY_DIM, GLA_KEY_DIM, GLA_VALUE_DIM,
    GLA_GATE_RANK,
    GLA_VALUE_DIM,
)
_W_SRC, _W_SCALE, _WK_SRC = _in_proj_columns(IN_SPLITS)
_W_RUNS = _column_runs(_W_SRC, _W_SCALE)
_WK_RUNS = _column_runs(_WK_SRC, np.ones_like(_WK_SRC, np.float32))
C_ALL = C_Q + C_V + C_KVC + C_CU + C_GLA + C_MISC
assert _W_SRC.shape[0] == C_ALL and _WK_SRC.shape[0] == C_KT


def _gather_columns(w, runs):
    parts = []
    for start, width, scale in runs:
        if start < 0:
            parts.append(jnp.zeros((w.shape[0], width), w.dtype))
        else:
            part = w[:, start:start + width]
            parts.append(part if scale == 1.0 else part * scale)
    return jnp.concatenate(parts, axis=1)


def _nsa_bias_tables(rel_bias):
    i = np.arange(TQ)[:, None]
    j = np.arange(TQ)[None, :]
    far = rel_bias[REL_BUCKETS - 1]
    tab = rel_bias - far[None, :]

    d = np.arange(2 * TQ - 1, -2 * TQ - 1, -1)
    by_dist = jnp.where(jnp.asarray(d >= 0)[:, None], jnp.take(tab, jnp.asarray(_t5_bucket_np(d)), axis=0), NEG_BIG).T

    n = by_dist.shape[1]
    skew = jnp.tile(by_dist, (1, TQ))[:, :TQ * (n - 1)].reshape(NSA_HEADS, TQ, n - 1)

    def toeplitz(dist0, step, width):
        first = 2 * TQ - 1 - dist0
        assert first >= TQ - 1 and first + step * (width - 1) < n - 1
        vals = lax.slice_in_dim(skew, first, first + step * (width - 1) + 1, stride=step, axis=2)
        return vals.reshape(NSA_KV_HEADS, ROWS, width)

    tri = jnp.asarray(np.where(j <= i, NEG_BIG, 0.0).astype(np.float32))
    tri = jnp.broadcast_to(jnp.tile(tri, (NSA_GROUP, 1))[None], (NSA_KV_HEADS, ROWS, TQ))
    bm = jnp.concatenate([tri, toeplitz(TQ, 1, 2 * TQ)], axis=-1)

    near = toeplitz((NEAR_W // 2) * CMP_STRIDE - (CMP_BLOCK - 1), CMP_STRIDE, NEAR_W)
    hi, lo = _split_hi_lo(near)
    pad = jnp.zeros((NSA_KV_HEADS, ROWS, LANES - 2 * NEAR_W - 1), MXU_DTYPE)
    big = jnp.full((NSA_KV_HEADS, ROWS, 1), NEG_BIG, MXU_DTYPE)
    ac = jnp.concatenate([hi, lo, big, pad], axis=-1)
    return bm, ac


def _agg_matrix(ncp, nbp):
    rs, rc = SLC_BLOCK // CMP_STRIDE, CMP_BLOCK // CMP_STRIDE
    agg = np.zeros((ncp, nbp), np.float32)
    ncmp = ncp - rc + 1
    for jb in range(nbp):
        for mm in range(rs):
            for nn in range(rc):
                idx = jb * rs + mm - nn
                if 0 <= idx < ncmp:
                    agg[idx, jb] += 1.0
    return agg


def _in_proj_kernel(x_ref, g_ref, w_ref, wk_ref, qc_ref, q_ref, ks_ref, kw_ref, vs_ref, vw_ref,
                    kvc_ref, cu_ref, gla_ref, misc_ref, *, tiles_per_seq):
    h = _rms(x_ref[...], g_ref[...]).astype(MXU_DTYPE)
    tm = h.shape[0]
    low = lax.broadcasted_iota(jnp.int32, (tm, LANES), 1) < HEAD_DIM

    def head_group(z, idx, const):
        g = z[:, (idx // 2) * LANES:(idx // 2 + 1) * LANES]
        if idx % 2:
            g = pltpu.roll(g, HEAD_DIM, 1)
        return jnp.where(low, g, const)

    o = 0
    zq = _dot(h, w_ref[:, o:o + C_Q])
    for hd in range(NSA_HEADS):
        q_ref[hd] = head_group(zq, hd, qc_ref[hd:hd + 1, :]).astype(q_ref.dtype)
    o += C_Q
    zv = _dot(h, w_ref[:, o:o + C_V])
    ones_lane = qc_ref[NSA_HEADS:NSA_HEADS + 1, :]
    for hh in range(NSA_KV_HEADS):
        vs_ref[hh] = head_group(zv, hh, ones_lane).astype(vs_ref.dtype)
        vw_ref[hh] = head_group(zv, NSA_KV_HEADS + hh, ones_lane).astype(vw_ref.dtype)
    o += C_V
    kt = _dot_nt(wk_ref[...], h)
    rows = lax.broadcasted_iota(jnp.int32, (LANES - HEAD_DIM, tm), 0)
    ones_rows = jnp.where(rows < 2, 1.0, 0.0).astype(ks_ref.dtype)
    tok = (pl.program_id(0) % tiles_per_seq) * tm + lax.broadcasted_iota(jnp.int32, (LANES, tm), 1)
    blk = lax.broadcasted_iota(jnp.int32, (LANES, tm), 0)
    blk_rows = jnp.where(lax.shift_right_logical(tok, int(math.log2(SLC_BLOCK))) == blk, 1.0, 0.0).astype(ks_ref.dtype)
    for hh in range(NSA_KV_HEADS):
        ks_ref[hh, 0:HEAD_DIM, :] = kt[hh * HEAD_DIM:(hh + 1) * HEAD_DIM].astype(ks_ref.dtype)
        ks_ref[hh, HEAD_DIM:LANES, :] = ones_rows
        ks_ref[hh, LANES:2 * LANES, :] = blk_rows
        kw_ref[hh, 0:HEAD_DIM, :] = kt[(NSA_KV_HEADS + hh) * HEAD_DIM:(NSA_KV_HEADS + hh + 1) * HEAD_DIM].astype(kw_ref.dtype)
        kw_ref[hh, HEAD_DIM:LANES, :] = ones_rows
    kvc_ref[...] = _dot(h, w_ref[:, o:o + C_KVC])
    o += C_KVC
    cu_ref[...] = _dot(h, w_ref[:, o:o + C_CU])
    o += C_CU
    gla_ref[...] = _dot(h, w_ref[:, o:o + C_GLA])
    o += C_GLA
    zm = _dot(h, w_ref[:, o:o + C_MISC])
    for s in range(NSA_KV_HEADS):
        misc_ref[s] = zm[:, s * LANES:(s + 1) * LANES]


def _in_proj(x2d, g, w_all, wk_t, qconst, bsz, seq):
    t, d = x2d.shape
    tm = min(TM_PROJ, seq)
    tps = seq // tm
    hk = NSA_KV_HEADS
    row = lambda i: (i, 0)
    kmap = lambda i: (i // tps, 0, 0, i % tps)
    vmap = lambda i: (i // tps, 0, i % tps, 0)
    return pl.pallas_call(
        functools.partial(_in_proj_kernel, tiles_per_seq=tps),
        grid=(t // tm,),
        in_specs=[pl.BlockSpec((tm, d), row), _full((1, d)), _full((d, C_ALL)), _full((C_KT, d)),
                  _full((NSA_HEADS + 1, LANES))],
        out_specs=[
            pl.BlockSpec((NSA_HEADS, tm, LANES), lambda i: (0, i, 0)),
            pl.BlockSpec((None, hk, 2 * LANES, tm), kmap),
            pl.BlockSpec((None, hk, LANES, tm), kmap),
            pl.BlockSpec((None, hk, tm, LANES), vmap),
            pl.BlockSpec((None, hk, tm, LANES), vmap),
            pl.BlockSpec((tm, C_KVC), row),
            pl.BlockSpec((tm, C_CU), row),
            pl.BlockSpec((tm, C_GLA), row),
            pl.BlockSpec((NSA_KV_HEADS, tm, LANES), lambda i: (0, i, 0)),
        ],
        out_shape=[
            jax.ShapeDtypeStruct((NSA_HEADS, t, LANES), MXU_DTYPE),
            jax.ShapeDtypeStruct((bsz, hk, 2 * LANES, seq), MXU_DTYPE),
            jax.ShapeDtypeStruct((bsz, hk, LANES, seq), MXU_DTYPE),
            jax.ShapeDtypeStruct((bsz, hk, seq, LANES), MXU_DTYPE),
            jax.ShapeDtypeStruct((bsz, hk, seq, LANES), MXU_DTYPE),
            jax.ShapeDtypeStruct((t, C_KVC), F32),
            jax.ShapeDtypeStruct((t, C_CU), F32),
            jax.ShapeDtypeStruct((t, C_GLA), F32),
            jax.ShapeDtypeStruct((NSA_KV_HEADS, t, LANES), F32),
        ],
        compiler_params=_params("parallel"),
        name="in_proj",
    )(x2d, g, w_all, wk_t, qconst)


def _compress_kernel(x_ref, pos_ref, w1_ref, b1_ref, w2_ref, o_ref):
    nc = o_ref.shape[0]
    top = bot = None
    for l in range(CMP_STRIDE):
        x = x_ref[pl.ds(l, nc, stride=CMP_STRIDE), :]
        t = _dot((x + pos_ref[l:l + 1, :]).astype(MXU_DTYPE), w1_ref[l])
        u = _dot((x + pos_ref[CMP_STRIDE + l:CMP_STRIDE + l + 1, :]).astype(MXU_DTYPE), w1_ref[CMP_STRIDE + l])
        top = t if top is None else top + t
        bot = u if bot is None else bot + u
    hid = _silu(top + pltpu.roll(bot, nc - 1, 0) + b1_ref[...])
    o_ref[...] = _dot(hid.astype(MXU_DTYPE), w2_ref[...]).astype(o_ref.dtype)


def _compress(kvc, pos, w1, b1, w2):
    b, s, _ = kvc.shape
    nc = s // CMP_STRIDE
    hw = NSA_KV_HEADS * HEAD_DIM
    hid = NSA_KV_HEADS * CMP_HIDDEN
    return pl.pallas_call(
        _compress_kernel,
        grid=(2, b),
        in_specs=[
            pl.BlockSpec((None, s, hw), lambda kv, bi: (bi, 0, kv)),
            pl.BlockSpec((None, CMP_BLOCK, hw), lambda kv, bi: (kv, 0, 0)),
            pl.BlockSpec((None, CMP_BLOCK, hw, hid), lambda kv, bi: (kv, 0, 0, 0)),
            pl.BlockSpec((None, 1, hid), lambda kv, bi: (kv, 0, 0)),
            pl.BlockSpec((None, hid, hw), lambda kv, bi: (kv, 0, 0)),
        ],
        out_specs=pl.BlockSpec((None, None, nc, hw), lambda kv, bi: (kv, bi, 0, 0)),
        out_shape=jax.ShapeDtypeStruct((2, b, nc, hw), MXU_DTYPE),
        compiler_params=_params("parallel", "parallel"),
        name="compress_kv",
    )(kvc, pos, w1, b1, w2)


def _block_diag2(w):
    z = jnp.zeros_like(w)
    return jnp.concatenate([jnp.concatenate([w, z], axis=-1), jnp.concatenate([z, w], axis=-1)], axis=-2)


def _nsa_kernel(*refs):
    c = pl.program_id(2)

    @pl.when(c >= 2)
    def _():
        _nsa_body(True, c, *refs)

    @pl.when(c < 2)
    def _():
        _nsa_body(False, c, *refs)


def _nsa_body(full, c, q_ref, ks_ref, vs_ref, kw_ref, vw_ref, kc_ref, vc_ref, gl_ref, ac_ref, bm_ref, agg_ref,
              o_ref, kcx_scr, qs_scr, m_scr, acc_scr, sa_scr, sb_scr):
    nc = kc_ref.shape[1]
    nb = agg_ref.shape[1]
    q = q_ref[...].reshape(ROWS, LANES)

    rowi = lax.broadcasted_iota(jnp.int32, (LANES, nc), 0)
    coli = lax.broadcasted_iota(jnp.int32, (LANES, nc), 1)
    near0 = c * (TQ // CMP_STRIDE) - NEAR_W // 2
    shift = (coli == near0 + (rowi & (NEAR_W - 1))) & (rowi < 2 * NEAR_W)
    future = (rowi == 2 * NEAR_W) & (coli >= near0 + NEAR_W)
    kcx_scr[0:LANES, :] = kc_ref[...]
    kcx_scr[LANES:2 * LANES, :] = jnp.where(shift | future, 1.0, 0.0).astype(MXU_DTYPE)
    s_c = _dot_halves(jnp.concatenate([q, ac_ref[...]], axis=1), kcx_scr[...])
    m_c = jnp.maximum(jnp.max(s_c, axis=1, keepdims=True), 0.1 * NEG_BIG)
    e_c = jnp.exp(s_c - m_c)
    l_c = jnp.sum(e_c, axis=1, keepdims=True)
    p_c = e_c * (1.0 / jnp.maximum(l_c, 1e-30))
    acc_c = _dot_halves(p_c.astype(MXU_DTYPE), vc_ref[...])

    imp = p_c[0:TQ] + p_c[TQ:2 * TQ] + p_c[2 * TQ:3 * TQ] + p_c[3 * TQ:4 * TQ]
    agg = agg_ref[...]
    i_hi, i_mid, i_lo = _split3(imp)
    imps = _dot(i_hi, agg) + _dot(i_mid, agg) + _dot(i_lo, agg)
    t_q = c * TQ + lax.broadcasted_iota(jnp.int32, (nb, TQ), 1)
    blk = lax.broadcasted_iota(jnp.int32, (nb, TQ), 0)
    cur = lax.shift_right_logical(t_q, int(math.log2(SLC_BLOCK)))
    causal = blk <= cur
    forced = (blk == 0) | (blk == cur) | (blk == cur - 1)
    neg_inf = -jnp.inf
    cand = jnp.where(causal & jnp.logical_not(forced), imps.T, neg_inf)
    sel = forced & causal
    blk_f = blk.astype(F32)
    for _ in range(N_SELECT - 3):
        best = jnp.max(cand, axis=0, keepdims=True)
        first = jnp.min(jnp.where(cand == best, blk_f, float(nb)), axis=0, keepdims=True)
        hit = (blk_f == first) & (best > neg_inf)
        sel = sel | hit
        cand = jnp.where(hit, neg_inf, cand)
    selneg = jnp.where(sel, 0.0, NEG_BIG).T.astype(MXU_DTYPE)
    qs_scr[:, 0:LANES] = q
    qs_scr[:, LANES:2 * LANES] = jnp.concatenate([selneg] * NSA_GROUP, axis=0)

    def flash_init():
        m_scr[...] = jnp.full(m_scr.shape, NEG_BIG, F32)
        acc_scr[...] = jnp.zeros(acc_scr.shape, F32)

    def scores(k_ref, kt):
        start = pl.multiple_of(kt * TQ, TQ)
        return _dot(qs_scr[:, 0:k_ref.shape[0]], k_ref[:, pl.ds(start, TQ)])

    def flash_update(s, v_ref, kt, ntiles=1, dot=_dot):
        start = pl.multiple_of(kt * TQ, TQ)
        m_prev = m_scr[...]
        m_new = jnp.maximum(m_prev, jnp.max(s, axis=1, keepdims=True))
        p = jnp.exp(s - jnp.concatenate([m_new] * (ntiles * TQ // LANES), axis=1))
        acc_scr[...] = jnp.exp(m_prev - m_new) * acc_scr[...] \
            + dot(p.astype(MXU_DTYPE), v_ref[pl.ds(start, ntiles * TQ), :])
        m_scr[...] = m_new

    def flash_step(k_ref, v_ref, kt, bias_tile):
        s = scores(k_ref, kt)
        if bias_tile is not None:
            s = s + bm_ref[:, bias_tile * TQ:(bias_tile + 1) * TQ]
        flash_update(s, v_ref, kt)

    def flash_out():
        acc = acc_scr[...]
        return acc * (1.0 / acc[:, HEAD_DIM:HEAD_DIM + 1])

    if full:
        w0 = pl.multiple_of((c - 2) * TQ, TQ)
        s_w = _dot_halves(q, kw_ref[:, pl.ds(w0, 3 * TQ)]) + bm_ref[...]
        p_w = jnp.exp(s_w - jnp.max(s_w, axis=1, keepdims=True))
        acc_w = _dot_halves(p_w.astype(MXU_DTYPE), vw_ref[pl.ds(w0, 3 * TQ), :])
        o_w = acc_w * (1.0 / acc_w[:, HEAD_DIM:HEAD_DIM + 1])
    else:
        flash_init()

        @pl.when(c >= 1)
        def _():
            flash_step(kw_ref, vw_ref, c - 1, 1)

        flash_step(kw_ref, vw_ref, c, 2)
        o_w = flash_out()

    flash_init()
    if full:
        n_far = c - 1
        n_pairs = lax.shift_right_logical(n_far, 1)

        @pl.when(n_pairs > 0)
        def _():
            sa_scr[...] = scores(ks_ref, 0)

        def far_pair(j, carry):
            sb_scr[...] = scores(ks_ref, 2 * j + 1)
            flash_update(sa_scr[...], vs_ref, 2 * j)
            sa_scr[...] = scores(ks_ref, 2 * j + 2)
            flash_update(sb_scr[...], vs_ref, 2 * j + 1)
            return carry

        lax.fori_loop(0, n_pairs, far_pair, 0)

        @pl.when(n_far > 2 * n_pairs)
        def _():
            flash_step(ks_ref, vs_ref, n_far - 1, None)

        sa_scr[...] = scores(ks_ref, c - 1) + bm_ref[:, TQ:2 * TQ]
        sb_scr[...] = scores(ks_ref, c) + bm_ref[:, 2 * TQ:3 * TQ]
        flash_update(sa_scr[...], vs_ref, c - 1)
        flash_update(sb_scr[...], vs_ref, c)
    else:
        @pl.when(c >= 1)
        def _():
            flash_step(ks_ref, vs_ref, c - 1, 1)

        flash_step(ks_ref, vs_ref, c, 2)
    o_s = flash_out()

    gates = _sigmoid(gl_ref[...])
    lane = lax.broadcasted_iota(jnp.int32, (TQ, LANES), 1)
    outs = []
    for g in range(NSA_GROUP):
        r = slice(g * TQ, (g + 1) * TQ)
        gc = gates[:, g * N_BRANCH + 0:g * N_BRANCH + 1]
        gs = gates[:, g * N_BRANCH + 1:g * N_BRANCH + 2]
        gw = gates[:, g * N_BRANCH + 2:g * N_BRANCH + 3]
        outs.append(gc * acc_c[r] + gs * o_s[r] + gw * o_w[r])
    for pair in range(NSA_GROUP // 2):
        both = jnp.where(lane < HEAD_DIM, outs[2 * pair], pltpu.roll(outs[2 * pair + 1], HEAD_DIM, 1))
        o_ref[:, pair * LANES:(pair + 1) * LANES] = both.astype(o_ref.dtype)


def _nsa(q8, ks, vs, kw, vw, kc, vc, gl, ac, bm, agg):
    _, b, s, _ = q8.shape
    nc = kc.shape[-1]
    nb = agg.shape[1]
    kvmap = lambda bi, h, c: (bi, h, 0, 0)
    hmap = lambda bi, h, c: (h, 0, 0)
    return pl.pallas_call(
        _nsa_kernel,
        grid=(b, NSA_KV_HEADS, s // TQ),
        in_specs=[
            pl.BlockSpec((NSA_GROUP, None, TQ, LANES), lambda bi, h, c: (h, bi, c, 0)),
            pl.BlockSpec((None, None, 2 * LANES, s), kvmap),
            pl.BlockSpec((None, None, s, LANES), kvmap),
            pl.BlockSpec((None, None, LANES, s), kvmap),
            pl.BlockSpec((None, None, s, LANES), kvmap),
            pl.BlockSpec((None, None, LANES, nc), kvmap),
            pl.BlockSpec((None, None, nc, LANES), kvmap),
            pl.BlockSpec((None, None, TQ, LANES), lambda bi, h, c: (h, bi, c, 0)),
            pl.BlockSpec((None, ROWS, LANES), hmap),
            pl.BlockSpec((None, ROWS, 3 * TQ), hmap),
            _full((nc, nb)),
        ],
        out_specs=pl.BlockSpec((None, TQ, NSA_GROUP * HEAD_DIM), lambda bi, h, c: (bi, c, h)),
        out_shape=jax.ShapeDtypeStruct((b, s, NSA_HEADS * HEAD_DIM), MXU_DTYPE),
        scratch_shapes=[
            pltpu.VMEM((2 * LANES, nc), MXU_DTYPE),
            pltpu.VMEM((ROWS, 2 * LANES), MXU_DTYPE),
            pltpu.VMEM((ROWS, LANES), F32),
            pltpu.VMEM((ROWS, LANES), F32),
            pltpu.VMEM((ROWS, TQ), F32),
            pltpu.VMEM((ROWS, TQ), F32),
        ],
        compiler_params=_params("parallel", "parallel", "arbitrary"),
        name="nsa_attention",
    )(q8, ks, vs, kw, vw, kc, vc, gl, ac, bm, agg)


def _conv_kernel(cu_ref, halo_ref, w_ref, b_ref, g_ref, bb_ref, o_ref, hh_scr, sh_scr):
    ts = cu_ref.shape[0]
    ch = CONV_CHANNELS

    def glu(u):
        return u[:, :ch] * _sigmoid(u[:, ch:])

    first = pl.program_id(1) == 0
    hh_scr[0:CONV_HALO, :] = jnp.where(first, 0.0, glu(halo_ref[...]))
    hh_scr[CONV_HALO:CONV_HALO + ts, :] = glu(cu_ref[...])
    hh_scr[CONV_HALO + ts:, :] = jnp.zeros((SUBLANES, ch), F32)
    acc = jnp.broadcast_to(b_ref[...], (ts, ch))
    lead = CONV_HALO - (CONV_WIDTH - 1)
    for shift in range(SUBLANES):
        sh_scr[...] = hh_scr[pl.ds(shift, CONV_HALO + ts), :]
        for start in range(shift, lead + CONV_WIDTH, SUBLANES):
            w = start - lead
            if 0 <= w < CONV_WIDTH:
                acc = acc + sh_scr[pl.ds(start - shift, ts), :] * w_ref[w:w + 1, :]
    mu = jnp.mean(acc, axis=-1, keepdims=True)
    xc = acc - mu
    y = xc * lax.rsqrt(jnp.mean(xc * xc, axis=-1, keepdims=True) + EPS) * g_ref[...] + bb_ref[...]
    o_ref[...] = _silu(y).astype(o_ref.dtype)


def _conv(cu, w, bias, ln_g, ln_b):
    b, s, _ = cu.shape
    ts = min(TS_CONV, s)
    per = ts // CONV_HALO
    return pl.pallas_call(
        _conv_kernel,
        grid=(b, s // ts),
        in_specs=[
            pl.BlockSpec((None, ts, C_CU), lambda bi, i: (bi, i, 0)),
            pl.BlockSpec((None, CONV_HALO, C_CU), lambda bi, i: (bi, jnp.maximum(i * per - 1, 0), 0)),
            _full((CONV_HALO, CONV_CHANNELS)), _full((1, CONV_CHANNELS)),
            _full((1, CONV_CHANNELS)), _full((1, CONV_CHANNELS)),
        ],
        out_specs=pl.BlockSpec((None, ts, CONV_CHANNELS), lambda bi, i: (bi, i, 0)),
        out_shape=jax.ShapeDtypeStruct((b, s, CONV_CHANNELS), MXU_DTYPE),
        scratch_shapes=[pltpu.VMEM((CONV_HALO + ts + SUBLANES, CONV_CHANNELS), F32),
                        pltpu.VMEM((CONV_HALO + ts, CONV_CHANNELS), F32)],
        compiler_params=_params("parallel", "parallel"),
        name="conformer_conv",
    )(cu, cu, w, bias, ln_g, ln_b)


def _gla_kernel(q_ref, k_ref, v_ref, r_ref, ga_ref, wa_ref, ba_ref, g_ref, tri_ref, hm_ref, vm_ref, bmk_ref,
                gm_ref, o_ref, st_scr, sb_scr, o_scr, u_scr):
    ts = q_ref.shape[0]
    nch = ts // GLA_CHUNK
    dk = GLA_KEY_DIM // GLA_HEADS
    cs = GLA_CHUNK

    @pl.when(pl.program_id(1) == 0)
    def _():
        st_scr[...] = jnp.zeros(st_scr.shape, F32)

    x = _dot(ga_ref[...].astype(MXU_DTYPE), wa_ref[...]) + ba_ref[...]
    log_a = (jnp.minimum(x, 0.0) - jnp.log(1.0 + jnp.exp(-jnp.abs(x)))) * (1.0 / GLA_TAU)
    wide = jnp.concatenate([log_a[n * cs:(n + 1) * cs] for n in range(nch)], axis=1)
    l_hi, l_mid, l_lo = _split3(wide)
    tri = tri_ref[...]
    b_wide = _dot(tri, l_hi) + _dot(tri, l_mid) + _dot(tri, l_lo)
    b = jnp.concatenate([b_wide[:, n * GLA_KEY_DIM:(n + 1) * GLA_KEY_DIM] for n in range(nch)], axis=0)
    eb = jnp.exp(b)
    q_t = (q_ref[...] * (dk ** -0.5) * eb).astype(MXU_DTYPE)
    k_all = k_ref[...]
    k_t = (k_all * jnp.exp(-b)).astype(MXU_DTYPE)
    v_all = v_ref[...].astype(MXU_DTYPE)
    causal = tri_ref[...] > 0
    blockmask = bmk_ref[...]

    for n in range(nch):
        r = slice(n * cs, (n + 1) * cs)
        b_last = b[(n + 1) * cs - 1:(n + 1) * cs, :]
        k_d = (k_all[r] * jnp.exp(b_last - b[r])).astype(MXU_DTYPE)
        u_scr[n] = _dot_tn(v_all[r], k_d) * blockmask
    state = st_scr[...]
    for n in range(nch):
        sb_scr[n] = state.astype(MXU_DTYPE)
        b_last = b[(n + 1) * cs - 1:(n + 1) * cs, :]
        state = state * jnp.exp(b_last) + u_scr[n]
    st_scr[...] = state

    causal_h = jnp.concatenate([causal] * GLA_HEADS, axis=0)
    for n in range(nch):
        r = slice(n * cs, (n + 1) * cs)
        qn = q_t[r]
        o = _dot_nt(qn, sb_scr[n])
        q_heads = jnp.concatenate([qn * hm_ref[h:h + 1, :] for h in range(GLA_HEADS)], axis=0)
        attn = jnp.where(causal_h, _dot_nt(q_heads, k_t[r]), 0.0).astype(MXU_DTYPE)
        pv = _dot(attn, v_all[r])
        for h in range(GLA_HEADS):
            o = o + pv[h * cs:(h + 1) * cs] * vm_ref[h:h + 1, :]
        o_scr[r, :] = o
    o = o_scr[...]
    o_hi, o_mid, o_lo = _split3(o * o)
    gm = gm_ref[...]
    ms = _dot(o_hi, gm) + _dot(o_mid, gm) + _dot(o_lo, gm)
    y = o * lax.rsqrt(ms + EPS) * g_ref[...]
    o_ref[...] = (y * _silu(r_ref[...])).astype(o_ref.dtype)


def _gla(gla, misc, wa_pad, ba, g):
    b, s, _ = gla.shape
    ts = min(TS_GLA, s)
    nch = ts // GLA_CHUNK
    dk, dv = GLA_KEY_DIM // GLA_HEADS, GLA_VALUE_DIM // GLA_HEADS
    tri = np.tril(np.ones((GLA_CHUNK, GLA_CHUNK), np.float32))
    hm = (np.arange(GLA_KEY_DIM)[None, :] // dk == np.arange(GLA_HEADS)[:, None]).astype(np.float32)
    vm = (np.arange(GLA_VALUE_DIM)[None, :] // dv == np.arange(GLA_HEADS)[:, None]).astype(np.float32)
    bmk = (np.arange(GLA_VALUE_DIM)[:, None] // dv == np.arange(GLA_KEY_DIM)[None, :] // dk).astype(np.float32)
    gm = (np.arange(GLA_VALUE_DIM)[:, None] // dv == np.arange(GLA_VALUE_DIM)[None, :] // dv).astype(np.float32) / dv
    return pl.pallas_call(
        _gla_kernel,
        grid=(b, s // ts),
        in_specs=[
            pl.BlockSpec((None, ts, GLA_KEY_DIM), lambda bi, i: (bi, i, 0)),
            pl.BlockSpec((None, ts, GLA_KEY_DIM), lambda bi, i: (bi, i, 1)),
            pl.BlockSpec((None, ts, GLA_VALUE_DIM), lambda bi, i: (bi, i, 1)),
            pl.BlockSpec((None, ts, GLA_VALUE_DIM), lambda bi, i: (bi, i, 2)),
            pl.BlockSpec((None, None, ts, LANES), lambda bi, i: (0, bi, i, 0)),
            _full((LANES, GLA_KEY_DIM)), _full((1, GLA_KEY_DIM)), _full((1, GLA_VALUE_DIM)),
            _full(tri.shape), _full(hm.shape), _full(vm.shape), _full(bmk.shape), _full(gm.shape),
        ],
        out_specs=pl.BlockSpec((None, ts, GLA_VALUE_DIM), lambda bi, i: (bi, i, 0)),
        out_shape=jax.ShapeDtypeStruct((b, s, GLA_VALUE_DIM), MXU_DTYPE),
        scratch_shapes=[
            pltpu.VMEM((GLA_VALUE_DIM, GLA_KEY_DIM), F32),
            pltpu.VMEM((nch, GLA_VALUE_DIM, GLA_KEY_DIM), MXU_DTYPE),
            pltpu.VMEM((ts, GLA_VALUE_DIM), F32),
            pltpu.VMEM((nch, GLA_VALUE_DIM, GLA_KEY_DIM), F32),
        ],
        compiler_params=_params("parallel", "arbitrary"),
        name="gla",
    )(gla, gla, gla, gla, misc, wa_pad, ba, g,
      jnp.asarray(tri, MXU_DTYPE), jnp.asarray(hm, MXU_DTYPE), jnp.asarray(vm), jnp.asarray(bmk),
      jnp.asarray(gm, MXU_DTYPE))


def _post_kernel(x_ref, yn_ref, yc_ref, yg_ref, p_ref, wo_ref, gf_ref, wg_ref, wu_ref, wd_ref,
                 gp_ref, wpg_ref, wpp_ref, gfin_ref, o_ref, *, final):
    dn = yn_ref.shape[1]
    dc = yc_ref.shape[1]
    mix = _dot(yn_ref[...], wo_ref[0:dn, :]) + _dot(yc_ref[...], wo_ref[dn:dn + dc, :]) \
        + _dot(yg_ref[...], wo_ref[dn + dc:, :])
    x = x_ref[...] + mix
    h = _rms(x, gf_ref[...]).astype(MXU_DTYPE)
    dff = wg_ref.shape[1]
    step = dff // FF_CHUNKS
    ff = None
    for j in range(FF_CHUNKS):
        cols = slice(j * step, (j + 1) * step)
        act = _silu(_dot(h, wg_ref[:, cols])) * _dot(h, wu_ref[:, cols])
        down = _dot(act.astype(MXU_DTYPE), wd_ref[cols, :])
        ff = down if ff is None else ff + down
    x = x + ff
    gate = _sigmoid(_dot(_rms(x, gp_ref[...]).astype(MXU_DTYPE), wpg_ref[...]))
    x = x + _dot(p_ref[...].astype(MXU_DTYPE), wpp_ref[...]) * gate
    if final:
        x = _rms(x, gfin_ref[...])
    o_ref[...] = x


def _post(x2d, yn, yc, yg, p2d, wo, gf, wg, wu, wd, gp, wpg, wpp, gfin, final):
    t, d = x2d.shape
    tm = min(TM_PROJ, t)
    row = lambda i: (i, 0)
    once = dict(pipeline_mode=pl.Buffered(1))

    def const(a):
        return pl.BlockSpec(a.shape, lambda i: (0,) * a.ndim, **once)

    return pl.pallas_call(
        functools.partial(_post_kernel, final=final),
        grid=(t // tm,),
        in_specs=[
            pl.BlockSpec((tm, d), row),
            pl.BlockSpec((tm, yn.shape[1]), row), pl.BlockSpec((tm, yc.shape[1]), row),
            pl.BlockSpec((tm, yg.shape[1]), row), pl.BlockSpec((tm, p2d.shape[1]), row),
            const(wo), const(gf), const(wg), const(wu), const(wd), const(gp), const(wpg), const(wpp),
            const(gfin),
        ],
        out_specs=pl.BlockSpec((tm, d), row),
        out_shape=jax.ShapeDtypeStruct((t, d), F32),
        compiler_params=_params("parallel"),
        name="post_mixer",
    )(x2d, yn, yc, yg, p2d, wo, gf, wg, wu, wd, gp, wpg, wpp, gfin)


def _layer(x2d, p2d, bsz, seq, rel_tabs, agg, lw, final_g, final):
    (mix_g, w_in, w_out, pos_kv, w1_kv, b1_kv, w2_kv, conv_w, conv_b, conv_ln_g, conv_ln_b,
     w_alpha, b_alpha, gla_g, ffn_g, w_gate, w_up, w_down, ple_g, w_pg, w_pp, qconst) = lw
    bm, ac = rel_tabs
    t = bsz * seq
    hk = NSA_KV_HEADS

    w_all = _gather_columns(w_in, _W_RUNS).astype(MXU_DTYPE)
    wk_t = _gather_columns(w_in, _WK_RUNS).T.astype(MXU_DTYPE)
    q8, ks, kw, vs, vw, kvc, cu, gla, misc = _in_proj(x2d, mix_g[None, :], w_all, wk_t, qconst, bsz, seq)

    nsub = seq // CMP_STRIDE
    cmp = _compress(kvc.reshape(bsz, seq, C_KVC), pos_kv, w1_kv, b1_kv, w2_kv)
    cmp = cmp.reshape(2, bsz, nsub, hk, HEAD_DIM)
    ones_row = jnp.ones((bsz, hk, 2, nsub), MXU_DTYPE)
    kc = jnp.concatenate([cmp[0].transpose(0, 2, 3, 1), ones_row,
                          jnp.zeros((bsz, hk, LANES - HEAD_DIM - 2, nsub), MXU_DTYPE)], axis=2)
    ones_col = jnp.ones((bsz, hk, nsub, 1), MXU_DTYPE)
    vc = jnp.concatenate([cmp[1].transpose(0, 2, 1, 3), ones_col,
                          jnp.zeros((bsz, hk, nsub, LANES - HEAD_DIM - 1), MXU_DTYPE)], axis=3)

    y_nsa = _nsa(q8.reshape(NSA_HEADS, bsz, seq, LANES), ks, vs, kw, vw, kc, vc,
                 misc.reshape(hk, bsz, seq, LANES), ac, bm, agg)

    y_conv = _conv(cu.reshape(bsz, seq, C_CU), conv_w, conv_b[None, :], conv_ln_g[None, :], conv_ln_b[None, :])
    y_gla = _gla(gla.reshape(bsz, seq, C_GLA), misc.reshape(hk, bsz, seq, LANES), w_alpha, b_alpha[None, :],
                 gla_g[None, :])

    return _post(x2d, y_nsa.reshape(t, -1), y_conv.reshape(t, -1), y_gla.reshape(t, -1), p2d,
                 w_out.astype(MXU_DTYPE), ffn_g[None, :], w_gate.astype(MXU_DTYPE), w_up.astype(MXU_DTYPE),
                 w_down.astype(MXU_DTYPE), ple_g[None, :], w_pg.astype(MXU_DTYPE), w_pp.astype(MXU_DTYPE),
                 final_g[None, :], final)


def kernel(x, p, rel_bias, mix_norm_g, w_in, w_out, cmp_pos_k, cmp_w1_k, cmp_b1_k, cmp_w2_k, cmp_pos_v, cmp_w1_v, cmp_b1_v, cmp_w2_v, conv_w, conv_b, conv_ln_g, conv_ln_b, gla_w_alpha, gla_b_alpha, gla_norm_g, ffn_norm_g, ffn_w_gate, ffn_w_up, ffn_w_down, ple_norm_g, ple_w_gate, ple_w_proj, final_norm_g):
    bsz, seq, d = x.shape
    depth = w_in.shape[0]
    assert seq % TQ == 0 and seq // SLC_BLOCK <= LANES
    t = bsz * seq
    rel_tabs = _nsa_bias_tables(rel_bias.astype(F32))
    nbp = LANES
    agg = jnp.asarray(_agg_matrix(seq // CMP_STRIDE, nbp), MXU_DTYPE)

    far = rel_bias[REL_BUCKETS - 1].astype(F32)
    f_hi, f_lo = _split_hi_lo(far)
    qconst = jnp.zeros((NSA_HEADS + 1, LANES), F32)
    qconst = qconst.at[:NSA_HEADS, FAR_LANE].set(f_hi.astype(F32)).at[:NSA_HEADS, FAR_LANE + 1].set(f_lo.astype(F32))
    qconst = qconst.at[NSA_HEADS, HEAD_DIM].set(1.0)

    x2d = x.reshape(t, d)
    for i in range(depth):
        wa_pad = jnp.zeros((LANES, GLA_KEY_DIM), F32).at[GA_LANE0:GA_LANE0 + GLA_GATE_RANK].set(gla_w_alpha[i])
        conv_w_pad = jnp.zeros((CONV_HALO, CONV_CHANNELS), F32).at[:CONV_WIDTH].set(conv_w[i])
        both = lambda a: jnp.concatenate([a] * NSA_KV_HEADS, axis=-1)
        w1_blocks = lambda w: _block_diag2(w.reshape(CMP_BLOCK, HEAD_DIM, CMP_HIDDEN))
        lw = (mix_norm_g[i], w_in[i], w_out[i],
              jnp.stack([both(cmp_pos_k[i]), both(cmp_pos_v[i])]),
              jnp.stack([w1_blocks(cmp_w1_k[i]), w1_blocks(cmp_w1_v[i])]).astype(MXU_DTYPE),
              jnp.stack([both(cmp_b1_k[i])[None, :], both(cmp_b1_v[i])[None, :]]),
              jnp.stack([_block_diag2(cmp_w2_k[i]), _block_diag2(cmp_w2_v[i])]).astype(MXU_DTYPE),
              conv_w_pad, conv_b[i], conv_ln_g[i], conv_ln_b[i],
              wa_pad.astype(MXU_DTYPE), gla_b_alpha[i], gla_norm_g[i],
              ffn_norm_g[i], ffn_w_gate[i], ffn_w_up[i], ffn_w_down[i],
              ple_norm_g[i], ple_w_gate[i], ple_w_proj[i], qconst)
        x2d = _layer(x2d, p[i].reshape(t, -1), bsz, seq, rel_tabs, agg, lw, final_norm_g, i == depth - 1)
    return x2d.reshape(bsz, seq, d)
```

```python
import functools
import math

import numpy as np
import jax
import jax.numpy as jnp
from jax import lax
from jax.experimental import pallas as pl
from jax.experimental.pallas import tpu as pltpu

F32 = jnp.float32
MXU_DTYPE = jnp.bfloat16

HEAD_DIM = 64
NSA_HEADS = 8
NSA_KV_HEADS = 2
NSA_GROUP = NSA_HEADS // NSA_KV_HEADS
CMP_BLOCK = 32
CMP_STRIDE = 16
CMP_HIDDEN = 256
SLC_BLOCK = 64
N_SELECT = 16
WINDOW = 512
N_BRANCH = 3
CONV_CHANNELS = 256
CONV_WIDTH = 31
GLA_HEADS = 4
GLA_KEY_DIM = 128
GLA_VALUE_DIM = 256
GLA_GATE_RANK = 16
GLA_TAU = 16.0
GLA_CHUNK = 64
REL_BUCKETS = 32
REL_MAX_DIST = 128
PLE_DIM = 256
EPS = 1e-6
NEG_BIG = -1e30

LANES = 128
SUBLANES = 8
VMEM_LIMIT_BYTES = 56 * 1024 * 1024

TM_PROJ = 512
TQ = 256
ROWS = NSA_GROUP * TQ
NEAR_W = 2 * TQ // CMP_STRIDE
FAR_UNROLL_LOG2 = 3
TS_CONV = 512
CONV_HALO = 32
TS_GLA = 1024
FF_CHUNKS = 2

C_Q = NSA_HEADS * HEAD_DIM
C_V = 2 * NSA_KV_HEADS * HEAD_DIM
C_KT = 2 * NSA_KV_HEADS * HEAD_DIM
C_KVC = 2 * NSA_KV_HEADS * HEAD_DIM
C_CU = 2 * CONV_CHANNELS
C_GLA = 2 * GLA_KEY_DIM + 2 * GLA_VALUE_DIM
C_MISC = NSA_KV_HEADS * LANES
GA_LANE0 = 16
FAR_LANE = HEAD_DIM


def _split_hi_lo(x):
    hi = x.astype(MXU_DTYPE)
    lo = (x - hi.astype(F32)).astype(MXU_DTYPE)
    return hi, lo


def _split3(x):
    hi = x.astype(MXU_DTYPE)
    r = x - hi.astype(F32)
    mid = r.astype(MXU_DTYPE)
    lo = (r - mid.astype(F32)).astype(MXU_DTYPE)
    return hi, mid, lo


def _dot(a, b):
    return jnp.dot(a, b, preferred_element_type=F32)


def _dot_halves(a, b):
    half = a.shape[0] // 2
    return jnp.concatenate([_dot(a[:half], b), _dot(a[half:], b)], axis=0)


def _dot_nt(a, b):
    return lax.dot_general(a, b, (((1,), (1,)), ((), ())), preferred_element_type=F32)


def _dot_tn(a, b):
    return lax.dot_general(a, b, (((0,), (0,)), ((), ())), preferred_element_type=F32)


def _rms(x, g):
    return x * lax.rsqrt(jnp.mean(x * x, axis=-1, keepdims=True) + EPS) * g


def _sigmoid(x):
    return 1.0 / (1.0 + jnp.exp(-x))


def _silu(x):
    return x * _sigmoid(x)


def _params(*sem):
    return pltpu.CompilerParams(dimension_semantics=sem, vmem_limit_bytes=VMEM_LIMIT_BYTES)


def _full(shape):
    nd = len(shape)
    return pl.BlockSpec(shape, lambda *_: (0,) * nd)


def _t5_bucket_np(dist):
    n = np.maximum(dist, 0)
    max_exact = REL_BUCKETS // 2
    nf = np.maximum(n, 1).astype(np.float32)
    large = max_exact + (np.log(nf / np.float32(max_exact)) / np.float32(math.log(REL_MAX_DIST / max_exact))
                         * np.float32(REL_BUCKETS - max_exact)).astype(np.int32)
    large = np.minimum(large, REL_BUCKETS - 1)
    return np.where(n < max_exact, n, large).astype(np.int32)


def _in_proj_columns(in_splits):
    offs = np.concatenate([[0], np.cumsum(in_splits)])
    (o_nq, o_kc, o_vc, o_ks, o_vs, o_kw, o_vw, o_ng, o_cu, o_gq, o_gk, o_gv, o_ga, o_gr) = offs[:-1]
    src, scale = [], []

    def put(cols, s=1.0):
        src.extend(cols)
        scale.extend([s] * len(cols))

    put(list(range(o_nq, o_nq + C_Q)), HEAD_DIM ** -0.5)
    kvw = NSA_KV_HEADS * HEAD_DIM
    for o in (o_vs, o_vw, o_kc, o_vc):
        put(list(range(o, o + kvw)))
    put(list(range(o_cu, o_cu + C_CU)))
    put(list(range(o_gq, o_gq + GLA_KEY_DIM)))
    put(list(range(o_gk, o_gk + GLA_KEY_DIM)))
    put(list(range(o_gv, o_gv + GLA_VALUE_DIM)))
    put(list(range(o_gr, o_gr + GLA_VALUE_DIM)))
    per = NSA_GROUP * N_BRANCH
    for h in range(NSA_KV_HEADS):
        slab = [-1] * LANES
        slab[:per] = list(range(o_ng + h * per, o_ng + (h + 1) * per))
        if h == 0:
            slab[GA_LANE0:GA_LANE0 + GLA_GATE_RANK] = list(range(o_ga, o_ga + GLA_GATE_RANK))
        put(slab)
    key_src = list(range(o_ks, o_ks + kvw)) + list(range(o_kw, o_kw + kvw))
    return np.asarray(src, np.int32), np.asarray(scale, np.float32), np.asarray(key_src, np.int32)


def _column_runs(src, scale):
    runs, i = [], 0
    while i < len(src):
        j = i + 1
        while j < len(src) and scale[j] == scale[i] and (
                (src[i] < 0 and src[j] < 0) or (src[i] >= 0 and src[j] == src[i] + (j - i))):
            j += 1
        runs.append((int(src[i]), j - i, float(scale[i])))
        i = j
    return runs


IN_SPLITS = (
    NSA_HEADS * HEAD_DIM,
    NSA_KV_HEADS * HEAD_DIM, NSA_KV_HEADS * HEAD_DIM,
    NSA_KV_HEADS * HEAD_DIM, NSA_KV_HEADS * HEAD_DIM,
    NSA_KV_HEADS * HEAD_DIM, NSA_KV_HEADS * HEAD_DIM,
    NSA_HEADS * N_BRANCH,
    2 * CONV_CHANNELS,
    GLA_KEY_DIM, GLA_KEY_DIM, GLA_VALUE_DIM,
    GLA_GATE_RANK,
    GLA_VALUE_DIM,
)
_W_SRC, _W_SCALE, _WK_SRC = _in_proj_columns(IN_SPLITS)
_W_RUNS = _column_runs(_W_SRC, _W_SCALE)
_WK_RUNS = _column_runs(_WK_SRC, np.ones_like(_WK_SRC, np.float32))
C_ALL = C_Q + C_V + C_KVC + C_CU + C_GLA + C_MISC
assert _W_SRC.shape[0] == C_ALL and _WK_SRC.shape[0] == C_KT


def _gather_columns(w, runs):
    parts = []
    for start, width, scale in runs:
        if start < 0:
            parts.append(jnp.zeros((w.shape[0], width), w.dtype))
        else:
            part = w[:, start:start + width]
            parts.append(part if scale == 1.0 else part * scale)
    return jnp.concatenate(parts, axis=1)


def _nsa_bias_tables(rel_bias):
    i = np.arange(TQ)[:, None]
    j = np.arange(TQ)[None, :]
    far = rel_bias[REL_BUCKETS - 1]
    tab = rel_bias - far[None, :]

    d = np.arange(2 * TQ - 1, -2 * TQ - 1, -1)
    by_dist = jnp.where(jnp.asarray(d >= 0)[:, None], jnp.take(tab, jnp.asarray(_t5_bucket_np(d)), axis=0), NEG_BIG).T

    n = by_dist.shape[1]
    skew = jnp.tile(by_dist, (1, TQ))[:, :TQ * (n - 1)].reshape(NSA_HEADS, TQ, n - 1)

    def toeplitz(dist0, step, width):
        first = 2 * TQ - 1 - dist0
        assert first >= TQ - 1 and first + step * (width - 1) < n - 1
        vals = lax.slice_in_dim(skew, first, first + step * (width - 1) + 1, stride=step, axis=2)
        return vals.reshape(NSA_KV_HEADS, ROWS, width)

    tri = jnp.asarray(np.where(j <= i, NEG_BIG, 0.0).astype(np.float32))
    tri = jnp.broadcast_to(jnp.tile(tri, (NSA_GROUP, 1))[None], (NSA_KV_HEADS, ROWS, TQ))
    bm = jnp.concatenate([tri, toeplitz(TQ, 1, 2 * TQ)], axis=-1)

    near = toeplitz((NEAR_W // 2) * CMP_STRIDE - (CMP_BLOCK - 1), CMP_STRIDE, NEAR_W)
    hi, lo = _split_hi_lo(near)
    pad = jnp.zeros((NSA_KV_HEADS, ROWS, LANES - 2 * NEAR_W - 1), MXU_DTYPE)
    big = jnp.full((NSA_KV_HEADS, ROWS, 1), NEG_BIG, MXU_DTYPE)
    ac = jnp.concatenate([hi, lo, big, pad], axis=-1)
    return bm, ac


def _agg_matrix(ncp, nbp):
    rs, rc = SLC_BLOCK // CMP_STRIDE, CMP_BLOCK // CMP_STRIDE
    agg = np.zeros((ncp, nbp), np.float32)
    ncmp = ncp - rc + 1
    for jb in range(nbp):
        for mm in range(rs):
            for nn in range(rc):
                idx = jb * rs + mm - nn
                if 0 <= idx < ncmp:
                    agg[idx, jb] += 1.0
    return agg


def _in_proj_kernel(x_ref, g_ref, w_ref, wk_ref, qc_ref, q_ref, ks_ref, kw_ref, vs_ref, vw_ref,
                    kvc_ref, cu_ref, gla_ref, misc_ref, *, tiles_per_seq):
    h = _rms(x_ref[...], g_ref[...]).astype(MXU_DTYPE)
    tm = h.shape[0]
    low = lax.broadcasted_iota(jnp.int32, (tm, LANES), 1) < HEAD_DIM

    def head_group(z, idx, const):
        g = z[:, (idx // 2) * LANES:(idx // 2 + 1) * LANES]
        if idx % 2:
            g = pltpu.roll(g, HEAD_DIM, 1)
        return jnp.where(low, g, const)

    o = 0
    zq = _dot(h, w_ref[:, o:o + C_Q])
    for hd in range(NSA_HEADS):
        q_ref[hd] = head_group(zq, hd, qc_ref[hd:hd + 1, :]).astype(q_ref.dtype)
    o += C_Q
    zv = _dot(h, w_ref[:, o:o + C_V])
    ones_lane = qc_ref[NSA_HEADS:NSA_HEADS + 1, :]
    for hh in range(NSA_KV_HEADS):
        vs_ref[hh] = head_group(zv, hh, ones_lane).astype(vs_ref.dtype)
        vw_ref[hh] = head_group(zv, NSA_KV_HEADS + hh, ones_lane).astype(vw_ref.dtype)
    o += C_V
    kt = _dot_nt(wk_ref[...], h)
    rows = lax.broadcasted_iota(jnp.int32, (LANES - HEAD_DIM, tm), 0)
    ones_rows = jnp.where(rows < 2, 1.0, 0.0).astype(ks_ref.dtype)
    tok = (pl.program_id(0) % tiles_per_seq) * tm + lax.broadcasted_iota(jnp.int32, (LANES, tm), 1)
    blk = lax.broadcasted_iota(jnp.int32, (LANES, tm), 0)
    blk_rows = jnp.where(lax.shift_right_logical(tok, int(math.log2(SLC_BLOCK))) == blk, 1.0, 0.0).astype(ks_ref.dtype)
    for hh in range(NSA_KV_HEADS):
        ks_ref[hh, 0:HEAD_DIM, :] = kt[hh * HEAD_DIM:(hh + 1) * HEAD_DIM].astype(ks_ref.dtype)
        ks_ref[hh, HEAD_DIM:LANES, :] = ones_rows
        ks_ref[hh, LANES:2 * LANES, :] = blk_rows
        kw_ref[hh, 0:HEAD_DIM, :] = kt[(NSA_KV_HEADS + hh) * HEAD_DIM:(NSA_KV_HEADS + hh + 1) * HEAD_DIM].astype(kw_ref.dtype)
        kw_ref[hh, HEAD_DIM:LANES, :] = ones_rows
    kvc_ref[...] = _dot(h, w_ref[:, o:o + C_KVC])
    o += C_KVC
    cu_ref[...] = _dot(h, w_ref[:, o:o + C_CU])
    o += C_CU
    gla_ref[...] = _dot(h, w_ref[:, o:o + C_GLA])
    o += C_GLA
    zm = _dot(h, w_ref[:, o:o + C_MISC])
    for s in range(NSA_KV_HEADS):
        misc_ref[s] = zm[:, s * LANES:(s + 1) * LANES]


def _in_proj(x2d, g, w_all, wk_t, qconst, bsz, seq):
    t, d = x2d.shape
    tm = min(TM_PROJ, seq)
    tps = seq // tm
    hk = NSA_KV_HEADS
    row = lambda i: (i, 0)
    kmap = lambda i: (i // tps, 0, 0, i % tps)
    vmap = lambda i: (i // tps, 0, i % tps, 0)
    return pl.pallas_call(
        functools.partial(_in_proj_kernel, tiles_per_seq=tps),
        grid=(t // tm,),
        in_specs=[pl.BlockSpec((tm, d), row), _full((1, d)), _full((d, C_ALL)), _full((C_KT, d)),
                  _full((NSA_HEADS + 1, LANES))],
        out_specs=[
            pl.BlockSpec((NSA_HEADS, tm, LANES), lambda i: (0, i, 0)),
            pl.BlockSpec((None, hk, 2 * LANES, tm), kmap),
            pl.BlockSpec((None, hk, LANES, tm), kmap),
            pl.BlockSpec((None, hk, tm, LANES), vmap),
            pl.BlockSpec((None, hk, tm, LANES), vmap),
            pl.BlockSpec((tm, C_KVC), row),
            pl.BlockSpec((tm, C_CU), row),
            pl.BlockSpec((tm, C_GLA), row),
            pl.BlockSpec((NSA_KV_HEADS, tm, LANES), lambda i: (0, i, 0)),
        ],
        out_shape=[
            jax.ShapeDtypeStruct((NSA_HEADS, t, LANES), MXU_DTYPE),
            jax.ShapeDtypeStruct((bsz, hk, 2 * LANES, seq), MXU_DTYPE),
            jax.ShapeDtypeStruct((bsz, hk, LANES, seq), MXU_DTYPE),
            jax.ShapeDtypeStruct((bsz, hk, seq, LANES), MXU_DTYPE),
            jax.ShapeDtypeStruct((bsz, hk, seq, LANES), MXU_DTYPE),
            jax.ShapeDtypeStruct((t, C_KVC), F32),
            jax.ShapeDtypeStruct((t, C_CU), F32),
            jax.ShapeDtypeStruct((t, C_GLA), F32),
            jax.ShapeDtypeStruct((NSA_KV_HEADS, t, LANES), F32),
        ],
        compiler_params=_params("parallel"),
        name="in_proj",
    )(x2d, g, w_all, wk_t, qconst)


def _compress_kernel(x_ref, pos_ref, w1_ref, b1_ref, w2_ref, o_ref):
    nc = o_ref.shape[0]
    top = bot = None
    for l in range(CMP_STRIDE):
        x = x_ref[pl.ds(l, nc, stride=CMP_STRIDE), :]
        t = _dot((x + pos_ref[l:l + 1, :]).astype(MXU_DTYPE), w1_ref[l])
        u = _dot((x + pos_ref[CMP_STRIDE + l:CMP_STRIDE + l + 1, :]).astype(MXU_DTYPE), w1_ref[CMP_STRIDE + l])
        top = t if top is None else top + t
        bot = u if bot is None else bot + u
    hid = _silu(top + pltpu.roll(bot, nc - 1, 0) + b1_ref[...])
    o_ref[...] = _dot(hid.astype(MXU_DTYPE), w2_ref[...]).astype(o_ref.dtype)


def _compress(kvc, pos, w1, b1, w2):
    b, s, _ = kvc.shape
    nc = s // CMP_STRIDE
    hw = NSA_KV_HEADS * HEAD_DIM
    hid = NSA_KV_HEADS * CMP_HIDDEN
    return pl.pallas_call(
        _compress_kernel,
        grid=(2, b),
        in_specs=[
            pl.BlockSpec((None, s, hw), lambda kv, bi: (bi, 0, kv)),
            pl.BlockSpec((None, CMP_BLOCK, hw), lambda kv, bi: (kv, 0, 0)),
            pl.BlockSpec((None, CMP_BLOCK, hw, hid), lambda kv, bi: (kv, 0, 0, 0)),
            pl.BlockSpec((None, 1, hid), lambda kv, bi: (kv, 0, 0)),
            pl.BlockSpec((None, hid, hw), lambda kv, bi: (kv, 0, 0)),
        ],
        out_specs=pl.BlockSpec((None, None, nc, hw), lambda kv, bi: (kv, bi, 0, 0)),
        out_shape=jax.ShapeDtypeStruct((2, b, nc, hw), MXU_DTYPE),
        compiler_params=_params("parallel", "parallel"),
        name="compress_kv",
    )(kvc, pos, w1, b1, w2)


def _block_diag2(w):
    z = jnp.zeros_like(w)
    return jnp.concatenate([jnp.concatenate([w, z], axis=-1), jnp.concatenate([z, w], axis=-1)], axis=-2)


def _nsa_kernel(*refs):
    c = pl.program_id(2)

    @pl.when(c >= 2)
    def _():
        _nsa_body(True, c, *refs)

    @pl.when(c < 2)
    def _():
        _nsa_body(False, c, *refs)


def _nsa_body(full, c, q_ref, ks_ref, vs_ref, kw_ref, vw_ref, kc_ref, vc_ref, gl_ref, ac_ref, bm_ref, agg_ref,
              o_ref, kcx_scr, qs_scr, m_scr, acc_scr, sa_scr, sb_scr):
    nc = kc_ref.shape[1]
    nb = agg_ref.shape[1]
    q = q_ref[...].reshape(ROWS, LANES)

    rowi = lax.broadcasted_iota(jnp.int32, (LANES, nc), 0)
    coli = lax.broadcasted_iota(jnp.int32, (LANES, nc), 1)
    near0 = c * (TQ // CMP_STRIDE) - NEAR_W // 2
    shift = (coli == near0 + (rowi & (NEAR_W - 1))) & (rowi < 2 * NEAR_W)
    future = (rowi == 2 * NEAR_W) & (coli >= near0 + NEAR_W)
    kcx_scr[0:LANES, :] = kc_ref[...]
    kcx_scr[LANES:2 * LANES, :] = jnp.where(shift | future, 1.0, 0.0).astype(MXU_DTYPE)
    s_c = _dot_halves(jnp.concatenate([q, ac_ref[...]], axis=1), kcx_scr[...])
    m_c = jnp.maximum(jnp.max(s_c, axis=1, keepdims=True), 0.1 * NEG_BIG)
    e_c = jnp.exp(s_c - m_c)
    l_c = jnp.sum(e_c, axis=1, keepdims=True)
    p_c = e_c * (1.0 / jnp.maximum(l_c, 1e-30))
    acc_c = _dot_halves(p_c.astype(MXU_DTYPE), vc_ref[...])

    imp = p_c[0:TQ] + p_c[TQ:2 * TQ] + p_c[2 * TQ:3 * TQ] + p_c[3 * TQ:4 * TQ]
    agg = agg_ref[...]
    i_hi, i_mid, i_lo = _split3(imp)
    imps = _dot(i_hi, agg) + _dot(i_mid, agg) + _dot(i_lo, agg)
    t_q = c * TQ + lax.broadcasted_iota(jnp.int32, (nb, TQ), 1)
    blk = lax.broadcasted_iota(jnp.int32, (nb, TQ), 0)
    cur = lax.shift_right_logical(t_q, int(math.log2(SLC_BLOCK)))
    causal = blk <= cur
    forced = (blk == 0) | (blk == cur) | (blk == cur - 1)
    neg_inf = -jnp.inf
    cand = jnp.where(causal & jnp.logical_not(forced), imps.T, neg_inf)
    sel = forced & causal
    blk_f = blk.astype(F32)
    for _ in range(N_SELECT - 3):
        best = jnp.max(cand, axis=0, keepdims=True)
        first = jnp.min(jnp.where(cand == best, blk_f, float(nb)), axis=0, keepdims=True)
        hit = (blk_f == first) & (best > neg_inf)
        sel = sel | hit
        cand = jnp.where(hit, neg_inf, cand)
    selneg = jnp.where(sel, 0.0, NEG_BIG).T.astype(MXU_DTYPE)
    qs_scr[:, 0:LANES] = q
    qs_scr[:, LANES:2 * LANES] = jnp.concatenate([selneg] * NSA_GROUP, axis=0)

    def flash_init():
        m_scr[...] = jnp.full(m_scr.shape, NEG_BIG, F32)
        acc_scr[...] = jnp.zeros(acc_scr.shape, F32)

    def scores(k_ref, kt):
        start = pl.multiple_of(kt * TQ, TQ)
        return _dot(qs_scr[:, 0:k_ref.shape[0]], k_ref[:, pl.ds(start, TQ)])

    def flash_update(s, v_ref, kt, ntiles=1, dot=_dot):
        start = pl.multiple_of(kt * TQ, TQ)
        m_prev = m_scr[...]
        m_new = jnp.maximum(m_prev, jnp.max(s, axis=1, keepdims=True))
        p = jnp.exp(s - jnp.concatenate([m_new] * (ntiles * TQ // LANES), axis=1))
        acc_scr[...] = jnp.exp(m_prev - m_new) * acc_scr[...] \
            + dot(p.astype(MXU_DTYPE), v_ref[pl.ds(start, ntiles * TQ), :])
        m_scr[...] = m_new

    def flash_step(k_ref, v_ref, kt, bias_tile):
        s = scores(k_ref, kt)
        if bias_tile is not None:
            s = s + bm_ref[:, bias_tile * TQ:(bias_tile + 1) * TQ]
        flash_update(s, v_ref, kt)

    def flash_out():
        acc = acc_scr[...]
        return acc * (1.0 / acc[:, HEAD_DIM:HEAD_DIM + 1])

    if full:
        w0 = pl.multiple_of((c - 2) * TQ, TQ)
        s_w = _dot_halves(q, kw_ref[:, pl.ds(w0, 3 * TQ)]) + bm_ref[...]
        p_w = jnp.exp(s_w - jnp.max(s_w, axis=1, keepdims=True))
        acc_w = _dot_halves(p_w.astype(MXU_DTYPE), vw_ref[pl.ds(w0, 3 * TQ), :])
        o_w = acc_w * (1.0 / acc_w[:, HEAD_DIM:HEAD_DIM + 1])
    else:
        flash_init()

        @pl.when(c >= 1)
        def _():
            flash_step(kw_ref, vw_ref, c - 1, 1)

        flash_step(kw_ref, vw_ref, c, 2)
        o_w = flash_out()

    flash_init()
    if full:
        n_far = c - 1
        n_loop = lax.shift_right_logical(n_far, FAR_UNROLL_LOG2)

        @pl.when(n_far >= 2)
        def _():
            sa_scr[...] = scores(ks_ref, 0)

        def far_pair(t0):
            sb_scr[...] = scores(ks_ref, t0 + 1)
            flash_update(sa_scr[...], vs_ref, t0)
            sa_scr[...] = scores(ks_ref, t0 + 2)
            flash_update(sb_scr[...], vs_ref, t0 + 1)

        def far_block(t0, ntiles):
            for k in range(0, ntiles, 2):
                far_pair(t0 + k)

        def far_body(j, carry):
            far_block(j * (1 << FAR_UNROLL_LOG2), 1 << FAR_UNROLL_LOG2)
            return carry

        lax.fori_loop(0, n_loop, far_body, 0)
        done = n_loop * (1 << FAR_UNROLL_LOG2)
        for bit in range(FAR_UNROLL_LOG2 - 1, 0, -1):
            size = 1 << bit
            take = (n_far & size) != 0

            @pl.when(take)
            def _(done=done, size=size):
                far_block(done, size)

            done = done + jnp.where(take, size, 0)

        @pl.when((n_far & 1) == 1)
        def _():
            flash_step(ks_ref, vs_ref, n_far - 1, None)

        sa_scr[...] = scores(ks_ref, c - 1) + bm_ref[:, TQ:2 * TQ]
        sb_scr[...] = scores(ks_ref, c) + bm_ref[:, 2 * TQ:3 * TQ]
        flash_update(sa_scr[...], vs_ref, c - 1)
        flash_update(sb_scr[...], vs_ref, c)
    else:
        @pl.when(c >= 1)
        def _():
            flash_step(ks_ref, vs_ref, c - 1, 1)

        flash_step(ks_ref, vs_ref, c, 2)
    o_s = flash_out()

    gates = _sigmoid(gl_ref[...])
    lane = lax.broadcasted_iota(jnp.int32, (TQ, LANES), 1)
    outs = []
    for g in range(NSA_GROUP):
        r = slice(g * TQ, (g + 1) * TQ)
        gc = gates[:, g * N_BRANCH + 0:g * N_BRANCH + 1]
        gs = gates[:, g * N_BRANCH + 1:g * N_BRANCH + 2]
        gw = gates[:, g * N_BRANCH + 2:g * N_BRANCH + 3]
        outs.append(gc * acc_c[r] + gs * o_s[r] + gw * o_w[r])
    for pair in range(NSA_GROUP // 2):
        both = jnp.where(lane < HEAD_DIM, outs[2 * pair], pltpu.roll(outs[2 * pair + 1], HEAD_DIM, 1))
        o_ref[:, pair * LANES:(pair + 1) * LANES] = both.astype(o_ref.dtype)


def _nsa(q8, ks, vs, kw, vw, kc, vc, gl, ac, bm, agg):
    _, b, s, _ = q8.shape
    nc = kc.shape[-1]
    nb = agg.shape[1]
    kvmap = lambda bi, h, c: (bi, h, 0, 0)
    hmap = lambda bi, h, c: (h, 0, 0)
    return pl.pallas_call(
        _nsa_kernel,
        grid=(b, NSA_KV_HEADS, s // TQ),
        in_specs=[
            pl.BlockSpec((NSA_GROUP, None, TQ, LANES), lambda bi, h, c: (h, bi, c, 0)),
            pl.BlockSpec((None, None, 2 * LANES, s), kvmap),
            pl.BlockSpec((None, None, s, LANES), kvmap),
            pl.BlockSpec((None, None, LANES, s), kvmap),
            pl.BlockSpec((None, None, s, LANES), kvmap),
            pl.BlockSpec((None, None, LANES, nc), kvmap),
            pl.BlockSpec((None, None, nc, LANES), kvmap),
            pl.BlockSpec((None, None, TQ, LANES), lambda bi, h, c: (h, bi, c, 0)),
            pl.BlockSpec((None, ROWS, LANES), hmap),
            pl.BlockSpec((None, ROWS, 3 * TQ), hmap),
            _full((nc, nb)),
        ],
        out_specs=pl.BlockSpec((None, TQ, NSA_GROUP * HEAD_DIM), lambda bi, h, c: (bi, c, h)),
        out_shape=jax.ShapeDtypeStruct((b, s, NSA_HEADS * HEAD_DIM), MXU_DTYPE),
        scratch_shapes=[
            pltpu.VMEM((2 * LANES, nc), MXU_DTYPE),
            pltpu.VMEM((ROWS, 2 * LANES), MXU_DTYPE),
            pltpu.VMEM((ROWS, LANES), F32),
            pltpu.VMEM((ROWS, LANES), F32),
            pltpu.VMEM((ROWS, TQ), F32),
            pltpu.VMEM((ROWS, TQ), F32),
        ],
        compiler_params=_params("parallel", "parallel", "arbitrary"),
        name="nsa_attention",
    )(q8, ks, vs, kw, vw, kc, vc, gl, ac, bm, agg)


def _conv_kernel(cu_ref, halo_ref, w_ref, b_ref, g_ref, bb_ref, o_ref, hh_scr, sh_scr):
    ts = cu_ref.shape[0]
    ch = CONV_CHANNELS

    def glu(u):
        return u[:, :ch] * _sigmoid(u[:, ch:])

    first = pl.program_id(1) == 0
    hh_scr[0:CONV_HALO, :] = jnp.where(first, 0.0, glu(halo_ref[...]))
    hh_scr[CONV_HALO:CONV_HALO + ts, :] = glu(cu_ref[...])
    hh_scr[CONV_HALO + ts:, :] = jnp.zeros((SUBLANES, ch), F32)
    acc = jnp.broadcast_to(b_ref[...], (ts, ch))
    lead = CONV_HALO - (CONV_WIDTH - 1)
    for shift in range(SUBLANES):
        sh_scr[...] = hh_scr[pl.ds(shift, CONV_HALO + ts), :]
        for start in range(shift, lead + CONV_WIDTH, SUBLANES):
            w = start - lead
            if 0 <= w < CONV_WIDTH:
                acc = acc + sh_scr[pl.ds(start - shift, ts), :] * w_ref[w:w + 1, :]
    mu = jnp.mean(acc, axis=-1, keepdims=True)
    xc = acc - mu
    y = xc * lax.rsqrt(jnp.mean(xc * xc, axis=-1, keepdims=True) + EPS) * g_ref[...] + bb_ref[...]
    o_ref[...] = _silu(y).astype(o_ref.dtype)


def _conv(cu, w, bias, ln_g, ln_b):
    b, s, _ = cu.shape
    ts = min(TS_CONV, s)
    per = ts // CONV_HALO
    return pl.pallas_call(
        _conv_kernel,
        grid=(b, s // ts),
        in_specs=[
            pl.BlockSpec((None, ts, C_CU), lambda bi, i: (bi, i, 0)),
            pl.BlockSpec((None, CONV_HALO, C_CU), lambda bi, i: (bi, jnp.maximum(i * per - 1, 0), 0)),
            _full((CONV_HALO, CONV_CHANNELS)), _full((1, CONV_CHANNELS)),
            _full((1, CONV_CHANNELS)), _full((1, CONV_CHANNELS)),
        ],
        out_specs=pl.BlockSpec((None, ts, CONV_CHANNELS), lambda bi, i: (bi, i, 0)),
        out_shape=jax.ShapeDtypeStruct((b, s, CONV_CHANNELS), MXU_DTYPE),
        scratch_shapes=[pltpu.VMEM((CONV_HALO + ts + SUBLANES, CONV_CHANNELS), F32),
                        pltpu.VMEM((CONV_HALO + ts, CONV_CHANNELS), F32)],
        compiler_params=_params("parallel", "parallel"),
        name="conformer_conv",
    )(cu, cu, w, bias, ln_g, ln_b)


def _gla_kernel(q_ref, k_ref, v_ref, r_ref, ga_ref, wa_ref, ba_ref, g_ref, tri_ref, hm_ref, vm_ref, bmk_ref,
                gm_ref, o_ref, st_scr, sb_scr, o_scr, u_scr):
    ts = q_ref.shape[0]
    nch = ts // GLA_CHUNK
    dk = GLA_KEY_DIM // GLA_HEADS
    cs = GLA_CHUNK

    @pl.when(pl.program_id(1) == 0)
    def _():
        st_scr[...] = jnp.zeros(st_scr.shape, F32)

    x = _dot(ga_ref[...].astype(MXU_DTYPE), wa_ref[...]) + ba_ref[...]
    log_a = (jnp.minimum(x, 0.0) - jnp.log(1.0 + jnp.exp(-jnp.abs(x)))) * (1.0 / GLA_TAU)
    wide = jnp.concatenate([log_a[n * cs:(n + 1) * cs] for n in range(nch)], axis=1)
    l_hi, l_mid, l_lo = _split3(wide)
    tri = tri_ref[...]
    b_wide = _dot(tri, l_hi) + _dot(tri, l_mid) + _dot(tri, l_lo)
    b = jnp.concatenate([b_wide[:, n * GLA_KEY_DIM:(n + 1) * GLA_KEY_DIM] for n in range(nch)], axis=0)
    eb = jnp.exp(b)
    q_t = (q_ref[...] * (dk ** -0.5) * eb).astype(MXU_DTYPE)
    k_all = k_ref[...]
    k_t = (k_all * jnp.exp(-b)).astype(MXU_DTYPE)
    v_all = v_ref[...].astype(MXU_DTYPE)
    causal = tri_ref[...] > 0
    blockmask = bmk_ref[...]

    for n in range(nch):
        r = slice(n * cs, (n + 1) * cs)
        b_last = b[(n + 1) * cs - 1:(n + 1) * cs, :]
        k_d = (k_all[r] * jnp.exp(b_last - b[r])).astype(MXU_DTYPE)
        u_scr[n] = _dot_tn(v_all[r], k_d) * blockmask
    state = st_scr[...]
    for n in range(nch):
        sb_scr[n] = state.astype(MXU_DTYPE)
        b_last = b[(n + 1) * cs - 1:(n + 1) * cs, :]
        state = state * jnp.exp(b_last) + u_scr[n]
    st_scr[...] = state

    causal_h = jnp.concatenate([causal] * GLA_HEADS, axis=0)
    for n in range(nch):
        r = slice(n * cs, (n + 1) * cs)
        qn = q_t[r]
        o = _dot_nt(qn, sb_scr[n])
        q_heads = jnp.concatenate([qn * hm_ref[h:h + 1, :] for h in range(GLA_HEADS)], axis=0)
        attn = jnp.where(causal_h, _dot_nt(q_heads, k_t[r]), 0.0).astype(MXU_DTYPE)
        pv = _dot(attn, v_all[r])
        for h in range(GLA_HEADS):
            o = o + pv[h * cs:(h + 1) * cs] * vm_ref[h:h + 1, :]
        o_scr[r, :] = o
    o = o_scr[...]
    o_hi, o_mid, o_lo = _split3(o * o)
    gm = gm_ref[...]
    ms = _dot(o_hi, gm) + _dot(o_mid, gm) + _dot(o_lo, gm)
    y = o * lax.rsqrt(ms + EPS) * g_ref[...]
    o_ref[...] = (y * _silu(r_ref[...])).astype(o_ref.dtype)


def _gla(gla, misc, wa_pad, ba, g):
    b, s, _ = gla.shape
    ts = min(TS_GLA, s)
    nch = ts // GLA_CHUNK
    dk, dv = GLA_KEY_DIM // GLA_HEADS, GLA_VALUE_DIM // GLA_HEADS
    tri = np.tril(np.ones((GLA_CHUNK, GLA_CHUNK), np.float32))
    hm = (np.arange(GLA_KEY_DIM)[None, :] // dk == np.arange(GLA_HEADS)[:, None]).astype(np.float32)
    vm = (np.arange(GLA_VALUE_DIM)[None, :] // dv == np.arange(GLA_HEADS)[:, None]).astype(np.float32)
    bmk = (np.arange(GLA_VALUE_DIM)[:, None] // dv == np.arange(GLA_KEY_DIM)[None, :] // dk).astype(np.float32)
    gm = (np.arange(GLA_VALUE_DIM)[:, None] // dv == np.arange(GLA_VALUE_DIM)[None, :] // dv).astype(np.float32) / dv
    return pl.pallas_call(
        _gla_kernel,
        grid=(b, s // ts),
        in_specs=[
            pl.BlockSpec((None, ts, GLA_KEY_DIM), lambda bi, i: (bi, i, 0)),
            pl.BlockSpec((None, ts, GLA_KEY_DIM), lambda bi, i: (bi, i, 1)),
            pl.BlockSpec((None, ts, GLA_VALUE_DIM), lambda bi, i: (bi, i, 1)),
            pl.BlockSpec((None, ts, GLA_VALUE_DIM), lambda bi, i: (bi, i, 2)),
            pl.BlockSpec((None, None, ts, LANES), lambda bi, i: (0, bi, i, 0)),
            _full((LANES, GLA_KEY_DIM)), _full((1, GLA_KEY_DIM)), _full((1, GLA_VALUE_DIM)),
            _full(tri.shape), _full(hm.shape), _full(vm.shape), _full(bmk.shape), _full(gm.shape),
        ],
        out_specs=pl.BlockSpec((None, ts, GLA_VALUE_DIM), lambda bi, i: (bi, i, 0)),
        out_shape=jax.ShapeDtypeStruct((b, s, GLA_VALUE_DIM), MXU_DTYPE),
        scratch_shapes=[
            pltpu.VMEM((GLA_VALUE_DIM, GLA_KEY_DIM), F32),
            pltpu.VMEM((nch, GLA_VALUE_DIM, GLA_KEY_DIM), MXU_DTYPE),
            pltpu.VMEM((ts, GLA_VALUE_DIM), F32),
            pltpu.VMEM((nch, GLA_VALUE_DIM, GLA_KEY_DIM), F32),
        ],
        compiler_params=_params("parallel", "arbitrary"),
        name="gla",
    )(gla, gla, gla, gla, misc, wa_pad, ba, g,
      jnp.asarray(tri, MXU_DTYPE), jnp.asarray(hm, MXU_DTYPE), jnp.asarray(vm), jnp.asarray(bmk),
      jnp.asarray(gm, MXU_DTYPE))


def _post_kernel(x_ref, yn_ref, yc_ref, yg_ref, p_ref, wo_ref, gf_ref, wg_ref, wu_ref, wd_ref,
                 gp_ref, wpg_ref, wpp_ref, gfin_ref, o_ref, *, final):
    dn = yn_ref.shape[1]
    dc = yc_ref.shape[1]
    mix = _dot(yn_ref[...], wo_ref[0:dn, :]) + _dot(yc_ref[...], wo_ref[dn:dn + dc, :]) \
        + _dot(yg_ref[...], wo_ref[dn + dc:, :])
    x = x_ref[...] + mix
    h = _rms(x, gf_ref[...]).astype(MXU_DTYPE)
    dff = wg_ref.shape[1]
    step = dff // FF_CHUNKS
    ff = None
    for j in range(FF_CHUNKS):
        cols = slice(j * step, (j + 1) * step)
        act = _silu(_dot(h, wg_ref[:, cols])) * _dot(h, wu_ref[:, cols])
        down = _dot(act.astype(MXU_DTYPE), wd_ref[cols, :])
        ff = down if ff is None else ff + down
    x = x + ff
    gate = _sigmoid(_dot(_rms(x, gp_ref[...]).astype(MXU_DTYPE), wpg_ref[...]))
    x = x + _dot(p_ref[...].astype(MXU_DTYPE), wpp_ref[...]) * gate
    if final:
        x = _rms(x, gfin_ref[...])
    o_ref[...] = x


def _post(x2d, yn, yc, yg, p2d, wo, gf, wg, wu, wd, gp, wpg, wpp, gfin, final):
    t, d = x2d.shape
    tm = min(TM_PROJ, t)
    row = lambda i: (i, 0)
    once = dict(pipeline_mode=pl.Buffered(1))

    def const(a):
        return pl.BlockSpec(a.shape, lambda i: (0,) * a.ndim, **once)

    return pl.pallas_call(
        functools.partial(_post_kernel, final=final),
        grid=(t // tm,),
        in_specs=[
            pl.BlockSpec((tm, d), row),
            pl.BlockSpec((tm, yn.shape[1]), row), pl.BlockSpec((tm, yc.shape[1]), row),
            pl.BlockSpec((tm, yg.shape[1]), row), pl.BlockSpec((tm, p2d.shape[1]), row),
            const(wo), const(gf), const(wg), const(wu), const(wd), const(gp), const(wpg), const(wpp),
            const(gfin),
        ],
        out_specs=pl.BlockSpec((tm, d), row),
        out_shape=jax.ShapeDtypeStruct((t, d), F32),
        compiler_params=_params("parallel"),
        name="post_mixer",
    )(x2d, yn, yc, yg, p2d, wo, gf, wg, wu, wd, gp, wpg, wpp, gfin)


def _layer(x2d, p2d, bsz, seq, rel_tabs, agg, lw, final_g, final):
    (mix_g, w_in, w_out, pos_kv, w1_kv, b1_kv, w2_kv, conv_w, conv_b, conv_ln_g, conv_ln_b,
     w_alpha, b_alpha, gla_g, ffn_g, w_gate, w_up, w_down, ple_g, w_pg, w_pp, qconst) = lw
    bm, ac = rel_tabs
    t = bsz * seq
    hk = NSA_KV_HEADS

    w_all = _gather_columns(w_in, _W_RUNS).astype(MXU_DTYPE)
    wk_t = _gather_columns(w_in, _WK_RUNS).T.astype(MXU_DTYPE)
    q8, ks, kw, vs, vw, kvc, cu, gla, misc = _in_proj(x2d, mix_g[None, :], w_all, wk_t, qconst, bsz, seq)

    nsub = seq // CMP_STRIDE
    cmp = _compress(kvc.reshape(bsz, seq, C_KVC), pos_kv, w1_kv, b1_kv, w2_kv)
    cmp = cmp.reshape(2, bsz, nsub, hk, HEAD_DIM)
    ones_row = jnp.ones((bsz, hk, 2, nsub), MXU_DTYPE)
    kc = jnp.concatenate([cmp[0].transpose(0, 2, 3, 1), ones_row,
                          jnp.zeros((bsz, hk, LANES - HEAD_DIM - 2, nsub), MXU_DTYPE)], axis=2)
    ones_col = jnp.ones((bsz, hk, nsub, 1), MXU_DTYPE)
    vc = jnp.concatenate([cmp[1].transpose(0, 2, 1, 3), ones_col,
                          jnp.zeros((bsz, hk, nsub, LANES - HEAD_DIM - 1), MXU_DTYPE)], axis=3)

    y_nsa = _nsa(q8.reshape(NSA_HEADS, bsz, seq, LANES), ks, vs, kw, vw, kc, vc,
                 misc.reshape(hk, bsz, seq, LANES), ac, bm, agg)

    y_conv = _conv(cu.reshape(bsz, seq, C_CU), conv_w, conv_b[None, :], conv_ln_g[None, :], conv_ln_b[None, :])
    y_gla = _gla(gla.reshape(bsz, seq, C_GLA), misc.reshape(hk, bsz, seq, LANES), w_alpha, b_alpha[None, :],
                 gla_g[None, :])

    return _post(x2d, y_nsa.reshape(t, -1), y_conv.reshape(t, -1), y_gla.reshape(t, -1), p2d,
                 w_out.astype(MXU_DTYPE), ffn_g[None, :], w_gate.astype(MXU_DTYPE), w_up.astype(MXU_DTYPE),
                 w_down.astype(MXU_DTYPE), ple_g[None, :], w_pg.astype(MXU_DTYPE), w_pp.astype(MXU_DTYPE),
                 final_g[None, :], final)


def kernel(x, p, rel_bias, mix_norm_g, w_in, w_out, cmp_pos_k, cmp_w1_k, cmp_b1_k, cmp_w2_k, cmp_pos_v, cmp_w1_v, cmp_b1_v, cmp_w2_v, conv_w, conv_b, conv_ln_g, conv_ln_b, gla_w_alpha, gla_b_alpha, gla_norm_g, ffn_norm_g, ffn_w_gate, ffn_w_up, ffn_w_down, ple_norm_g, ple_w_gate, ple_w_proj, final_norm_g):
    bsz, seq, d = x.shape
    depth = w_in.shape[0]
    assert seq % TQ == 0 and seq // SLC_BLOCK <= LANES
    t = bsz * seq
    rel_tabs = _nsa_bias_tables(rel_bias.astype(F32))
    nbp = LANES
    agg = jnp.asarray(_agg_matrix(seq // CMP_STRIDE, nbp), MXU_DTYPE)

    far = rel_bias[REL_BUCKETS - 1].astype(F32)
    f_hi, f_lo = _split_hi_lo(far)
    qconst = jnp.zeros((NSA_HEADS + 1, LANES), F32)
    qconst = qconst.at[:NSA_HEADS, FAR_LANE].set(f_hi.astype(F32)).at[:NSA_HEADS, FAR_LANE + 1].set(f_lo.astype(F32))
    qconst = qconst.at[NSA_HEADS, HEAD_DIM].set(1.0)

    x2d = x.reshape(t, d)
    for i in range(depth):
        wa_pad = jnp.zeros((LANES, GLA_KEY_DIM), F32).at[GA_LANE0:GA_LANE0 + GLA_GATE_RANK].set(gla_w_alpha[i])
        conv_w_pad = jnp.zeros((CONV_HALO, CONV_CHANNELS), F32).at[:CONV_WIDTH].set(conv_w[i])
        both = lambda a: jnp.concatenate([a] * NSA_KV_HEADS, axis=-1)
        w1_blocks = lambda w: _block_diag2(w.reshape(CMP_BLOCK, HEAD_DIM, CMP_HIDDEN))
        lw = (mix_norm_g[i], w_in[i], w_out[i],
              jnp.stack([both(cmp_pos_k[i]), both(cmp_pos_v[i])]),
              jnp.stack([w1_blocks(cmp_w1_k[i]), w1_blocks(cmp_w1_v[i])]).astype(MXU_DTYPE),
              jnp.stack([both(cmp_b1_k[i])[None, :], both(cmp_b1_v[i])[None, :]]),
              jnp.stack([_block_diag2(cmp_w2_k[i]), _block_diag2(cmp_w2_v[i])]).astype(MXU_DTYPE),
              conv_w_pad, conv_b[i], conv_ln_g[i], conv_ln_b[i],
              wa_pad.astype(MXU_DTYPE), gla_b_alpha[i], gla_norm_g[i],
              ffn_norm_g[i], ffn_w_gate[i], ffn_w_up[i], ffn_w_down[i],
              ple_norm_g[i], ple_w_gate[i], ple_w_proj[i], qconst)
        x2d = _layer(x2d, p[i].reshape(t, -1), bsz, seq, rel_tabs, agg, lw, final_norm_g, i == depth - 1)
    return x2d.reshape(bsz, seq, d)
```

```python
import functools
import math

import numpy as np
import jax
import jax.numpy as jnp
from jax import lax
from jax.experimental import pallas as pl
from jax.experimental.pallas import tpu as pltpu

F32 = jnp.float32
MXU_DTYPE = jnp.bfloat16

HEAD_DIM = 64
NSA_HEADS = 8
NSA_KV_HEADS = 2
NSA_GROUP = NSA_HEADS // NSA_KV_HEADS
CMP_BLOCK = 32
CMP_STRIDE = 16
CMP_HIDDEN = 256
SLC_BLOCK = 64
N_SELECT = 16
WINDOW = 512
N_BRANCH = 3
CONV_CHANNELS = 256
CONV_WIDTH = 31
GLA_HEADS = 4
GLA_KEY_DIM = 128
GLA_VALUE_DIM = 256
GLA_GATE_RANK = 16
GLA_TAU = 16.0
GLA_CHUNK = 64
REL_BUCKETS = 32
REL_MAX_DIST = 128
PLE_DIM = 256
EPS = 1e-6
NEG_BIG = -1e30

LANES = 128
SUBLANES = 8
VMEM_LIMIT_BYTES = 56 * 1024 * 1024

TM_PROJ = 512
TQ = 256
ROWS = NSA_GROUP * TQ
NEAR_W = 2 * TQ // CMP_STRIDE
FAR_UNROLL_LOG2 = 3
TS_CONV = 512
CONV_HALO = 32
TS_GLA = 1024
FF_CHUNKS = 2

C_Q = NSA_HEADS * HEAD_DIM
C_V = 2 * NSA_KV_HEADS * HEAD_DIM
C_KT = 2 * NSA_KV_HEADS * HEAD_DIM
C_KVC = 2 * NSA_KV_HEADS * HEAD_DIM
C_CU = 2 * CONV_CHANNELS
C_GLA = 2 * GLA_KEY_DIM + 2 * GLA_VALUE_DIM
C_MISC = NSA_KV_HEADS * LANES
GA_LANE0 = 16
FAR_LANE = HEAD_DIM


def _split_hi_lo(x):
    hi = x.astype(MXU_DTYPE)
    lo = (x - hi.astype(F32)).astype(MXU_DTYPE)
    return hi, lo


def _split3(x):
    hi = x.astype(MXU_DTYPE)
    r = x - hi.astype(F32)
    mid = r.astype(MXU_DTYPE)
    lo = (r - mid.astype(F32)).astype(MXU_DTYPE)
    return hi, mid, lo


def _dot(a, b):
    return jnp.dot(a, b, preferred_element_type=F32)


def _dot_halves(a, b):
    half = a.shape[0] // 2
    return jnp.concatenate([_dot(a[:half], b), _dot(a[half:], b)], axis=0)


def _dot_nt(a, b):
    return lax.dot_general(a, b, (((1,), (1,)), ((), ())), preferred_element_type=F32)


def _dot_tn(a, b):
    return lax.dot_general(a, b, (((0,), (0,)), ((), ())), preferred_element_type=F32)


def _rms(x, g):
    return x * lax.rsqrt(jnp.mean(x * x, axis=-1, keepdims=True) + EPS) * g


def _sigmoid(x):
    return 1.0 / (1.0 + jnp.exp(-x))


def _silu(x):
    return x * _sigmoid(x)


def _params(*sem):
    return pltpu.CompilerParams(dimension_semantics=sem, vmem_limit_bytes=VMEM_LIMIT_BYTES)


def _full(shape):
    nd = len(shape)
    return pl.BlockSpec(shape, lambda *_: (0,) * nd)


def _t5_bucket_np(dist):
    n = np.maximum(dist, 0)
    max_exact = REL_BUCKETS // 2
    nf = np.maximum(n, 1).astype(np.float32)
    large = max_exact + (np.log(nf / np.float32(max_exact)) / np.float32(math.log(REL_MAX_DIST / max_exact))
                         * np.float32(REL_BUCKETS - max_exact)).astype(np.int32)
    large = np.minimum(large, REL_BUCKETS - 1)
    return np.where(n < max_exact, n, large).astype(np.int32)


def _in_proj_columns(in_splits):
    offs = np.concatenate([[0], np.cumsum(in_splits)])
    (o_nq, o_kc, o_vc, o_ks, o_vs, o_kw, o_vw, o_ng, o_cu, o_gq, o_gk, o_gv, o_ga, o_gr) = offs[:-1]
    src, scale = [], []

    def put(cols, s=1.0):
        src.extend(cols)
        scale.extend([s] * len(cols))

    put(list(range(o_nq, o_nq + C_Q)), HEAD_DIM ** -0.5)
    kvw = NSA_KV_HEADS * HEAD_DIM
    for o in (o_vs, o_vw, o_kc, o_vc):
        put(list(range(o, o + kvw)))
    put(list(range(o_cu, o_cu + C_CU)))
    put(list(range(o_gq, o_gq + GLA_KEY_DIM)))
    put(list(range(o_gk, o_gk + GLA_KEY_DIM)))
    put(list(range(o_gv, o_gv + GLA_VALUE_DIM)))
    put(list(range(o_gr, o_gr + GLA_VALUE_DIM)))
    per = NSA_GROUP * N_BRANCH
    for h in range(NSA_KV_HEADS):
        slab = [-1] * LANES
        slab[:per] = list(range(o_ng + h * per, o_ng + (h + 1) * per))
        if h == 0:
            slab[GA_LANE0:GA_LANE0 + GLA_GATE_RANK] = list(range(o_ga, o_ga + GLA_GATE_RANK))
        put(slab)
    key_src = list(range(o_ks, o_ks + kvw)) + list(range(o_kw, o_kw + kvw))
    return np.asarray(src, np.int32), np.asarray(scale, np.float32), np.asarray(key_src, np.int32)


def _column_runs(src, scale):
    runs, i = [], 0
    while i < len(src):
        j = i + 1
        while j < len(src) and scale[j] == scale[i] and (
                (src[i] < 0 and src[j] < 0) or (src[i] >= 0 and src[j] == src[i] + (j - i))):
            j += 1
        runs.append((int(src[i]), j - i, float(scale[i])))
        i = j
    return runs


IN_SPLITS = (
    NSA_HEADS * HEAD_DIM,
    NSA_KV_HEADS * HEAD_DIM, NSA_KV_HEADS * HEAD_DIM,
    NSA_KV_HEADS * HEAD_DIM, NSA_KV_HEADS * HEAD_DIM,
    NSA_KV_HEADS * HEAD_DIM, NSA_KV_HEADS * HEAD_DIM,
    NSA_HEADS * N_BRANCH,
    2 * CONV_CHANNELS,
    GLA_KEY_DIM, GLA_KEY_DIM, GLA_VALUE_DIM,
    GLA_GATE_RANK,
    GLA_VALUE_DIM,
)
_W_SRC, _W_SCALE, _WK_SRC = _in_proj_columns(IN_SPLITS)
_W_RUNS = _column_runs(_W_SRC, _W_SCALE)
_WK_RUNS = _column_runs(_WK_SRC, np.ones_like(_WK_SRC, np.float32))
C_ALL = C_Q + C_V + C_KVC + C_CU + C_GLA + C_MISC
assert _W_SRC.shape[0] == C_ALL and _WK_SRC.shape[0] == C_KT


def _gather_columns(w, runs):
    parts = []
    for start, width, scale in runs:
        if start < 0:
            parts.append(jnp.zeros((w.shape[0], width), w.dtype))
        else:
            part = w[:, start:start + width]
            parts.append(part if scale == 1.0 else part * scale)
    return jnp.concatenate(parts, axis=1)


def _nsa_bias_tables(rel_bias):
    i = np.arange(TQ)[:, None]
    j = np.arange(TQ)[None, :]
    far = rel_bias[REL_BUCKETS - 1]
    tab = rel_bias - far[None, :]

    d = np.arange(2 * TQ - 1, -2 * TQ - 1, -1)
    by_dist = jnp.where(jnp.asarray(d >= 0)[:, None], jnp.take(tab, jnp.asarray(_t5_bucket_np(d)), axis=0), NEG_BIG).T

    n = by_dist.shape[1]
    skew = jnp.tile(by_dist, (1, TQ))[:, :TQ * (n - 1)].reshape(NSA_HEADS, TQ, n - 1)

    def toeplitz(dist0, step, width):
        first = 2 * TQ - 1 - dist0
        assert first >= TQ - 1 and first + step * (width - 1) < n - 1
        vals = lax.slice_in_dim(skew, first, first + step * (width - 1) + 1, stride=step, axis=2)
        return vals.reshape(NSA_KV_HEADS, ROWS, width)

    tri = jnp.asarray(np.where(j <= i, NEG_BIG, 0.0).astype(np.float32))
    tri = jnp.broadcast_to(jnp.tile(tri, (NSA_GROUP, 1))[None], (NSA_KV_HEADS, ROWS, TQ))
    bm = jnp.concatenate([tri, toeplitz(TQ, 1, 2 * TQ)], axis=-1)

    near = toeplitz((NEAR_W // 2) * CMP_STRIDE - (CMP_BLOCK - 1), CMP_STRIDE, NEAR_W)
    hi, lo = _split_hi_lo(near)
    pad = jnp.zeros((NSA_KV_HEADS, ROWS, LANES - 2 * NEAR_W - 1), MXU_DTYPE)
    big = jnp.full((NSA_KV_HEADS, ROWS, 1), NEG_BIG, MXU_DTYPE)
    ac = jnp.concatenate([hi, lo, big, pad], axis=-1)
    return bm, ac


def _agg_matrix(ncp, nbp):
    rs, rc = SLC_BLOCK // CMP_STRIDE, CMP_BLOCK // CMP_STRIDE
    agg = np.zeros((ncp, nbp), np.float32)
    ncmp = ncp - rc + 1
    for jb in range(nbp):
        for mm in range(rs):
            for nn in range(rc):
                idx = jb * rs + mm - nn
                if 0 <= idx < ncmp:
                    agg[idx, jb] += 1.0
    return agg


def _in_proj_kernel(x_ref, g_ref, w_ref, wk_ref, qc_ref, q_ref, ks_ref, kw_ref, vs_ref, vw_ref,
                    kvc_ref, cu_ref, gla_ref, misc_ref, *, tiles_per_seq):
    h = _rms(x_ref[...], g_ref[...]).astype(MXU_DTYPE)
    tm = h.shape[0]
    low = lax.broadcasted_iota(jnp.int32, (tm, LANES), 1) < HEAD_DIM

    def head_group(z, idx, const):
        g = z[:, (idx // 2) * LANES:(idx // 2 + 1) * LANES]
        if idx % 2:
            g = pltpu.roll(g, HEAD_DIM, 1)
        return jnp.where(low, g, const)

    o = 0
    zq = _dot(h, w_ref[:, o:o + C_Q])
    for hd in range(NSA_HEADS):
        q_ref[hd] = head_group(zq, hd, qc_ref[hd:hd + 1, :]).astype(q_ref.dtype)
    o += C_Q
    zv = _dot(h, w_ref[:, o:o + C_V])
    ones_lane = qc_ref[NSA_HEADS:NSA_HEADS + 1, :]
    for hh in range(NSA_KV_HEADS):
        vs_ref[hh] = head_group(zv, hh, ones_lane).astype(vs_ref.dtype)
        vw_ref[hh] = head_group(zv, NSA_KV_HEADS + hh, ones_lane).astype(vw_ref.dtype)
    o += C_V
    kt = _dot_nt(wk_ref[...], h)
    rows = lax.broadcasted_iota(jnp.int32, (LANES - HEAD_DIM, tm), 0)
    ones_rows = jnp.where(rows < 2, 1.0, 0.0).astype(ks_ref.dtype)
    tok = (pl.program_id(0) % tiles_per_seq) * tm + lax.broadcasted_iota(jnp.int32, (LANES, tm), 1)
    blk = lax.broadcasted_iota(jnp.int32, (LANES, tm), 0)
    blk_rows = jnp.where(lax.shift_right_logical(tok, int(math.log2(SLC_BLOCK))) == blk, 1.0, 0.0).astype(ks_ref.dtype)
    for hh in range(NSA_KV_HEADS):
        ks_ref[hh, 0:HEAD_DIM, :] = kt[hh * HEAD_DIM:(hh + 1) * HEAD_DIM].astype(ks_ref.dtype)
        ks_ref[hh, HEAD_DIM:LANES, :] = ones_rows
        ks_ref[hh, LANES:2 * LANES, :] = blk_rows
        kw_ref[hh, 0:HEAD_DIM, :] = kt[(NSA_KV_HEADS + hh) * HEAD_DIM:(NSA_KV_HEADS + hh + 1) * HEAD_DIM].astype(kw_ref.dtype)
        kw_ref[hh, HEAD_DIM:LANES, :] = ones_rows
    kvc_ref[...] = _dot(h, w_ref[:, o:o + C_KVC])
    o += C_KVC
    cu_ref[...] = _dot(h, w_ref[:, o:o + C_CU])
    o += C_CU
    gla_ref[...] = _dot(h, w_ref[:, o:o + C_GLA])
    o += C_GLA
    zm = _dot(h, w_ref[:, o:o + C_MISC])
    for s in range(NSA_KV_HEADS):
        misc_ref[s] = zm[:, s * LANES:(s + 1) * LANES]


def _in_proj(x2d, g, w_all, wk_t, qconst, bsz, seq):
    t, d = x2d.shape
    tm = min(TM_PROJ, seq)
    tps = seq // tm
    hk = NSA_KV_HEADS
    row = lambda i: (i, 0)
    kmap = lambda i: (i // tps, 0, 0, i % tps)
    vmap = lambda i: (i // tps, 0, i % tps, 0)
    return pl.pallas_call(
        functools.partial(_in_proj_kernel, tiles_per_seq=tps),
        grid=(t // tm,),
        in_specs=[pl.BlockSpec((tm, d), row), _full((1, d)), _full((d, C_ALL)), _full((C_KT, d)),
                  _full((NSA_HEADS + 1, LANES))],
        out_specs=[
            pl.BlockSpec((NSA_HEADS, tm, LANES), lambda i: (0, i, 0)),
            pl.BlockSpec((None, hk, 2 * LANES, tm), kmap),
            pl.BlockSpec((None, hk, LANES, tm), kmap),
            pl.BlockSpec((None, hk, tm, LANES), vmap),
            pl.BlockSpec((None, hk, tm, LANES), vmap),
            pl.BlockSpec((tm, C_KVC), row),
            pl.BlockSpec((tm, C_CU), row),
            pl.BlockSpec((tm, C_GLA), row),
            pl.BlockSpec((NSA_KV_HEADS, tm, LANES), lambda i: (0, i, 0)),
        ],
        out_shape=[
            jax.ShapeDtypeStruct((NSA_HEADS, t, LANES), MXU_DTYPE),
            jax.ShapeDtypeStruct((bsz, hk, 2 * LANES, seq), MXU_DTYPE),
            jax.ShapeDtypeStruct((bsz, hk, LANES, seq), MXU_DTYPE),
            jax.ShapeDtypeStruct((bsz, hk, seq, LANES), MXU_DTYPE),
            jax.ShapeDtypeStruct((bsz, hk, seq, LANES), MXU_DTYPE),
            jax.ShapeDtypeStruct((t, C_KVC), F32),
            jax.ShapeDtypeStruct((t, C_CU), F32),
            jax.ShapeDtypeStruct((t, C_GLA), F32),
            jax.ShapeDtypeStruct((NSA_KV_HEADS, t, LANES), F32),
        ],
        compiler_params=_params("parallel"),
        name="in_proj",
    )(x2d, g, w_all, wk_t, qconst)


def _compress_kernel(x_ref, pos_ref, w1_ref, b1_ref, w2_ref, o_ref):
    nc = o_ref.shape[0]
    top = bot = None
    for l in range(CMP_STRIDE):
        x = x_ref[pl.ds(l, nc, stride=CMP_STRIDE), :]
        t = _dot((x + pos_ref[l:l + 1, :]).astype(MXU_DTYPE), w1_ref[l])
        u = _dot((x + pos_ref[CMP_STRIDE + l:CMP_STRIDE + l + 1, :]).astype(MXU_DTYPE), w1_ref[CMP_STRIDE + l])
        top = t if top is None else top + t
        bot = u if bot is None else bot + u
    hid = _silu(top + pltpu.roll(bot, nc - 1, 0) + b1_ref[...])
    o_ref[...] = _dot(hid.astype(MXU_DTYPE), w2_ref[...]).astype(o_ref.dtype)


def _compress(kvc, pos, w1, b1, w2):
    b, s, _ = kvc.shape
    nc = s // CMP_STRIDE
    hw = NSA_KV_HEADS * HEAD_DIM
    hid = NSA_KV_HEADS * CMP_HIDDEN
    return pl.pallas_call(
        _compress_kernel,
        grid=(2, b),
        in_specs=[
            pl.BlockSpec((None, s, hw), lambda kv, bi: (bi, 0, kv)),
            pl.BlockSpec((None, CMP_BLOCK, hw), lambda kv, bi: (kv, 0, 0)),
            pl.BlockSpec((None, CMP_BLOCK, hw, hid), lambda kv, bi: (kv, 0, 0, 0)),
            pl.BlockSpec((None, 1, hid), lambda kv, bi: (kv, 0, 0)),
            pl.BlockSpec((None, hid, hw), lambda kv, bi: (kv, 0, 0)),
        ],
        out_specs=pl.BlockSpec((None, None, nc, hw), lambda kv, bi: (kv, bi, 0, 0)),
        out_shape=jax.ShapeDtypeStruct((2, b, nc, hw), MXU_DTYPE),
        compiler_params=_params("parallel", "parallel"),
        name="compress_kv",
    )(kvc, pos, w1, b1, w2)


def _block_diag2(w):
    z = jnp.zeros_like(w)
    return jnp.concatenate([jnp.concatenate([w, z], axis=-1), jnp.concatenate([z, w], axis=-1)], axis=-2)


def _nsa_kernel(*refs):
    c = pl.program_id(2)

    @pl.when(c >= 2)
    def _():
        _nsa_body(True, c, *refs)

    @pl.when(c < 2)
    def _():
        _nsa_body(False, c, *refs)


def _nsa_body(full, c, q_ref, ks_ref, vs_ref, kw_ref, vw_ref, kc_ref, vc_ref, gl_ref, ac_ref, bm_ref, agg_ref,
              o_ref, kcx_scr, qs_scr, m_scr, acc_scr, sa_scr, sb_scr, part_scr):
    nc = kc_ref.shape[1]
    nb = agg_ref.shape[1]
    q = q_ref[...].reshape(ROWS, LANES)

    rowi = lax.broadcasted_iota(jnp.int32, (LANES, nc), 0)
    coli = lax.broadcasted_iota(jnp.int32, (LANES, nc), 1)
    near0 = c * (TQ // CMP_STRIDE) - NEAR_W // 2
    shift = (coli == near0 + (rowi & (NEAR_W - 1))) & (rowi < 2 * NEAR_W)
    future = (rowi == 2 * NEAR_W) & (coli >= near0 + NEAR_W)
    kcx_scr[0:LANES, :] = kc_ref[...]
    kcx_scr[LANES:2 * LANES, :] = jnp.where(shift | future, 1.0, 0.0).astype(MXU_DTYPE)
    s_c = _dot_halves(jnp.concatenate([q, ac_ref[...]], axis=1), kcx_scr[...])
    m_c = jnp.maximum(jnp.max(s_c, axis=1, keepdims=True), 0.1 * NEG_BIG)
    e_c = jnp.exp(s_c - m_c)
    l_c = jnp.sum(e_c, axis=1, keepdims=True)
    p_c = e_c * (1.0 / jnp.maximum(l_c, 1e-30))
    acc_c = _dot_halves(p_c.astype(MXU_DTYPE), vc_ref[...])

    imp = p_c[0:TQ] + p_c[TQ:2 * TQ] + p_c[2 * TQ:3 * TQ] + p_c[3 * TQ:4 * TQ]
    agg = agg_ref[...]
    i_hi, i_mid, i_lo = _split3(imp)
    imps = _dot(i_hi, agg) + _dot(i_mid, agg) + _dot(i_lo, agg)
    t_q = c * TQ + lax.broadcasted_iota(jnp.int32, (nb, TQ), 1)
    blk = lax.broadcasted_iota(jnp.int32, (nb, TQ), 0)
    cur = lax.shift_right_logical(t_q, int(math.log2(SLC_BLOCK)))
    causal = blk <= cur
    forced = (blk == 0) | (blk == cur) | (blk == cur - 1)
    neg_inf = -jnp.inf
    cand = jnp.where(causal & jnp.logical_not(forced), imps.T, neg_inf)
    blk_f = blk.astype(F32)
    for _ in range(N_SELECT - 3):
        best = jnp.max(cand, axis=0, keepdims=True)
        first = jnp.min(jnp.where(cand == best, blk_f, float(nb)), axis=0, keepdims=True)
        cand = jnp.where(blk_f == first, neg_inf, cand)
    sel = causal & (forced | (cand == neg_inf))
    selneg = jnp.where(sel, 0.0, NEG_BIG).T.astype(MXU_DTYPE)
    qs_scr[:, 0:LANES] = q
    qs_scr[:, LANES:2 * LANES] = jnp.concatenate([selneg] * NSA_GROUP, axis=0)

    def flash_init():
        m_scr[...] = jnp.full(m_scr.shape, NEG_BIG, F32)
        acc_scr[...] = jnp.zeros(acc_scr.shape, F32)

    def scores(k_ref, kt):
        start = pl.multiple_of(kt * TQ, TQ)
        return _dot(qs_scr[:, 0:k_ref.shape[0]], k_ref[:, pl.ds(start, TQ)])

    def flash_update(s, v_ref, kt, ntiles=1, dot=_dot):
        start = pl.multiple_of(kt * TQ, TQ)
        m_prev = m_scr[...]
        m_new = jnp.maximum(m_prev, jnp.max(s, axis=1, keepdims=True))
        p = jnp.exp(s - jnp.concatenate([m_new] * (ntiles * TQ // LANES), axis=1))
        acc_scr[...] = jnp.exp(m_prev - m_new) * acc_scr[...] \
            + dot(p.astype(MXU_DTYPE), v_ref[pl.ds(start, ntiles * TQ), :])
        m_scr[...] = m_new

    def flash_step(k_ref, v_ref, kt, bias_tile):
        s = scores(k_ref, kt)
        if bias_tile is not None:
            s = s + bm_ref[:, bias_tile * TQ:(bias_tile + 1) * TQ]
        flash_update(s, v_ref, kt)

    def flash_out():
        acc = acc_scr[...]
        return acc * (1.0 / acc[:, HEAD_DIM:HEAD_DIM + 1])

    if full:
        w0 = pl.multiple_of((c - 2) * TQ, TQ)
        s_w = _dot_halves(q, kw_ref[:, pl.ds(w0, 3 * TQ)]) + bm_ref[...]
        p_w = jnp.exp(s_w - jnp.max(s_w, axis=1, keepdims=True))
        acc_w = _dot_halves(p_w.astype(MXU_DTYPE), vw_ref[pl.ds(w0, 3 * TQ), :])
        o_w = acc_w * (1.0 / acc_w[:, HEAD_DIM:HEAD_DIM + 1])
    else:
        flash_init()

        @pl.when(c >= 1)
        def _():
            flash_step(kw_ref, vw_ref, c - 1, 1)

        flash_step(kw_ref, vw_ref, c, 2)
        o_w = flash_out()

    gates = _sigmoid(gl_ref[...])
    for g in range(NSA_GROUP):
        r = slice(g * TQ, (g + 1) * TQ)
        part_scr[r, :] = gates[:, g * N_BRANCH:g * N_BRANCH + 1] * acc_c[r] \
            + gates[:, g * N_BRANCH + 2:g * N_BRANCH + 3] * o_w[r]

    flash_init()
    if full:
        n_far = c - 1
        n_loop = lax.shift_right_logical(n_far, FAR_UNROLL_LOG2)

        @pl.when(n_far >= 2)
        def _():
            sa_scr[...] = scores(ks_ref, 0)

        def far_pair(t0):
            sb_scr[...] = scores(ks_ref, t0 + 1)
            flash_update(sa_scr[...], vs_ref, t0)
            sa_scr[...] = scores(ks_ref, t0 + 2)
            flash_update(sb_scr[...], vs_ref, t0 + 1)

        def far_block(t0, ntiles):
            for k in range(0, ntiles, 2):
                far_pair(t0 + k)

        def far_body(j, carry):
            far_block(j * (1 << FAR_UNROLL_LOG2), 1 << FAR_UNROLL_LOG2)
            return carry

        lax.fori_loop(0, n_loop, far_body, 0)
        done = n_loop * (1 << FAR_UNROLL_LOG2)
        for bit in range(FAR_UNROLL_LOG2 - 1, 0, -1):
            size = 1 << bit
            take = (n_far & size) != 0

            @pl.when(take)
            def _(done=done, size=size):
                far_block(done, size)

            done = done + jnp.where(take, size, 0)

        odd = (n_far & 1) == 1

        @pl.when(n_far == 1)
        def _():
            flash_step(ks_ref, vs_ref, 0, None)

        @pl.when(odd & (n_far > 1))
        def _():
            flash_update(sa_scr[...], vs_ref, n_far - 1)

        @pl.when(odd)
        def _():
            sa_scr[...] = scores(ks_ref, c - 1)

        sa_scr[...] = sa_scr[...] + bm_ref[:, TQ:2 * TQ]
        sb_scr[...] = scores(ks_ref, c) + bm_ref[:, 2 * TQ:3 * TQ]
        flash_update(sa_scr[...], vs_ref, c - 1)
        flash_update(sb_scr[...], vs_ref, c)
    else:
        @pl.when(c >= 1)
        def _():
            flash_step(ks_ref, vs_ref, c - 1, 1)

        flash_step(ks_ref, vs_ref, c, 2)
    o_s = flash_out()

    gates = _sigmoid(gl_ref[...])
    lane = lax.broadcasted_iota(jnp.int32, (TQ, LANES), 1)
    outs = []
    for g in range(NSA_GROUP):
        r = slice(g * TQ, (g + 1) * TQ)
        outs.append(part_scr[r, :] + gates[:, g * N_BRANCH + 1:g * N_BRANCH + 2] * o_s[r])
    for pair in range(NSA_GROUP // 2):
        both = jnp.where(lane < HEAD_DIM, outs[2 * pair], pltpu.roll(outs[2 * pair + 1], HEAD_DIM, 1))
        o_ref[:, pair * LANES:(pair + 1) * LANES] = both.astype(o_ref.dtype)


def _nsa(q8, ks, vs, kw, vw, kc, vc, gl, ac, bm, agg):
    _, b, s, _ = q8.shape
    nc = kc.shape[-1]
    nb = agg.shape[1]
    kvmap = lambda bi, h, c: (bi, h, 0, 0)
    hmap = lambda bi, h, c: (h, 0, 0)
    return pl.pallas_call(
        _nsa_kernel,
        grid=(b, NSA_KV_HEADS, s // TQ),
        in_specs=[
            pl.BlockSpec((NSA_GROUP, None, TQ, LANES), lambda bi, h, c: (h, bi, c, 0)),
            pl.BlockSpec((None, None, 2 * LANES, s), kvmap),
            pl.BlockSpec((None, None, s, LANES), kvmap),
            pl.BlockSpec((None, None, LANES, s), kvmap),
            pl.BlockSpec((None, None, s, LANES), kvmap),
            pl.BlockSpec((None, None, LANES, nc), kvmap),
            pl.BlockSpec((None, None, nc, LANES), kvmap),
            pl.BlockSpec((None, None, TQ, LANES), lambda bi, h, c: (h, bi, c, 0)),
            pl.BlockSpec((None, ROWS, LANES), hmap),
            pl.BlockSpec((None, ROWS, 3 * TQ), hmap),
            _full((nc, nb)),
        ],
        out_specs=pl.BlockSpec((None, TQ, NSA_GROUP * HEAD_DIM), lambda bi, h, c: (bi, c, h)),
        out_shape=jax.ShapeDtypeStruct((b, s, NSA_HEADS * HEAD_DIM), MXU_DTYPE),
        scratch_shapes=[
            pltpu.VMEM((2 * LANES, nc), MXU_DTYPE),
            pltpu.VMEM((ROWS, 2 * LANES), MXU_DTYPE),
            pltpu.VMEM((ROWS, LANES), F32),
            pltpu.VMEM((ROWS, LANES), F32),
            pltpu.VMEM((ROWS, TQ), F32),
            pltpu.VMEM((ROWS, TQ), F32),
            pltpu.VMEM((ROWS, LANES), F32),
        ],
        compiler_params=_params("parallel", "parallel", "arbitrary"),
        name="nsa_attention",
    )(q8, ks, vs, kw, vw, kc, vc, gl, ac, bm, agg)


def _conv_kernel(cu_ref, halo_ref, w_ref, b_ref, g_ref, bb_ref, o_ref, hh_scr, sh_scr):
    ts = cu_ref.shape[0]
    ch = CONV_CHANNELS

    def glu(u):
        return u[:, :ch] * _sigmoid(u[:, ch:])

    first = pl.program_id(1) == 0
    hh_scr[0:CONV_HALO, :] = jnp.where(first, 0.0, glu(halo_ref[...]))
    hh_scr[CONV_HALO:CONV_HALO + ts, :] = glu(cu_ref[...])
    hh_scr[CONV_HALO + ts:, :] = jnp.zeros((SUBLANES, ch), F32)
    acc = jnp.broadcast_to(b_ref[...], (ts, ch))
    lead = CONV_HALO - (CONV_WIDTH - 1)
    for shift in range(SUBLANES):
        sh_scr[...] = hh_scr[pl.ds(shift, CONV_HALO + ts), :]
        for start in range(shift, lead + CONV_WIDTH, SUBLANES):
            w = start - lead
            if 0 <= w < CONV_WIDTH:
                acc = acc + sh_scr[pl.ds(start - shift, ts), :] * w_ref[w:w + 1, :]
    mu = jnp.mean(acc, axis=-1, keepdims=True)
    xc = acc - mu
    y = xc * lax.rsqrt(jnp.mean(xc * xc, axis=-1, keepdims=True) + EPS) * g_ref[...] + bb_ref[...]
    o_ref[...] = _silu(y).astype(o_ref.dtype)


def _conv(cu, w, bias, ln_g, ln_b):
    b, s, _ = cu.shape
    ts = min(TS_CONV, s)
    per = ts // CONV_HALO
    return pl.pallas_call(
        _conv_kernel,
        grid=(b, s // ts),
        in_specs=[
            pl.BlockSpec((None, ts, C_CU), lambda bi, i: (bi, i, 0)),
            pl.BlockSpec((None, CONV_HALO, C_CU), lambda bi, i: (bi, jnp.maximum(i * per - 1, 0), 0)),
            _full((CONV_HALO, CONV_CHANNELS)), _full((1, CONV_CHANNELS)),
            _full((1, CONV_CHANNELS)), _full((1, CONV_CHANNELS)),
        ],
        out_specs=pl.BlockSpec((None, ts, CONV_CHANNELS), lambda bi, i: (bi, i, 0)),
        out_shape=jax.ShapeDtypeStruct((b, s, CONV_CHANNELS), MXU_DTYPE),
        scratch_shapes=[pltpu.VMEM((CONV_HALO + ts + SUBLANES, CONV_CHANNELS), F32),
                        pltpu.VMEM((CONV_HALO + ts, CONV_CHANNELS), F32)],
        compiler_params=_params("parallel", "parallel"),
        name="conformer_conv",
    )(cu, cu, w, bias, ln_g, ln_b)


def _gla_kernel(q_ref, k_ref, v_ref, r_ref, ga_ref, wa_ref, ba_ref, g_ref, tri_ref, hm_ref, vm_ref, bmk_ref,
                gm_ref, o_ref, st_scr, sb_scr, o_scr, u_scr):
    ts = q_ref.shape[0]
    nch = ts // GLA_CHUNK
    dk = GLA_KEY_DIM // GLA_HEADS
    cs = GLA_CHUNK

    @pl.when(pl.program_id(1) == 0)
    def _():
        st_scr[...] = jnp.zeros(st_scr.shape, F32)

    x = _dot(ga_ref[...].astype(MXU_DTYPE), wa_ref[...]) + ba_ref[...]
    log_a = (jnp.minimum(x, 0.0) - jnp.log(1.0 + jnp.exp(-jnp.abs(x)))) * (1.0 / GLA_TAU)
    wide = jnp.concatenate([log_a[n * cs:(n + 1) * cs] for n in range(nch)], axis=1)
    l_hi, l_mid, l_lo = _split3(wide)
    tri = tri_ref[...]
    b_wide = _dot(tri, l_hi) + _dot(tri, l_mid) + _dot(tri, l_lo)
    b = jnp.concatenate([b_wide[:, n * GLA_KEY_DIM:(n + 1) * GLA_KEY_DIM] for n in range(nch)], axis=0)
    eb = jnp.exp(b)
    q_t = (q_ref[...] * (dk ** -0.5) * eb).astype(MXU_DTYPE)
    k_all = k_ref[...]
    k_t = (k_all * jnp.exp(-b)).astype(MXU_DTYPE)
    v_all = v_ref[...].astype(MXU_DTYPE)
    causal = tri_ref[...] > 0
    blockmask = bmk_ref[...]

    for n in range(nch):
        r = slice(n * cs, (n + 1) * cs)
        b_last = b[(n + 1) * cs - 1:(n + 1) * cs, :]
        k_d = (k_all[r] * jnp.exp(b_last - b[r])).astype(MXU_DTYPE)
        u_scr[n] = _dot_tn(v_all[r], k_d) * blockmask
    state = st_scr[...]
    for n in range(nch):
        sb_scr[n] = state.astype(MXU_DTYPE)
        b_last = b[(n + 1) * cs - 1:(n + 1) * cs, :]
        state = state * jnp.exp(b_last) + u_scr[n]
    st_scr[...] = state

    causal_h = jnp.concatenate([causal] * GLA_HEADS, axis=0)
    for n in range(nch):
        r = slice(n * cs, (n + 1) * cs)
        qn = q_t[r]
        o = _dot_nt(qn, sb_scr[n])
        q_heads = jnp.concatenate([qn * hm_ref[h:h + 1, :] for h in range(GLA_HEADS)], axis=0)
        attn = jnp.where(causal_h, _dot_nt(q_heads, k_t[r]), 0.0).astype(MXU_DTYPE)
        pv = _dot(attn, v_all[r])
        for h in range(GLA_HEADS):
            o = o + pv[h * cs:(h + 1) * cs] * vm_ref[h:h + 1, :]
        o_scr[r, :] = o
    o = o_scr[...]
    o_hi, o_mid, o_lo = _split3(o * o)
    gm = gm_ref[...]
    ms = _dot(o_hi, gm) + _dot(o_mid, gm) + _dot(o_lo, gm)
    y = o * lax.rsqrt(ms + EPS) * g_ref[...]
    o_ref[...] = (y * _silu(r_ref[...])).astype(o_ref.dtype)


def _gla(gla, misc, wa_pad, ba, g):
    b, s, _ = gla.shape
    ts = min(TS_GLA, s)
    nch = ts // GLA_CHUNK
    dk, dv = GLA_KEY_DIM // GLA_HEADS, GLA_VALUE_DIM // GLA_HEADS
    tri = np.tril(np.ones((GLA_CHUNK, GLA_CHUNK), np.float32))
    hm = (np.arange(GLA_KEY_DIM)[None, :] // dk == np.arange(GLA_HEADS)[:, None]).astype(np.float32)
    vm = (np.arange(GLA_VALUE_DIM)[None, :] // dv == np.arange(GLA_HEADS)[:, None]).astype(np.float32)
    bmk = (np.arange(GLA_VALUE_DIM)[:, None] // dv == np.arange(GLA_KEY_DIM)[None, :] // dk).astype(np.float32)
    gm = (np.arange(GLA_VALUE_DIM)[:, None] // dv == np.arange(GLA_VALUE_DIM)[None, :] // dv).astype(np.float32) / dv
    return pl.pallas_call(
        _gla_kernel,
        grid=(b, s // ts),
        in_specs=[
            pl.BlockSpec((None, ts, GLA_KEY_DIM), lambda bi, i: (bi, i, 0)),
            pl.BlockSpec((None, ts, GLA_KEY_DIM), lambda bi, i: (bi, i, 1)),
            pl.BlockSpec((None, ts, GLA_VALUE_DIM), lambda bi, i: (bi, i, 1)),
            pl.BlockSpec((None, ts, GLA_VALUE_DIM), lambda bi, i: (bi, i, 2)),
            pl.BlockSpec((None, None, ts, LANES), lambda bi, i: (0, bi, i, 0)),
            _full((LANES, GLA_KEY_DIM)), _full((1, GLA_KEY_DIM)), _full((1, GLA_VALUE_DIM)),
            _full(tri.shape), _full(hm.shape), _full(vm.shape), _full(bmk.shape), _full(gm.shape),
        ],
        out_specs=pl.BlockSpec((None, ts, GLA_VALUE_DIM), lambda bi, i: (bi, i, 0)),
        out_shape=jax.ShapeDtypeStruct((b, s, GLA_VALUE_DIM), MXU_DTYPE),
        scratch_shapes=[
            pltpu.VMEM((GLA_VALUE_DIM, GLA_KEY_DIM), F32),
            pltpu.VMEM((nch, GLA_VALUE_DIM, GLA_KEY_DIM), MXU_DTYPE),
            pltpu.VMEM((ts, GLA_VALUE_DIM), F32),
            pltpu.VMEM((nch, GLA_VALUE_DIM, GLA_KEY_DIM), F32),
        ],
        compiler_params=_params("parallel", "arbitrary"),
        name="gla",
    )(gla, gla, gla, gla, misc, wa_pad, ba, g,
      jnp.asarray(tri, MXU_DTYPE), jnp.asarray(hm, MXU_DTYPE), jnp.asarray(vm), jnp.asarray(bmk),
      jnp.asarray(gm, MXU_DTYPE))


def _post_kernel(x_ref, yn_ref, yc_ref, yg_ref, p_ref, wo_ref, gf_ref, wg_ref, wu_ref, wd_ref,
                 gp_ref, wpg_ref, wpp_ref, gfin_ref, o_ref, *, final):
    dn = yn_ref.shape[1]
    dc = yc_ref.shape[1]
    mix = _dot(yn_ref[...], wo_ref[0:dn, :]) + _dot(yc_ref[...], wo_ref[dn:dn + dc, :]) \
        + _dot(yg_ref[...], wo_ref[dn + dc:, :])
    x = x_ref[...] + mix
    h = _rms(x, gf_ref[...]).astype(MXU_DTYPE)
    dff = wg_ref.shape[1]
    step = dff // FF_CHUNKS
    ff = None
    for j in range(FF_CHUNKS):
        cols = slice(j * step, (j + 1) * step)
        act = _silu(_dot(h, wg_ref[:, cols])) * _dot(h, wu_ref[:, cols])
        down = _dot(act.astype(MXU_DTYPE), wd_ref[cols, :])
        ff = down if ff is None else ff + down
    x = x + ff
    gate = _sigmoid(_dot(_rms(x, gp_ref[...]).astype(MXU_DTYPE), wpg_ref[...]))
    x = x + _dot(p_ref[...].astype(MXU_DTYPE), wpp_ref[...]) * gate
    if final:
        x = _rms(x, gfin_ref[...])
    o_ref[...] = x


def _post(x2d, yn, yc, yg, p2d, wo, gf, wg, wu, wd, gp, wpg, wpp, gfin, final):
    t, d = x2d.shape
    tm = min(TM_PROJ, t)
    row = lambda i: (i, 0)
    once = dict(pipeline_mode=pl.Buffered(1))

    def const(a):
        return pl.BlockSpec(a.shape, lambda i: (0,) * a.ndim, **once)

    return pl.pallas_call(
        functools.partial(_post_kernel, final=final),
        grid=(t // tm,),
        in_specs=[
            pl.BlockSpec((tm, d), row),
            pl.BlockSpec((tm, yn.shape[1]), row), pl.BlockSpec((tm, yc.shape[1]), row),
            pl.BlockSpec((tm, yg.shape[1]), row), pl.BlockSpec((tm, p2d.shape[1]), row),
            const(wo), const(gf), const(wg), const(wu), const(wd), const(gp), const(wpg), const(wpp),
            const(gfin),
        ],
        out_specs=pl.BlockSpec((tm, d), row),
        out_shape=jax.ShapeDtypeStruct((t, d), F32),
        compiler_params=_params("parallel"),
        name="post_mixer",
    )(x2d, yn, yc, yg, p2d, wo, gf, wg, wu, wd, gp, wpg, wpp, gfin)


def _layer(x2d, p2d, bsz, seq, rel_tabs, agg, lw, final_g, final):
    (mix_g, w_in, w_out, pos_kv, w1_kv, b1_kv, w2_kv, conv_w, conv_b, conv_ln_g, conv_ln_b,
     w_alpha, b_alpha, gla_g, ffn_g, w_gate, w_up, w_down, ple_g, w_pg, w_pp, qconst) = lw
    bm, ac = rel_tabs
    t = bsz * seq
    hk = NSA_KV_HEADS

    w_all = _gather_columns(w_in, _W_RUNS).astype(MXU_DTYPE)
    wk_t = _gather_columns(w_in, _WK_RUNS).T.astype(MXU_DTYPE)
    q8, ks, kw, vs, vw, kvc, cu, gla, misc = _in_proj(x2d, mix_g[None, :], w_all, wk_t, qconst, bsz, seq)

    nsub = seq // CMP_STRIDE
    cmp = _compress(kvc.reshape(bsz, seq, C_KVC), pos_kv, w1_kv, b1_kv, w2_kv)
    cmp = cmp.reshape(2, bsz, nsub, hk, HEAD_DIM)
    ones_row = jnp.ones((bsz, hk, 2, nsub), MXU_DTYPE)
    kc = jnp.concatenate([cmp[0].transpose(0, 2, 3, 1), ones_row,
                          jnp.zeros((bsz, hk, LANES - HEAD_DIM - 2, nsub), MXU_DTYPE)], axis=2)
    ones_col = jnp.ones((bsz, hk, nsub, 1), MXU_DTYPE)
    vc = jnp.concatenate([cmp[1].transpose(0, 2, 1, 3), ones_col,
                          jnp.zeros((bsz, hk, nsub, LANES - HEAD_DIM - 1), MXU_DTYPE)], axis=3)

    y_nsa = _nsa(q8.reshape(NSA_HEADS, bsz, seq, LANES), ks, vs, kw, vw, kc, vc,
                 misc.reshape(hk, bsz, seq, LANES), ac, bm, agg)

    y_conv = _conv(cu.reshape(bsz, seq, C_CU), conv_w, conv_b[None, :], conv_ln_g[None, :], conv_ln_b[None, :])
    y_gla = _gla(gla.reshape(bsz, seq, C_GLA), misc.reshape(hk, bsz, seq, LANES), w_alpha, b_alpha[None, :],
                 gla_g[None, :])

    return _post(x2d, y_nsa.reshape(t, -1), y_conv.reshape(t, -1), y_gla.reshape(t, -1), p2d,
                 w_out.astype(MXU_DTYPE), ffn_g[None, :], w_gate.astype(MXU_DTYPE), w_up.astype(MXU_DTYPE),
                 w_down.astype(MXU_DTYPE), ple_g[None, :], w_pg.astype(MXU_DTYPE), w_pp.astype(MXU_DTYPE),
                 final_g[None, :], final)


def kernel(x, p, rel_bias, mix_norm_g, w_in, w_out, cmp_pos_k, cmp_w1_k, cmp_b1_k, cmp_w2_k, cmp_pos_v, cmp_w1_v, cmp_b1_v, cmp_w2_v, conv_w, conv_b, conv_ln_g, conv_ln_b, gla_w_alpha, gla_b_alpha, gla_norm_g, ffn_norm_g, ffn_w_gate, ffn_w_up, ffn_w_down, ple_norm_g, ple_w_gate, ple_w_proj, final_norm_g):
    bsz, seq, d = x.shape
    depth = w_in.shape[0]
    assert seq % TQ == 0 and seq // SLC_BLOCK <= LANES
    t = bsz * seq
    rel_tabs = _nsa_bias_tables(rel_bias.astype(F32))
    nbp = LANES
    agg = jnp.asarray(_agg_matrix(seq // CMP_STRIDE, nbp), MXU_DTYPE)

    far = rel_bias[REL_BUCKETS - 1].astype(F32)
    f_hi, f_lo = _split_hi_lo(far)
    qconst = jnp.zeros((NSA_HEADS + 1, LANES), F32)
    qconst = qconst.at[:NSA_HEADS, FAR_LANE].set(f_hi.astype(F32)).at[:NSA_HEADS, FAR_LANE + 1].set(f_lo.astype(F32))
    qconst = qconst.at[NSA_HEADS, HEAD_DIM].set(1.0)

    x2d = x.reshape(t, d)
    for i in range(depth):
        wa_pad = jnp.zeros((LANES, GLA_KEY_DIM), F32).at[GA_LANE0:GA_LANE0 + GLA_GATE_RANK].set(gla_w_alpha[i])
        conv_w_pad = jnp.zeros((CONV_HALO, CONV_CHANNELS), F32).at[:CONV_WIDTH].set(conv_w[i])
        both = lambda a: jnp.concatenate([a] * NSA_KV_HEADS, axis=-1)
        w1_blocks = lambda w: _block_diag2(w.reshape(CMP_BLOCK, HEAD_DIM, CMP_HIDDEN))
        lw = (mix_norm_g[i], w_in[i], w_out[i],
              jnp.stack([both(cmp_pos_k[i]), both(cmp_pos_v[i])]),
              jnp.stack([w1_blocks(cmp_w1_k[i]), w1_blocks(cmp_w1_v[i])]).astype(MXU_DTYPE),
              jnp.stack([both(cmp_b1_k[i])[None, :], both(cmp_b1_v[i])[None, :]]),
              jnp.stack([_block_diag2(cmp_w2_k[i]), _block_diag2(cmp_w2_v[i])]).astype(MXU_DTYPE),
              conv_w_pad, conv_b[i], conv_ln_g[i], conv_ln_b[i],
              wa_pad.astype(MXU_DTYPE), gla_b_alpha[i], gla_norm_g[i],
              ffn_norm_g[i], ffn_w_gate[i], ffn_w_up[i], ffn_w_down[i],
              ple_norm_g[i], ple_w_gate[i], ple_w_proj[i], qconst)
        x2d = _layer(x2d, p[i].reshape(t, -1), bsz, seq, rel_tabs, agg, lw, final_norm_g, i == depth - 1)
    return x2d.reshape(bsz, seq, d)
```

```python
import functools
import math

import numpy as np
import jax
import jax.numpy as jnp
from jax import lax
from jax.experimental import pallas as pl
from jax.experimental.pallas import tpu as pltpu

F32 = jnp.float32
MXU_DTYPE = jnp.bfloat16

HEAD_DIM = 64
NSA_HEADS = 8
NSA_KV_HEADS = 2
NSA_GROUP = NSA_HEADS // NSA_KV_HEADS
CMP_BLOCK = 32
CMP_STRIDE = 16
CMP_HIDDEN = 256
SLC_BLOCK = 64
N_SELECT = 16
WINDOW = 512
N_BRANCH = 3
CONV_CHANNELS = 256
CONV_WIDTH = 31
GLA_HEADS = 4
GLA_KEY_DIM = 128
GLA_VALUE_DIM = 256
GLA_GATE_RANK = 16
GLA_TAU = 16.0
GLA_CHUNK = 64
REL_BUCKETS = 32
REL_MAX_DIST = 128
PLE_DIM = 256
EPS = 1e-6
NEG_BIG = -1e30

LANES = 128
SUBLANES = 8
VMEM_LIMIT_BYTES = 56 * 1024 * 1024

TM_PROJ = 512
TQ = 256
ROWS = NSA_GROUP * TQ
NEAR_W = 2 * TQ // CMP_STRIDE
FAR_UNROLL_LOG2 = 3
assert WINDOW == 2 * TQ and REL_MAX_DIST <= TQ and TQ % SLC_BLOCK == 0 and TQ % CMP_STRIDE == 0
TS_CONV = 512
CONV_HALO = 32
TS_GLA = 1024
FF_CHUNKS = 2

C_Q = NSA_HEADS * HEAD_DIM
C_V = 2 * NSA_KV_HEADS * HEAD_DIM
C_KT = 2 * NSA_KV_HEADS * HEAD_DIM
C_KVC = 2 * NSA_KV_HEADS * HEAD_DIM
C_CU = 2 * CONV_CHANNELS
C_GLA = 2 * GLA_KEY_DIM + 2 * GLA_VALUE_DIM
C_MISC = NSA_KV_HEADS * LANES
GA_LANE0 = 16
FAR_LANE = HEAD_DIM


def _split_hi_lo(x):
    hi = x.astype(MXU_DTYPE)
    lo = (x - hi.astype(F32)).astype(MXU_DTYPE)
    return hi, lo


def _split3(x):
    hi = x.astype(MXU_DTYPE)
    r = x - hi.astype(F32)
    mid = r.astype(MXU_DTYPE)
    lo = (r - mid.astype(F32)).astype(MXU_DTYPE)
    return hi, mid, lo


def _dot(a, b):
    return jnp.dot(a, b, preferred_element_type=F32)


def _dot_halves(a, b):
    half = a.shape[0] // 2
    return jnp.concatenate([_dot(a[:half], b), _dot(a[half:], b)], axis=0)


def _dot_nt(a, b):
    return lax.dot_general(a, b, (((1,), (1,)), ((), ())), preferred_element_type=F32)


def _dot_tn(a, b):
    return lax.dot_general(a, b, (((0,), (0,)), ((), ())), preferred_element_type=F32)


def _rms(x, g):
    return x * lax.rsqrt(jnp.mean(x * x, axis=-1, keepdims=True) + EPS) * g


def _sigmoid(x):
    return 1.0 / (1.0 + jnp.exp(-x))


def _silu(x):
    return x * _sigmoid(x)


def _params(*sem):
    return pltpu.CompilerParams(dimension_semantics=sem, vmem_limit_bytes=VMEM_LIMIT_BYTES)


def _full(shape):
    nd = len(shape)
    return pl.BlockSpec(shape, lambda *_: (0,) * nd)


def _t5_bucket_np(dist):
    n = np.maximum(dist, 0)
    max_exact = REL_BUCKETS // 2
    nf = np.maximum(n, 1).astype(np.float32)
    large = max_exact + (np.log(nf / np.float32(max_exact)) / np.float32(math.log(REL_MAX_DIST / max_exact))
                         * np.float32(REL_BUCKETS - max_exact)).astype(np.int32)
    large = np.minimum(large, REL_BUCKETS - 1)
    return np.where(n < max_exact, n, large).astype(np.int32)


def _in_proj_columns(in_splits):
    offs = np.concatenate([[0], np.cumsum(in_splits)])
    (o_nq, o_kc, o_vc, o_ks, o_vs, o_kw, o_vw, o_ng, o_cu, o_gq, o_gk, o_gv, o_ga, o_gr) = offs[:-1]
    src, scale = [], []

    def put(cols, s=1.0):
        src.extend(cols)
        scale.extend([s] * len(cols))

    put(list(range(o_nq, o_nq + C_Q)), HEAD_DIM ** -0.5)
    kvw = NSA_KV_HEADS * HEAD_DIM
    for o in (o_vs, o_vw, o_kc, o_vc):
        put(list(range(o, o + kvw)))
    put(list(range(o_cu, o_cu + C_CU)))
    put(list(range(o_gq, o_gq + GLA_KEY_DIM)))
    put(list(range(o_gk, o_gk + GLA_KEY_DIM)))
    put(list(range(o_gv, o_gv + GLA_VALUE_DIM)))
    put(list(range(o_gr, o_gr + GLA_VALUE_DIM)))
    per = NSA_GROUP * N_BRANCH
    for h in range(NSA_KV_HEADS):
        slab = [-1] * LANES
        slab[:per] = list(range(o_ng + h * per, o_ng + (h + 1) * per))
        if h == 0:
            slab[GA_LANE0:GA_LANE0 + GLA_GATE_RANK] = list(range(o_ga, o_ga + GLA_GATE_RANK))
        put(slab)
    key_src = list(range(o_ks, o_ks + kvw)) + list(range(o_kw, o_kw + kvw))
    return np.asarray(src, np.int32), np.asarray(scale, np.float32), np.asarray(key_src, np.int32)


def _column_runs(src, scale):
    runs, i = [], 0
    while i < len(src):
        j = i + 1
        while j < len(src) and scale[j] == scale[i] and (
                (src[i] < 0 and src[j] < 0) or (src[i] >= 0 and src[j] == src[i] + (j - i))):
            j += 1
        runs.append((int(src[i]), j - i, float(scale[i])))
        i = j
    return runs


IN_SPLITS = (
    NSA_HEADS * HEAD_DIM,
    NSA_KV_HEADS * HEAD_DIM, NSA_KV_HEADS * HEAD_DIM,
    NSA_KV_HEADS * HEAD_DIM, NSA_KV_HEADS * HEAD_DIM,
    NSA_KV_HEADS * HEAD_DIM, NSA_KV_HEADS * HEAD_DIM,
    NSA_HEADS * N_BRANCH,
    2 * CONV_CHANNELS,
    GLA_KEY_DIM, GLA_KEY_DIM, GLA_VALUE_DIM,
    GLA_GATE_RANK,
    GLA_VALUE_DIM,
)
_W_SRC, _W_SCALE, _WK_SRC = _in_proj_columns(IN_SPLITS)
_W_RUNS = _column_runs(_W_SRC, _W_SCALE)
_WK_RUNS = _column_runs(_WK_SRC, np.ones_like(_WK_SRC, np.float32))
C_ALL = C_Q + C_V + C_KVC + C_CU + C_GLA + C_MISC
assert _W_SRC.shape[0] == C_ALL and _WK_SRC.shape[0] == C_KT


def _gather_columns(w, runs):
    parts = []
    for start, width, scale in runs:
        if start < 0:
            parts.append(jnp.zeros((w.shape[0], width), w.dtype))
        else:
            part = w[:, start:start + width]
            parts.append(part if scale == 1.0 else part * scale)
    return jnp.concatenate(parts, axis=1)


def _nsa_bias_tables(rel_bias):
    i = np.arange(TQ)[:, None]
    j = np.arange(TQ)[None, :]
    far = rel_bias[REL_BUCKETS - 1]
    tab = rel_bias - far[None, :]

    d = np.arange(2 * TQ - 1, -2 * TQ - 1, -1)
    by_dist = jnp.where(jnp.asarray(d >= 0)[:, None], jnp.take(tab, jnp.asarray(_t5_bucket_np(d)), axis=0), NEG_BIG).T

    n = by_dist.shape[1]
    skew = jnp.tile(by_dist, (1, TQ))[:, :TQ * (n - 1)].reshape(NSA_HEADS, TQ, n - 1)

    def toeplitz(dist0, step, width):
        first = 2 * TQ - 1 - dist0
        assert first >= TQ - 1 and first + step * (width - 1) < n - 1
        vals = lax.slice_in_dim(skew, first, first + step * (width - 1) + 1, stride=step, axis=2)
        return vals.reshape(NSA_KV_HEADS, ROWS, width)

    tri = jnp.asarray(np.where(j <= i, NEG_BIG, 0.0).astype(np.float32))
    tri = jnp.broadcast_to(jnp.tile(tri, (NSA_GROUP, 1))[None], (NSA_KV_HEADS, ROWS, TQ))
    bm = jnp.concatenate([tri, toeplitz(TQ, 1, 2 * TQ)], axis=-1)

    near = toeplitz((NEAR_W // 2) * CMP_STRIDE - (CMP_BLOCK - 1), CMP_STRIDE, NEAR_W)
    hi, lo = _split_hi_lo(near)
    pad = jnp.zeros((NSA_KV_HEADS, ROWS, LANES - 2 * NEAR_W - 1), MXU_DTYPE)
    big = jnp.full((NSA_KV_HEADS, ROWS, 1), NEG_BIG, MXU_DTYPE)
    ac = jnp.concatenate([hi, lo, big, pad], axis=-1)
    return bm, ac


def _agg_matrix(ncp, nbp):
    rs, rc = SLC_BLOCK // CMP_STRIDE, CMP_BLOCK // CMP_STRIDE
    agg = np.zeros((ncp, nbp), np.float32)
    ncmp = ncp - rc + 1
    for jb in range(nbp):
        for mm in range(rs):
            for nn in range(rc):
                idx = jb * rs + mm - nn
                if 0 <= idx < ncmp:
                    agg[idx, jb] += 1.0
    return agg


def _in_proj_kernel(x_ref, g_ref, w_ref, wk_ref, qc_ref, q_ref, ks_ref, kw_ref, vs_ref, vw_ref,
                    kvc_ref, cu_ref, gla_ref, misc_ref, *, tiles_per_seq):
    h = _rms(x_ref[...], g_ref[...]).astype(MXU_DTYPE)
    tm = h.shape[0]
    low = lax.broadcasted_iota(jnp.int32, (tm, LANES), 1) < HEAD_DIM

    def head_group(z, idx, const):
        g = z[:, (idx // 2) * LANES:(idx // 2 + 1) * LANES]
        if idx % 2:
            g = pltpu.roll(g, HEAD_DIM, 1)
        return jnp.where(low, g, const)

    o = 0
    zq = _dot(h, w_ref[:, o:o + C_Q])
    for hd in range(NSA_HEADS):
        q_ref[hd] = head_group(zq, hd, qc_ref[hd:hd + 1, :]).astype(q_ref.dtype)
    o += C_Q
    zv = _dot(h, w_ref[:, o:o + C_V])
    ones_lane = qc_ref[NSA_HEADS:NSA_HEADS + 1, :]
    for hh in range(NSA_KV_HEADS):
        vs_ref[hh] = head_group(zv, hh, ones_lane).astype(vs_ref.dtype)
        vw_ref[hh] = head_group(zv, NSA_KV_HEADS + hh, ones_lane).astype(vw_ref.dtype)
    o += C_V
    kt = _dot_nt(wk_ref[...], h)
    rows = lax.broadcasted_iota(jnp.int32, (LANES - HEAD_DIM, tm), 0)
    ones_rows = jnp.where(rows < 2, 1.0, 0.0).astype(ks_ref.dtype)
    tok = (pl.program_id(0) % tiles_per_seq) * tm + lax.broadcasted_iota(jnp.int32, (LANES, tm), 1)
    blk = lax.broadcasted_iota(jnp.int32, (LANES, tm), 0)
    blk_rows = jnp.where(lax.shift_right_logical(tok, int(math.log2(SLC_BLOCK))) == blk, 1.0, 0.0).astype(ks_ref.dtype)
    for hh in range(NSA_KV_HEADS):
        ks_ref[hh, 0:HEAD_DIM, :] = kt[hh * HEAD_DIM:(hh + 1) * HEAD_DIM].astype(ks_ref.dtype)
        ks_ref[hh, HEAD_DIM:LANES, :] = ones_rows
        ks_ref[hh, LANES:2 * LANES, :] = blk_rows
        kw_ref[hh, 0:HEAD_DIM, :] = kt[(NSA_KV_HEADS + hh) * HEAD_DIM:(NSA_KV_HEADS + hh + 1) * HEAD_DIM].astype(kw_ref.dtype)
        kw_ref[hh, HEAD_DIM:LANES, :] = ones_rows
    kvc_ref[...] = _dot(h, w_ref[:, o:o + C_KVC])
    o += C_KVC
    cu_ref[...] = _dot(h, w_ref[:, o:o + C_CU])
    o += C_CU
    gla_ref[...] = _dot(h, w_ref[:, o:o + C_GLA])
    o += C_GLA
    zm = _dot(h, w_ref[:, o:o + C_MISC])
    for s in range(NSA_KV_HEADS):
        misc_ref[s] = zm[:, s * LANES:(s + 1) * LANES]


def _in_proj(x2d, g, w_all, wk_t, qconst, bsz, seq):
    t, d = x2d.shape
    tm = min(TM_PROJ, seq)
    tps = seq // tm
    hk = NSA_KV_HEADS
    row = lambda i: (i, 0)
    kmap = lambda i: (i // tps, 0, 0, i % tps)
    vmap = lambda i: (i // tps, 0, i % tps, 0)
    return pl.pallas_call(
        functools.partial(_in_proj_kernel, tiles_per_seq=tps),
        grid=(t // tm,),
        in_specs=[pl.BlockSpec((tm, d), row), _full((1, d)), _full((d, C_ALL)), _full((C_KT, d)),
                  _full((NSA_HEADS + 1, LANES))],
        out_specs=[
            pl.BlockSpec((NSA_HEADS, tm, LANES), lambda i: (0, i, 0)),
            pl.BlockSpec((None, hk, 2 * LANES, tm), kmap),
            pl.BlockSpec((None, hk, LANES, tm), kmap),
            pl.BlockSpec((None, hk, tm, LANES), vmap),
            pl.BlockSpec((None, hk, tm, LANES), vmap),
            pl.BlockSpec((tm, C_KVC), row),
            pl.BlockSpec((tm, C_CU), row),
            pl.BlockSpec((tm, C_GLA), row),
            pl.BlockSpec((NSA_KV_HEADS, tm, LANES), lambda i: (0, i, 0)),
        ],
        out_shape=[
            jax.ShapeDtypeStruct((NSA_HEADS, t, LANES), MXU_DTYPE),
            jax.ShapeDtypeStruct((bsz, hk, 2 * LANES, seq), MXU_DTYPE),
            jax.ShapeDtypeStruct((bsz, hk, LANES, seq), MXU_DTYPE),
            jax.ShapeDtypeStruct((bsz, hk, seq, LANES), MXU_DTYPE),
            jax.ShapeDtypeStruct((bsz, hk, seq, LANES), MXU_DTYPE),
            jax.ShapeDtypeStruct((t, C_KVC), F32),
            jax.ShapeDtypeStruct((t, C_CU), F32),
            jax.ShapeDtypeStruct((t, C_GLA), F32),
            jax.ShapeDtypeStruct((NSA_KV_HEADS, t, LANES), F32),
        ],
        compiler_params=_params("parallel"),
        name="in_proj",
    )(x2d, g, w_all, wk_t, qconst)


def _compress_kernel(x_ref, pos_ref, w1_ref, b1_ref, w2_ref, o_ref):
    nc = o_ref.shape[0]
    top = bot = None
    for l in range(CMP_STRIDE):
        x = x_ref[pl.ds(l, nc, stride=CMP_STRIDE), :]
        t = _dot((x + pos_ref[l:l + 1, :]).astype(MXU_DTYPE), w1_ref[l])
        u = _dot((x + pos_ref[CMP_STRIDE + l:CMP_STRIDE + l + 1, :]).astype(MXU_DTYPE), w1_ref[CMP_STRIDE + l])
        top = t if top is None else top + t
        bot = u if bot is None else bot + u
    hid = _silu(top + pltpu.roll(bot, nc - 1, 0) + b1_ref[...])
    o_ref[...] = _dot(hid.astype(MXU_DTYPE), w2_ref[...]).astype(o_ref.dtype)


def _compress(kvc, pos, w1, b1, w2):
    b, s, _ = kvc.shape
    nc = s // CMP_STRIDE
    hw = NSA_KV_HEADS * HEAD_DIM
    hid = NSA_KV_HEADS * CMP_HIDDEN
    return pl.pallas_call(
        _compress_kernel,
        grid=(2, b),
        in_specs=[
            pl.BlockSpec((None, s, hw), lambda kv, bi: (bi, 0, kv)),
            pl.BlockSpec((None, CMP_BLOCK, hw), lambda kv, bi: (kv, 0, 0)),
            pl.BlockSpec((None, CMP_BLOCK, hw, hid), lambda kv, bi: (kv, 0, 0, 0)),
            pl.BlockSpec((None, 1, hid), lambda kv, bi: (kv, 0, 0)),
            pl.BlockSpec((None, hid, hw), lambda kv, bi: (kv, 0, 0)),
        ],
        out_specs=pl.BlockSpec((None, None, nc, hw), lambda kv, bi: (kv, bi, 0, 0)),
        out_shape=jax.ShapeDtypeStruct((2, b, nc, hw), MXU_DTYPE),
        compiler_params=_params("parallel", "parallel"),
        name="compress_kv",
    )(kvc, pos, w1, b1, w2)


def _block_diag2(w):
    z = jnp.zeros_like(w)
    return jnp.concatenate([jnp.concatenate([w, z], axis=-1), jnp.concatenate([z, w], axis=-1)], axis=-2)


def _nsa_kernel(*refs):
    c = pl.program_id(2)

    @pl.when(c >= 2)
    def _():
        _nsa_body(True, c, *refs)

    @pl.when(c < 2)
    def _():
        _nsa_body(False, c, *refs)


def _nsa_block_masks(tile, nb):
    t_q = tile * TQ + lax.broadcasted_iota(jnp.int32, (nb, TQ), 1)
    blk = lax.broadcasted_iota(jnp.int32, (nb, TQ), 0)
    cur = lax.shift_right_logical(t_q, int(math.log2(SLC_BLOCK)))
    return blk, blk <= cur, (blk == 0) | (blk == cur) | (blk == cur - 1)


def _nsa_compressed(tile, q, kc_ref, vc_ref, ac_ref, agg_ref, kcx_scr):
    nc = kc_ref.shape[1]
    nb = agg_ref.shape[1]
    rowi = lax.broadcasted_iota(jnp.int32, (LANES, nc), 0)
    coli = lax.broadcasted_iota(jnp.int32, (LANES, nc), 1)
    near0 = tile * (TQ // CMP_STRIDE) - NEAR_W // 2
    shift = (coli == near0 + (rowi & (NEAR_W - 1))) & (rowi < 2 * NEAR_W)
    future = (rowi == 2 * NEAR_W) & (coli >= near0 + NEAR_W)
    kcx_scr[0:LANES, :] = kc_ref[...]
    kcx_scr[LANES:2 * LANES, :] = jnp.where(shift | future, 1.0, 0.0).astype(MXU_DTYPE)
    s_c = _dot_halves(jnp.concatenate([q, ac_ref[...]], axis=1), kcx_scr[...])
    m_c = jnp.maximum(jnp.max(s_c, axis=1, keepdims=True), 0.1 * NEG_BIG)
    e_c = jnp.exp(s_c - m_c)
    l_c = jnp.sum(e_c, axis=1, keepdims=True)
    p_c = e_c * (1.0 / jnp.maximum(l_c, 1e-30))
    acc_c = _dot_halves(p_c.astype(MXU_DTYPE), vc_ref[...])

    imp = p_c[0:TQ] + p_c[TQ:2 * TQ] + p_c[2 * TQ:3 * TQ] + p_c[3 * TQ:4 * TQ]
    agg = agg_ref[...]
    i_hi, i_mid, i_lo = _split3(imp)
    imps = _dot(i_hi, agg) + _dot(i_mid, agg) + _dot(i_lo, agg)
    _, causal, forced = _nsa_block_masks(tile, nb)
    return acc_c, jnp.where(causal & jnp.logical_not(forced), imps.T, -jnp.inf)


def _nsa_select(tile, cand):
    nb = cand.shape[0]
    blk, causal, forced = _nsa_block_masks(tile, nb)
    neg_inf = -jnp.inf
    blk_f = blk.astype(F32)
    for _ in range(N_SELECT - 3):
        best = jnp.max(cand, axis=0, keepdims=True)
        first = jnp.min(jnp.where(cand == best, blk_f, float(nb)), axis=0, keepdims=True)
        cand = jnp.where(blk_f == first, neg_inf, cand)
    sel = causal & (forced | (cand == neg_inf))
    selneg = jnp.where(sel, 0.0, NEG_BIG).T.astype(MXU_DTYPE)
    return jnp.concatenate([selneg] * NSA_GROUP, axis=0)


def _nsa_body(full, c, q_ref, ks_ref, vs_ref, kw_ref, vw_ref, kc_ref, vc_ref, gl_ref, ac_ref, bm_ref,
              agg_ref, o_ref, kcx_scr, qs_scr, m_scr, acc_scr, sa_scr, sb_scr, part_scr):
    q = q_ref[...].reshape(ROWS, LANES)
    acc_c, cand = _nsa_compressed(c, q, kc_ref, vc_ref, ac_ref, agg_ref, kcx_scr)
    qs_scr[:, 0:LANES] = q
    qs_scr[:, LANES:2 * LANES] = _nsa_select(c, cand)

    def flash_init():
        m_scr[...] = jnp.full(m_scr.shape, NEG_BIG, F32)
        acc_scr[...] = jnp.zeros(acc_scr.shape, F32)

    def scores(k_ref, kt, ntiles=1):
        start = pl.multiple_of(kt * TQ, TQ)
        return _dot(qs_scr[:, 0:k_ref.shape[0]], k_ref[:, pl.ds(start, ntiles * TQ)])

    def flash_update(s, v_ref, kt):
        start = pl.multiple_of(kt * TQ, TQ)
        width = s.shape[1]
        m_prev = m_scr[...]
        m_new = jnp.maximum(m_prev, jnp.max(s, axis=1, keepdims=True))
        p = jnp.exp(s - jnp.concatenate([m_new] * (width // LANES), axis=1))
        acc_scr[...] = jnp.exp(m_prev - m_new) * acc_scr[...] \
            + _dot(p.astype(MXU_DTYPE), v_ref[pl.ds(start, width), :])
        m_scr[...] = m_new

    def flash_step(k_ref, v_ref, kt, bias_tile):
        s = scores(k_ref, kt)
        if bias_tile is not None:
            s = s + bm_ref[:, bias_tile * TQ:(bias_tile + 1) * TQ]
        flash_update(s, v_ref, kt)

    def flash_out():
        acc = acc_scr[...]
        return acc * (1.0 / acc[:, HEAD_DIM:HEAD_DIM + 1])

    if full:
        w0 = pl.multiple_of((c - 2) * TQ, TQ)
        s_w = _dot_halves(q, kw_ref[:, pl.ds(w0, 3 * TQ)]) + bm_ref[...]
        p_w = jnp.exp(s_w - jnp.max(s_w, axis=1, keepdims=True))
        acc_w = _dot_halves(p_w.astype(MXU_DTYPE), vw_ref[pl.ds(w0, 3 * TQ), :])
        o_w = acc_w * (1.0 / acc_w[:, HEAD_DIM:HEAD_DIM + 1])
    else:
        flash_init()

        @pl.when(c >= 1)
        def _():
            flash_step(kw_ref, vw_ref, c - 1, 1)

        flash_step(kw_ref, vw_ref, c, 2)
        o_w = flash_out()

    gates = _sigmoid(gl_ref[...])
    for g in range(NSA_GROUP):
        r = slice(g * TQ, (g + 1) * TQ)
        part_scr[r, :] = gates[:, g * N_BRANCH:g * N_BRANCH + 1] * acc_c[r] \
            + gates[:, g * N_BRANCH + 2:g * N_BRANCH + 3] * o_w[r]

    flash_init()
    if full:
        n_far = c - 1
        n_loop = lax.shift_right_logical(n_far, FAR_UNROLL_LOG2)

        sa_scr[...] = scores(ks_ref, 0, 2)

        def far_pair(t0):
            sb_scr[...] = scores(ks_ref, t0 + 2, 2)
            flash_update(sa_scr[...], vs_ref, t0)
            sa_scr[...] = scores(ks_ref, t0 + 4, 2)
            flash_update(sb_scr[...], vs_ref, t0 + 2)

        def far_block(t0, ntiles):
            for k in range(0, ntiles, 4):
                far_pair(t0 + k)

        def far_body(j, carry):
            far_block(j * (1 << FAR_UNROLL_LOG2), 1 << FAR_UNROLL_LOG2)
            return carry

        lax.fori_loop(0, n_loop, far_body, 0)
        done = n_loop * (1 << FAR_UNROLL_LOG2)
        for bit in range(FAR_UNROLL_LOG2 - 1, 1, -1):
            size = 1 << bit
            take = (n_far & size) != 0

            @pl.when(take)
            def _(done=done, size=size):
                far_block(done, size)

            done = done + jnp.where(take, size, 0)

        @pl.when((n_far & 2) != 0)
        def _(done=done):
            flash_update(sa_scr[...], vs_ref, done)
            sa_scr[...] = scores(ks_ref, done + 2, 2)

        @pl.when((n_far & 1) == 1)
        def _():
            flash_update(sa_scr[:, 0:TQ], vs_ref, n_far - 1)

        flash_update(scores(ks_ref, c - 1, 2) + bm_ref[:, TQ:3 * TQ], vs_ref, c - 1)
    else:
        @pl.when(c >= 1)
        def _():
            flash_step(ks_ref, vs_ref, c - 1, 1)

        flash_step(ks_ref, vs_ref, c, 2)
    o_s = flash_out()

    gates = _sigmoid(gl_ref[...])
    lane = lax.broadcasted_iota(jnp.int32, (TQ, LANES), 1)
    outs = []
    for g in range(NSA_GROUP):
        r = slice(g * TQ, (g + 1) * TQ)
        outs.append(part_scr[r, :] + gates[:, g * N_BRANCH + 1:g * N_BRANCH + 2] * o_s[r])
    for pair in range(NSA_GROUP // 2):
        both = jnp.where(lane < HEAD_DIM, outs[2 * pair], pltpu.roll(outs[2 * pair + 1], HEAD_DIM, 1))
        o_ref[:, pair * LANES:(pair + 1) * LANES] = both.astype(o_ref.dtype)


def _nsa(q8, ks, vs, kw, vw, kc, vc, gl, ac, bm, agg):
    _, b, s, _ = q8.shape
    nc = kc.shape[-1]
    nb = agg.shape[1]
    kvmap = lambda bi, h, c: (bi, h, 0, 0)
    hmap = lambda bi, h, c: (h, 0, 0)
    return pl.pallas_call(
        _nsa_kernel,
        grid=(b, NSA_KV_HEADS, s // TQ),
        in_specs=[
            pl.BlockSpec((NSA_GROUP, None, TQ, LANES), lambda bi, h, c: (h, bi, c, 0)),
            pl.BlockSpec((None, None, 2 * LANES, s), kvmap),
            pl.BlockSpec((None, None, s, LANES), kvmap),
            pl.BlockSpec((None, None, LANES, s), kvmap),
            pl.BlockSpec((None, None, s, LANES), kvmap),
            pl.BlockSpec((None, None, LANES, nc), kvmap),
            pl.BlockSpec((None, None, nc, LANES), kvmap),
            pl.BlockSpec((None, None, TQ, LANES), lambda bi, h, c: (h, bi, c, 0)),
            pl.BlockSpec((None, ROWS, LANES), hmap),
            pl.BlockSpec((None, ROWS, 3 * TQ), hmap),
            _full((nc, nb)),
        ],
        out_specs=pl.BlockSpec((None, TQ, NSA_GROUP * HEAD_DIM), lambda bi, h, c: (bi, c, h)),
        out_shape=jax.ShapeDtypeStruct((b, s, NSA_HEADS * HEAD_DIM), MXU_DTYPE),
        scratch_shapes=[
            pltpu.VMEM((2 * LANES, nc), MXU_DTYPE),
            pltpu.VMEM((ROWS, 2 * LANES), MXU_DTYPE),
            pltpu.VMEM((ROWS, LANES), F32),
            pltpu.VMEM((ROWS, LANES), F32),
            pltpu.VMEM((ROWS, 2 * TQ), F32),
            pltpu.VMEM((ROWS, 2 * TQ), F32),
            pltpu.VMEM((ROWS, LANES), F32),
        ],
        compiler_params=_params("parallel", "parallel", "arbitrary"),
        name="nsa_attention",
    )(q8, ks, vs, kw, vw, kc, vc, gl, ac, bm, agg)


def _conv_kernel(cu_ref, halo_ref, w_ref, b_ref, g_ref, bb_ref, o_ref, hh_scr, sh_scr):
    ts = cu_ref.shape[0]
    ch = CONV_CHANNELS

    def glu(u):
        return u[:, :ch] * _sigmoid(u[:, ch:])

    first = pl.program_id(1) == 0
    hh_scr[0:CONV_HALO, :] = jnp.where(first, 0.0, glu(halo_ref[...]))
    hh_scr[CONV_HALO:CONV_HALO + ts, :] = glu(cu_ref[...])
    hh_scr[CONV_HALO + ts:, :] = jnp.zeros((SUBLANES, ch), F32)
    acc = jnp.broadcast_to(b_ref[...], (ts, ch))
    lead = CONV_HALO - (CONV_WIDTH - 1)
    for shift in range(SUBLANES):
        sh_scr[...] = hh_scr[pl.ds(shift, CONV_HALO + ts), :]
        for start in range(shift, lead + CONV_WIDTH, SUBLANES):
            w = start - lead
            if 0 <= w < CONV_WIDTH:
                acc = acc + sh_scr[pl.ds(start - shift, ts), :] * w_ref[w:w + 1, :]
    mu = jnp.mean(acc, axis=-1, keepdims=True)
    xc = acc - mu
    y = xc * lax.rsqrt(jnp.mean(xc * xc, axis=-1, keepdims=True) + EPS) * g_ref[...] + bb_ref[...]
    o_ref[...] = _silu(y).astype(o_ref.dtype)


def _conv(cu, w, bias, ln_g, ln_b):
    b, s, _ = cu.shape
    ts = min(TS_CONV, s)
    per = ts // CONV_HALO
    return pl.pallas_call(
        _conv_kernel,
        grid=(b, s // ts),
        in_specs=[
            pl.BlockSpec((None, ts, C_CU), lambda bi, i: (bi, i, 0)),
            pl.BlockSpec((None, CONV_HALO, C_CU), lambda bi, i: (bi, jnp.maximum(i * per - 1, 0), 0)),
            _full((CONV_HALO, CONV_CHANNELS)), _full((1, CONV_CHANNELS)),
            _full((1, CONV_CHANNELS)), _full((1, CONV_CHANNELS)),
        ],
        out_specs=pl.BlockSpec((None, ts, CONV_CHANNELS), lambda bi, i: (bi, i, 0)),
        out_shape=jax.ShapeDtypeStruct((b, s, CONV_CHANNELS), MXU_DTYPE),
        scratch_shapes=[pltpu.VMEM((CONV_HALO + ts + SUBLANES, CONV_CHANNELS), F32),
                        pltpu.VMEM((CONV_HALO + ts, CONV_CHANNELS), F32)],
        compiler_params=_params("parallel", "parallel"),
        name="conformer_conv",
    )(cu, cu, w, bias, ln_g, ln_b)


def _gla_kernel(q_ref, k_ref, v_ref, r_ref, ga_ref, wa_ref, ba_ref, g_ref, tri_ref, hm_ref, vm_ref, bmk_ref,
                gm_ref, o_ref, st_scr, sb_scr, o_scr, u_scr):
    ts = q_ref.shape[0]
    nch = ts // GLA_CHUNK
    dk = GLA_KEY_DIM // GLA_HEADS
    cs = GLA_CHUNK

    @pl.when(pl.program_id(1) == 0)
    def _():
        st_scr[...] = jnp.zeros(st_scr.shape, F32)

    x = _dot(ga_ref[...].astype(MXU_DTYPE), wa_ref[...]) + ba_ref[...]
    log_a = (jnp.minimum(x, 0.0) - jnp.log(1.0 + jnp.exp(-jnp.abs(x)))) * (1.0 / GLA_TAU)
    wide = jnp.concatenate([log_a[n * cs:(n + 1) * cs] for n in range(nch)], axis=1)
    l_hi, l_mid, l_lo = _split3(wide)
    tri = tri_ref[...]
    b_wide = _dot(tri, l_hi) + _dot(tri, l_mid) + _dot(tri, l_lo)
    b = jnp.concatenate([b_wide[:, n * GLA_KEY_DIM:(n + 1) * GLA_KEY_DIM] for n in range(nch)], axis=0)
    eb = jnp.exp(b)
    q_t = (q_ref[...] * (dk ** -0.5) * eb).astype(MXU_DTYPE)
    k_all = k_ref[...]
    k_t = (k_all * jnp.exp(-b)).astype(MXU_DTYPE)
    v_all = v_ref[...].astype(MXU_DTYPE)
    causal = tri_ref[...] > 0
    blockmask = bmk_ref[...]

    for n in range(nch):
        r = slice(n * cs, (n + 1) * cs)
        b_last = b[(n + 1) * cs - 1:(n + 1) * cs, :]
        k_d = (k_all[r] * jnp.exp(b_last - b[r])).astype(MXU_DTYPE)
        u_scr[n] = _dot_tn(v_all[r], k_d) * blockmask
    state = st_scr[...]
    for n in range(nch):
        sb_scr[n] = state.astype(MXU_DTYPE)
        b_last = b[(n + 1) * cs - 1:(n + 1) * cs, :]
        state = state * jnp.exp(b_last) + u_scr[n]
    st_scr[...] = state

    causal_h = jnp.concatenate([causal] * GLA_HEADS, axis=0)
    for n in range(nch):
        r = slice(n * cs, (n + 1) * cs)
        qn = q_t[r]
        o = _dot_nt(qn, sb_scr[n])
        q_heads = jnp.concatenate([qn * hm_ref[h:h + 1, :] for h in range(GLA_HEADS)], axis=0)
        attn = jnp.where(causal_h, _dot_nt(q_heads, k_t[r]), 0.0).astype(MXU_DTYPE)
        pv = _dot(attn, v_all[r])
        for h in range(GLA_HEADS):
            o = o + pv[h * cs:(h + 1) * cs] * vm_ref[h:h + 1, :]
        o_scr[r, :] = o
    o = o_scr[...]
    o_hi, o_mid, o_lo = _split3(o * o)
    gm = gm_ref[...]
    ms = _dot(o_hi, gm) + _dot(o_mid, gm) + _dot(o_lo, gm)
    y = o * lax.rsqrt(ms + EPS) * g_ref[...]
    o_ref[...] = (y * _silu(r_ref[...])).astype(o_ref.dtype)


def _gla(gla, misc, wa_pad, ba, g):
    b, s, _ = gla.shape
    ts = min(TS_GLA, s)
    nch = ts // GLA_CHUNK
    dk, dv = GLA_KEY_DIM // GLA_HEADS, GLA_VALUE_DIM // GLA_HEADS
    tri = np.tril(np.ones((GLA_CHUNK, GLA_CHUNK), np.float32))
    hm = (np.arange(GLA_KEY_DIM)[None, :] // dk == np.arange(GLA_HEADS)[:, None]).astype(np.float32)
    vm = (np.arange(GLA_VALUE_DIM)[None, :] // dv == np.arange(GLA_HEADS)[:, None]).astype(np.float32)
    bmk = (np.arange(GLA_VALUE_DIM)[:, None] // dv == np.arange(GLA_KEY_DIM)[None, :] // dk).astype(np.float32)
    gm = (np.arange(GLA_VALUE_DIM)[:, None] // dv == np.arange(GLA_VALUE_DIM)[None, :] // dv).astype(np.float32) / dv
    return pl.pallas_call(
        _gla_kernel,
        grid=(b, s // ts),
        in_specs=[
            pl.BlockSpec((None, ts, GLA_KEY_DIM), lambda bi, i: (bi, i, 0)),
            pl.BlockSpec((None, ts, GLA_KEY_DIM), lambda bi, i: (bi, i, 1)),
            pl.BlockSpec((None, ts, GLA_VALUE_DIM), lambda bi, i: (bi, i, 1)),
            pl.BlockSpec((None, ts, GLA_VALUE_DIM), lambda bi, i: (bi, i, 2)),
            pl.BlockSpec((None, None, ts, LANES), lambda bi, i: (0, bi, i, 0)),
            _full((LANES, GLA_KEY_DIM)), _full((1, GLA_KEY_DIM)), _full((1, GLA_VALUE_DIM)),
            _full(tri.shape), _full(hm.shape), _full(vm.shape), _full(bmk.shape), _full(gm.shape),
        ],
        out_specs=pl.BlockSpec((None, ts, GLA_VALUE_DIM), lambda bi, i: (bi, i, 0)),
        out_shape=jax.ShapeDtypeStruct((b, s, GLA_VALUE_DIM), MXU_DTYPE),
        scratch_shapes=[
            pltpu.VMEM((GLA_VALUE_DIM, GLA_KEY_DIM), F32),
            pltpu.VMEM((nch, GLA_VALUE_DIM, GLA_KEY_DIM), MXU_DTYPE),
            pltpu.VMEM((ts, GLA_VALUE_DIM), F32),
            pltpu.VMEM((nch, GLA_VALUE_DIM, GLA_KEY_DIM), F32),
        ],
        compiler_params=_params("parallel", "arbitrary"),
        name="gla",
    )(gla, gla, gla, gla, misc, wa_pad, ba, g,
      jnp.asarray(tri, MXU_DTYPE), jnp.asarray(hm, MXU_DTYPE), jnp.asarray(vm), jnp.asarray(bmk),
      jnp.asarray(gm, MXU_DTYPE))


def _post_kernel(x_ref, yn_ref, yc_ref, yg_ref, p_ref, wo_ref, gf_ref, wg_ref, wu_ref, wd_ref,
                 gp_ref, wpg_ref, wpp_ref, gfin_ref, o_ref, *, final):
    dn = yn_ref.shape[1]
    dc = yc_ref.shape[1]
    mix = _dot(yn_ref[...], wo_ref[0:dn, :]) + _dot(yc_ref[...], wo_ref[dn:dn + dc, :]) \
        + _dot(yg_ref[...], wo_ref[dn + dc:, :])
    x = x_ref[...] + mix
    h = _rms(x, gf_ref[...]).astype(MXU_DTYPE)
    dff = wg_ref.shape[1]
    step = dff // FF_CHUNKS
    ff = None
    for j in range(FF_CHUNKS):
        cols = slice(j * step, (j + 1) * step)
        act = _silu(_dot(h, wg_ref[:, cols])) * _dot(h, wu_ref[:, cols])
        down = _dot(act.astype(MXU_DTYPE), wd_ref[cols, :])
        ff = down if ff is None else ff + down
    x = x + ff
    gate = _sigmoid(_dot(_rms(x, gp_ref[...]).astype(MXU_DTYPE), wpg_ref[...]))
    x = x + _dot(p_ref[...].astype(MXU_DTYPE), wpp_ref[...]) * gate
    if final:
        x = _rms(x, gfin_ref[...])
    o_ref[...] = x


def _post(x2d, yn, yc, yg, p_all, layer, wo, gf, wg, wu, wd, gp, wpg, wpp, gfin, final):
    t, d = x2d.shape
    tm = min(TM_PROJ, t)
    row = lambda i: (i, 0)
    once = dict(pipeline_mode=pl.Buffered(1))

    def const(a):
        return pl.BlockSpec(a.shape, lambda i: (0,) * a.ndim, **once)

    return pl.pallas_call(
        functools.partial(_post_kernel, final=final),
        grid=(t // tm,),
        in_specs=[
            pl.BlockSpec((tm, d), row),
            pl.BlockSpec((tm, yn.shape[1]), row), pl.BlockSpec((tm, yc.shape[1]), row),
            pl.BlockSpec((tm, yg.shape[1]), row),
            pl.BlockSpec((None, tm, p_all.shape[2]), lambda i: (layer, i, 0)),
            const(wo), const(gf), const(wg), const(wu), const(wd), const(gp), const(wpg), const(wpp),
            const(gfin),
        ],
        out_specs=pl.BlockSpec((tm, d), row),
        out_shape=jax.ShapeDtypeStruct((t, d), F32),
        compiler_params=_params("parallel"),
        name="post_mixer",
    )(x2d, yn, yc, yg, p_all, wo, gf, wg, wu, wd, gp, wpg, wpp, gfin)


def _layer(x2d, p_all, layer, bsz, seq, rel_tabs, agg, lw, final_g, final):
    (mix_g, w_in, w_out, pos_kv, w1_kv, b1_kv, w2_kv, conv_w, conv_b, conv_ln_g, conv_ln_b,
     w_alpha, b_alpha, gla_g, ffn_g, w_gate, w_up, w_down, ple_g, w_pg, w_pp, qconst) = lw
    bm, ac = rel_tabs
    t = bsz * seq
    hk = NSA_KV_HEADS

    w_all = _gather_columns(w_in, _W_RUNS).astype(MXU_DTYPE)
    wk_t = _gather_columns(w_in, _WK_RUNS).T.astype(MXU_DTYPE)
    q8, ks, kw, vs, vw, kvc, cu, gla, misc = _in_proj(x2d, mix_g[None, :], w_all, wk_t, qconst, bsz, seq)

    nsub = seq // CMP_STRIDE
    cmp = _compress(kvc.reshape(bsz, seq, C_KVC), pos_kv, w1_kv, b1_kv, w2_kv)
    cmp = cmp.reshape(2, bsz, nsub, hk, HEAD_DIM)
    ones_row = jnp.ones((bsz, hk, 2, nsub), MXU_DTYPE)
    kc = jnp.concatenate([cmp[0].transpose(0, 2, 3, 1), ones_row,
                          jnp.zeros((bsz, hk, LANES - HEAD_DIM - 2, nsub), MXU_DTYPE)], axis=2)
    ones_col = jnp.ones((bsz, hk, nsub, 1), MXU_DTYPE)
    vc = jnp.concatenate([cmp[1].transpose(0, 2, 1, 3), ones_col,
                          jnp.zeros((bsz, hk, nsub, LANES - HEAD_DIM - 1), MXU_DTYPE)], axis=3)

    y_nsa = _nsa(q8.reshape(NSA_HEADS, bsz, seq, LANES), ks, vs, kw, vw, kc, vc,
                 misc.reshape(hk, bsz, seq, LANES), ac, bm, agg)

    y_conv = _conv(cu.reshape(bsz, seq, C_CU), conv_w, conv_b[None, :], conv_ln_g[None, :], conv_ln_b[None, :])
    y_gla = _gla(gla.reshape(bsz, seq, C_GLA), misc.reshape(hk, bsz, seq, LANES), w_alpha, b_alpha[None, :],
                 gla_g[None, :])

    return _post(x2d, y_nsa.reshape(t, -1), y_conv.reshape(t, -1), y_gla.reshape(t, -1), p_all, layer,
                 w_out.astype(MXU_DTYPE), ffn_g[None, :], w_gate.astype(MXU_DTYPE), w_up.astype(MXU_DTYPE),
                 w_down.astype(MXU_DTYPE), ple_g[None, :], w_pg.astype(MXU_DTYPE), w_pp.astype(MXU_DTYPE),
                 final_g[None, :], final)


def kernel(x, p, rel_bias, mix_norm_g, w_in, w_out, cmp_pos_k, cmp_w1_k, cmp_b1_k, cmp_w2_k, cmp_pos_v, cmp_w1_v, cmp_b1_v, cmp_w2_v, conv_w, conv_b, conv_ln_g, conv_ln_b, gla_w_alpha, gla_b_alpha, gla_norm_g, ffn_norm_g, ffn_w_gate, ffn_w_up, ffn_w_down, ple_norm_g, ple_w_gate, ple_w_proj, final_norm_g):
    bsz, seq, d = x.shape
    depth = w_in.shape[0]
    assert seq % TQ == 0 and seq // SLC_BLOCK <= LANES
    t = bsz * seq
    rel_tabs = _nsa_bias_tables(rel_bias.astype(F32))
    nbp = LANES
    agg = jnp.asarray(_agg_matrix(seq // CMP_STRIDE, nbp), MXU_DTYPE)

    far = rel_bias[REL_BUCKETS - 1].astype(F32)
    f_hi, f_lo = _split_hi_lo(far)
    qconst = jnp.zeros((NSA_HEADS + 1, LANES), F32)
    qconst = qconst.at[:NSA_HEADS, FAR_LANE].set(f_hi.astype(F32)).at[:NSA_HEADS, FAR_LANE + 1].set(f_lo.astype(F32))
    qconst = qconst.at[NSA_HEADS, HEAD_DIM].set(1.0)

    x2d = x.reshape(t, d)
    for i in range(depth):
        wa_pad = jnp.zeros((LANES, GLA_KEY_DIM), F32).at[GA_LANE0:GA_LANE0 + GLA_GATE_RANK].set(gla_w_alpha[i])
        conv_w_pad = jnp.zeros((CONV_HALO, CONV_CHANNELS), F32).at[:CONV_WIDTH].set(conv_w[i])
        both = lambda a: jnp.concatenate([a] * NSA_KV_HEADS, axis=-1)
        w1_blocks = lambda w: _block_diag2(w.reshape(CMP_BLOCK, HEAD_DIM, CMP_HIDDEN))
        lw = (mix_norm_g[i], w_in[i], w_out[i],
              jnp.stack([both(cmp_pos_k[i]), both(cmp_pos_v[i])]),
              jnp.stack([w1_blocks(cmp_w1_k[i]), w1_blocks(cmp_w1_v[i])]).astype(MXU_DTYPE),
              jnp.stack([both(cmp_b1_k[i])[None, :], both(cmp_b1_v[i])[None, :]]),
              jnp.stack([_block_diag2(cmp_w2_k[i]), _block_diag2(cmp_w2_v[i])]).astype(MXU_DTYPE),
              conv_w_pad, conv_b[i], conv_ln_g[i], conv_ln_b[i],
              wa_pad.astype(MXU_DTYPE), gla_b_alpha[i], gla_norm_g[i],
              ffn_norm_g[i], ffn_w_gate[i], ffn_w_up[i], ffn_w_down[i],
              ple_norm_g[i], ple_w_gate[i], ple_w_proj[i], qconst)
        x2d = _layer(x2d, p.reshape(depth, t, -1), i, bsz, seq, rel_tabs, agg, lw, final_norm_g, i == depth - 1)
    return x2d.reshape(bsz, seq, d)
```

```python
import functools
import math

import numpy as np
import jax
import jax.numpy as jnp
from jax import lax
from jax.experimental import pallas as pl
from jax.experimental.pallas import tpu as pltpu

F32 = jnp.float32
MXU_DTYPE = jnp.bfloat16

HEAD_DIM = 64
NSA_HEADS = 8
NSA_KV_HEADS = 2
NSA_GROUP = NSA_HEADS // NSA_KV_HEADS
CMP_BLOCK = 32
CMP_STRIDE = 16
CMP_HIDDEN = 256
SLC_BLOCK = 64
N_SELECT = 16
WINDOW = 512
N_BRANCH = 3
CONV_CHANNELS = 256
CONV_WIDTH = 31
GLA_HEADS = 4
GLA_KEY_DIM = 128
GLA_VALUE_DIM = 256
GLA_GATE_RANK = 16
GLA_TAU = 16.0
GLA_CHUNK = 64
REL_BUCKETS = 32
REL_MAX_DIST = 128
PLE_DIM = 256
EPS = 1e-6
NEG_BIG = -1e30

LANES = 128
SUBLANES = 8
VMEM_LIMIT_BYTES = 56 * 1024 * 1024

TM_PROJ = 512
TQ = 256
ROWS = NSA_GROUP * TQ
NEAR_W = 2 * TQ // CMP_STRIDE
FAR_UNROLL_LOG2 = 3
assert WINDOW == 2 * TQ and REL_MAX_DIST <= TQ and TQ % SLC_BLOCK == 0 and TQ % CMP_STRIDE == 0
TS_CONV = 512
CONV_HALO = 32
TS_GLA = 1024
FF_CHUNKS = 1

C_Q = NSA_HEADS * HEAD_DIM
C_V = 2 * NSA_KV_HEADS * HEAD_DIM
C_KT = 2 * NSA_KV_HEADS * HEAD_DIM
C_KVC = 2 * NSA_KV_HEADS * HEAD_DIM
C_CU = 2 * CONV_CHANNELS
C_GLA = 2 * GLA_KEY_DIM + 2 * GLA_VALUE_DIM
C_MISC = NSA_KV_HEADS * LANES
GA_LANE0 = 16
FAR_LANE = HEAD_DIM


def _split_hi_lo(x):
    hi = x.astype(MXU_DTYPE)
    lo = (x - hi.astype(F32)).astype(MXU_DTYPE)
    return hi, lo


def _split3(x):
    hi = x.astype(MXU_DTYPE)
    r = x - hi.astype(F32)
    mid = r.astype(MXU_DTYPE)
    lo = (r - mid.astype(F32)).astype(MXU_DTYPE)
    return hi, mid, lo


def _dot(a, b):
    return jnp.dot(a, b, preferred_element_type=F32)


def _dot_halves(a, b):
    half = a.shape[0] // 2
    return jnp.concatenate([_dot(a[:half], b), _dot(a[half:], b)], axis=0)


def _dot_nt(a, b):
    return lax.dot_general(a, b, (((1,), (1,)), ((), ())), preferred_element_type=F32)


def _dot_tn(a, b):
    return lax.dot_general(a, b, (((0,), (0,)), ((), ())), preferred_element_type=F32)


def _rms(x, g):
    return x * lax.rsqrt(jnp.mean(x * x, axis=-1, keepdims=True) + EPS) * g


def _sigmoid(x):
    return 1.0 / (1.0 + jnp.exp(-x))


def _silu(x):
    return x * _sigmoid(x)


def _params(*sem):
    return pltpu.CompilerParams(dimension_semantics=sem, vmem_limit_bytes=VMEM_LIMIT_BYTES)


def _full(shape):
    nd = len(shape)
    return pl.BlockSpec(shape, lambda *_: (0,) * nd)


def _t5_bucket_np(dist):
    n = np.maximum(dist, 0)
    max_exact = REL_BUCKETS // 2
    nf = np.maximum(n, 1).astype(np.float32)
    large = max_exact + (np.log(nf / np.float32(max_exact)) / np.float32(math.log(REL_MAX_DIST / max_exact))
                         * np.float32(REL_BUCKETS - max_exact)).astype(np.int32)
    large = np.minimum(large, REL_BUCKETS - 1)
    return np.where(n < max_exact, n, large).astype(np.int32)


def _in_proj_columns(in_splits):
    offs = np.concatenate([[0], np.cumsum(in_splits)])
    (o_nq, o_kc, o_vc, o_ks, o_vs, o_kw, o_vw, o_ng, o_cu, o_gq, o_gk, o_gv, o_ga, o_gr) = offs[:-1]
    src, scale = [], []

    def put(cols, s=1.0):
        src.extend(cols)
        scale.extend([s] * len(cols))

    put(list(range(o_nq, o_nq + C_Q)), HEAD_DIM ** -0.5)
    kvw = NSA_KV_HEADS * HEAD_DIM
    for o in (o_vs, o_vw, o_kc, o_vc):
        put(list(range(o, o + kvw)))
    put(list(range(o_cu, o_cu + C_CU)))
    put(list(range(o_gq, o_gq + GLA_KEY_DIM)))
    put(list(range(o_gk, o_gk + GLA_KEY_DIM)))
    put(list(range(o_gv, o_gv + GLA_VALUE_DIM)))
    put(list(range(o_gr, o_gr + GLA_VALUE_DIM)))
    per = NSA_GROUP * N_BRANCH
    for h in range(NSA_KV_HEADS):
        slab = [-1] * LANES
        slab[:per] = list(range(o_ng + h * per, o_ng + (h + 1) * per))
        if h == 0:
            slab[GA_LANE0:GA_LANE0 + GLA_GATE_RANK] = list(range(o_ga, o_ga + GLA_GATE_RANK))
        put(slab)
    key_src = list(range(o_ks, o_ks + kvw)) + list(range(o_kw, o_kw + kvw))
    return np.asarray(src, np.int32), np.asarray(scale, np.float32), np.asarray(key_src, np.int32)


def _column_runs(src, scale):
    runs, i = [], 0
    while i < len(src):
        j = i + 1
        while j < len(src) and scale[j] == scale[i] and (
                (src[i] < 0 and src[j] < 0) or (src[i] >= 0 and src[j] == src[i] + (j - i))):
            j += 1
        runs.append((int(src[i]), j - i, float(scale[i])))
        i = j
    return runs


IN_SPLITS = (
    NSA_HEADS * HEAD_DIM,
    NSA_KV_HEADS * HEAD_DIM, NSA_KV_HEADS * HEAD_DIM,
    NSA_KV_HEADS * HEAD_DIM, NSA_KV_HEADS * HEAD_DIM,
    NSA_KV_HEADS * HEAD_DIM, NSA_KV_HEADS * HEAD_DIM,
    NSA_HEADS * N_BRANCH,
    2 * CONV_CHANNELS,
    GLA_KEY_DIM, GLA_KEY_DIM, GLA_VALUE_DIM,
    GLA_GATE_RANK,
    GLA_VALUE_DIM,
)
_W_SRC, _W_SCALE, _WK_SRC = _in_proj_columns(IN_SPLITS)
_W_RUNS = _column_runs(_W_SRC, _W_SCALE)
_WK_RUNS = _column_runs(_WK_SRC, np.ones_like(_WK_SRC, np.float32))
C_ALL = C_Q + C_V + C_KVC + C_CU + C_GLA + C_MISC
assert _W_SRC.shape[0] == C_ALL and _WK_SRC.shape[0] == C_KT


def _gather_columns(w, runs):
    parts = []
    for start, width, scale in runs:
        if start < 0:
            parts.append(jnp.zeros((w.shape[0], width), w.dtype))
        else:
            part = w[:, start:start + width]
            parts.append(part if scale == 1.0 else part * scale)
    return jnp.concatenate(parts, axis=1)


def _nsa_bias_tables(rel_bias):
    i = np.arange(TQ)[:, None]
    j = np.arange(TQ)[None, :]
    far = rel_bias[REL_BUCKETS - 1]
    tab = rel_bias - far[None, :]

    d = np.arange(2 * TQ - 1, -2 * TQ - 1, -1)
    by_dist = jnp.where(jnp.asarray(d >= 0)[:, None], jnp.take(tab, jnp.asarray(_t5_bucket_np(d)), axis=0), NEG_BIG).T

    n = by_dist.shape[1]
    skew = jnp.tile(by_dist, (1, TQ))[:, :TQ * (n - 1)].reshape(NSA_HEADS, TQ, n - 1)

    def toeplitz(dist0, step, width):
        first = 2 * TQ - 1 - dist0
        assert first >= TQ - 1 and first + step * (width - 1) < n - 1
        vals = lax.slice_in_dim(skew, first, first + step * (width - 1) + 1, stride=step, axis=2)
        return vals.reshape(NSA_KV_HEADS, ROWS, width)

    tri = jnp.asarray(np.where(j <= i, NEG_BIG, 0.0).astype(np.float32))
    tri = jnp.broadcast_to(jnp.tile(tri, (NSA_GROUP, 1))[None], (NSA_KV_HEADS, ROWS, TQ))
    bm = jnp.concatenate([tri, toeplitz(TQ, 1, 2 * TQ)], axis=-1)

    near = toeplitz((NEAR_W // 2) * CMP_STRIDE - (CMP_BLOCK - 1), CMP_STRIDE, NEAR_W)
    hi, lo = _split_hi_lo(near)
    pad = jnp.zeros((NSA_KV_HEADS, ROWS, LANES - 2 * NEAR_W - 1), MXU_DTYPE)
    big = jnp.full((NSA_KV_HEADS, ROWS, 1), NEG_BIG, MXU_DTYPE)
    ac = jnp.concatenate([hi, lo, big, pad], axis=-1)
    return bm, ac


def _agg_matrix(ncp, nbp):
    rs, rc = SLC_BLOCK // CMP_STRIDE, CMP_BLOCK // CMP_STRIDE
    agg = np.zeros((ncp, nbp), np.float32)
    ncmp = ncp - rc + 1
    for jb in range(nbp):
        for mm in range(rs):
            for nn in range(rc):
                idx = jb * rs + mm - nn
                if 0 <= idx < ncmp:
                    agg[idx, jb] += 1.0
    return agg


def _in_proj_kernel(x_ref, g_ref, w_ref, wk_ref, qc_ref, q_ref, ks_ref, kw_ref, vs_ref, vw_ref,
                    kvc_ref, cu_ref, gla_ref, misc_ref, *, tiles_per_seq):
    h = _rms(x_ref[...], g_ref[...]).astype(MXU_DTYPE)
    tm = h.shape[0]
    low = lax.broadcasted_iota(jnp.int32, (tm, LANES), 1) < HEAD_DIM

    def head_group(z, idx, const):
        g = z[:, (idx // 2) * LANES:(idx // 2 + 1) * LANES]
        if idx % 2:
            g = pltpu.roll(g, HEAD_DIM, 1)
        return jnp.where(low, g, const)

    o = 0
    zq = _dot(h, w_ref[:, o:o + C_Q])
    for hd in range(NSA_HEADS):
        q_ref[hd] = head_group(zq, hd, qc_ref[hd:hd + 1, :]).astype(q_ref.dtype)
    o += C_Q
    zv = _dot(h, w_ref[:, o:o + C_V])
    ones_lane = qc_ref[NSA_HEADS:NSA_HEADS + 1, :]
    for hh in range(NSA_KV_HEADS):
        vs_ref[hh] = head_group(zv, hh, ones_lane).astype(vs_ref.dtype)
        vw_ref[hh] = head_group(zv, NSA_KV_HEADS + hh, ones_lane).astype(vw_ref.dtype)
    o += C_V
    kt = _dot_nt(wk_ref[...], h)
    rows = lax.broadcasted_iota(jnp.int32, (LANES - HEAD_DIM, tm), 0)
    ones_rows = jnp.where(rows < 2, 1.0, 0.0).astype(ks_ref.dtype)
    tok = (pl.program_id(0) % tiles_per_seq) * tm + lax.broadcasted_iota(jnp.int32, (LANES, tm), 1)
    blk = lax.broadcasted_iota(jnp.int32, (LANES, tm), 0)
    blk_rows = jnp.where(lax.shift_right_logical(tok, int(math.log2(SLC_BLOCK))) == blk, 1.0, 0.0).astype(ks_ref.dtype)
    for hh in range(NSA_KV_HEADS):
        ks_ref[hh, 0:HEAD_DIM, :] = kt[hh * HEAD_DIM:(hh + 1) * HEAD_DIM].astype(ks_ref.dtype)
        ks_ref[hh, HEAD_DIM:LANES, :] = ones_rows
        ks_ref[hh, LANES:2 * LANES, :] = blk_rows
        kw_ref[hh, 0:HEAD_DIM, :] = kt[(NSA_KV_HEADS + hh) * HEAD_DIM:(NSA_KV_HEADS + hh + 1) * HEAD_DIM].astype(kw_ref.dtype)
        kw_ref[hh, HEAD_DIM:LANES, :] = ones_rows
    kvc_ref[...] = _dot(h, w_ref[:, o:o + C_KVC])
    o += C_KVC
    cu_ref[...] = _dot(h, w_ref[:, o:o + C_CU])
    o += C_CU
    gla_ref[...] = _dot(h, w_ref[:, o:o + C_GLA])
    o += C_GLA
    zm = _dot(h, w_ref[:, o:o + C_MISC])
    for s in range(NSA_KV_HEADS):
        misc_ref[s] = zm[:, s * LANES:(s + 1) * LANES]


def _in_proj(x2d, g, w_all, wk_t, qconst, bsz, seq):
    t, d = x2d.shape
    tm = min(TM_PROJ, seq)
    tps = seq // tm
    hk = NSA_KV_HEADS
    row = lambda i: (i, 0)
    kmap = lambda i: (i // tps, 0, 0, i % tps)
    vmap = lambda i: (i // tps, 0, i % tps, 0)
    return pl.pallas_call(
        functools.partial(_in_proj_kernel, tiles_per_seq=tps),
        grid=(t // tm,),
        in_specs=[pl.BlockSpec((tm, d), row), _full((1, d)), _full((d, C_ALL)), _full((C_KT, d)),
                  _full((NSA_HEADS + 1, LANES))],
        out_specs=[
            pl.BlockSpec((NSA_HEADS, tm, LANES), lambda i: (0, i, 0)),
            pl.BlockSpec((None, hk, 2 * LANES, tm), kmap),
            pl.BlockSpec((None, hk, LANES, tm), kmap),
            pl.BlockSpec((None, hk, tm, LANES), vmap),
            pl.BlockSpec((None, hk, tm, LANES), vmap),
            pl.BlockSpec((tm, C_KVC), row),
            pl.BlockSpec((tm, C_CU), row),
            pl.BlockSpec((tm, C_GLA), row),
            pl.BlockSpec((NSA_KV_HEADS, tm, LANES), lambda i: (0, i, 0)),
        ],
        out_shape=[
            jax.ShapeDtypeStruct((NSA_HEADS, t, LANES), MXU_DTYPE),
            jax.ShapeDtypeStruct((bsz, hk, 2 * LANES, seq), MXU_DTYPE),
            jax.ShapeDtypeStruct((bsz, hk, LANES, seq), MXU_DTYPE),
            jax.ShapeDtypeStruct((bsz, hk, seq, LANES), MXU_DTYPE),
            jax.ShapeDtypeStruct((bsz, hk, seq, LANES), MXU_DTYPE),
            jax.ShapeDtypeStruct((t, C_KVC), F32),
            jax.ShapeDtypeStruct((t, C_CU), F32),
            jax.ShapeDtypeStruct((t, C_GLA), F32),
            jax.ShapeDtypeStruct((NSA_KV_HEADS, t, LANES), F32),
        ],
        compiler_params=_params("parallel"),
        name="in_proj",
    )(x2d, g, w_all, wk_t, qconst)


def _compress_kernel(x_ref, pos_ref, w1_ref, b1_ref, w2_ref, o_ref):
    nc = o_ref.shape[0]
    top = bot = None
    for l in range(CMP_STRIDE):
        x = x_ref[pl.ds(l, nc, stride=CMP_STRIDE), :]
        t = _dot((x + pos_ref[l:l + 1, :]).astype(MXU_DTYPE), w1_ref[l])
        u = _dot((x + pos_ref[CMP_STRIDE + l:CMP_STRIDE + l + 1, :]).astype(MXU_DTYPE), w1_ref[CMP_STRIDE + l])
        top = t if top is None else top + t
        bot = u if bot is None else bot + u
    hid = _silu(top + pltpu.roll(bot, nc - 1, 0) + b1_ref[...])
    o_ref[...] = _dot(hid.astype(MXU_DTYPE), w2_ref[...]).astype(o_ref.dtype)


def _compress(kvc, pos, w1, b1, w2):
    b, s, _ = kvc.shape
    nc = s // CMP_STRIDE
    hw = NSA_KV_HEADS * HEAD_DIM
    hid = NSA_KV_HEADS * CMP_HIDDEN
    return pl.pallas_call(
        _compress_kernel,
        grid=(2, b),
        in_specs=[
            pl.BlockSpec((None, s, hw), lambda kv, bi: (bi, 0, kv)),
            pl.BlockSpec((None, CMP_BLOCK, hw), lambda kv, bi: (kv, 0, 0)),
            pl.BlockSpec((None, CMP_BLOCK, hw, hid), lambda kv, bi: (kv, 0, 0, 0)),
            pl.BlockSpec((None, 1, hid), lambda kv, bi: (kv, 0, 0)),
            pl.BlockSpec((None, hid, hw), lambda kv, bi: (kv, 0, 0)),
        ],
        out_specs=pl.BlockSpec((None, None, nc, hw), lambda kv, bi: (kv, bi, 0, 0)),
        out_shape=jax.ShapeDtypeStruct((2, b, nc, hw), MXU_DTYPE),
        compiler_params=_params("parallel", "parallel"),
        name="compress_kv",
    )(kvc, pos, w1, b1, w2)


def _block_diag2(w):
    z = jnp.zeros_like(w)
    return jnp.concatenate([jnp.concatenate([w, z], axis=-1), jnp.concatenate([z, w], axis=-1)], axis=-2)


def _nsa_kernel(*refs):
    c = pl.program_id(2)

    @pl.when(c >= 2)
    def _():
        _nsa_body(True, c, *refs)

    @pl.when(c < 2)
    def _():
        _nsa_body(False, c, *refs)


def _nsa_block_masks(tile, nb):
    t_q = tile * TQ + lax.broadcasted_iota(jnp.int32, (nb, TQ), 1)
    blk = lax.broadcasted_iota(jnp.int32, (nb, TQ), 0)
    cur = lax.shift_right_logical(t_q, int(math.log2(SLC_BLOCK)))
    return blk, blk <= cur, (blk == 0) | (blk == cur) | (blk == cur - 1)


def _nsa_compressed(tile, q, kc_ref, vc_ref, ac_ref, agg_ref, kcx_scr):
    nc = kc_ref.shape[1]
    nb = agg_ref.shape[1]
    rowi = lax.broadcasted_iota(jnp.int32, (LANES, nc), 0)
    coli = lax.broadcasted_iota(jnp.int32, (LANES, nc), 1)
    near0 = tile * (TQ // CMP_STRIDE) - NEAR_W // 2
    shift = (coli == near0 + (rowi & (NEAR_W - 1))) & (rowi < 2 * NEAR_W)
    future = (rowi == 2 * NEAR_W) & (coli >= near0 + NEAR_W)
    kcx_scr[0:LANES, :] = kc_ref[...]
    kcx_scr[LANES:2 * LANES, :] = jnp.where(shift | future, 1.0, 0.0).astype(MXU_DTYPE)
    s_c = _dot_halves(jnp.concatenate([q, ac_ref[...]], axis=1), kcx_scr[...])
    m_c = jnp.maximum(jnp.max(s_c, axis=1, keepdims=True), 0.1 * NEG_BIG)
    e_c = jnp.exp(s_c - m_c)
    l_c = jnp.sum(e_c, axis=1, keepdims=True)
    p_c = e_c * (1.0 / jnp.maximum(l_c, 1e-30))
    acc_c = _dot_halves(p_c.astype(MXU_DTYPE), vc_ref[...])

    imp = p_c[0:TQ] + p_c[TQ:2 * TQ] + p_c[2 * TQ:3 * TQ] + p_c[3 * TQ:4 * TQ]
    agg = agg_ref[...]
    i_hi, i_mid, i_lo = _split3(imp)
    imps = _dot(i_hi, agg) + _dot(i_mid, agg) + _dot(i_lo, agg)
    _, causal, forced = _nsa_block_masks(tile, nb)
    return acc_c, jnp.where(causal & jnp.logical_not(forced), imps.T, -jnp.inf)


def _nsa_select(tile, cand):
    nb = cand.shape[0]
    blk, causal, forced = _nsa_block_masks(tile, nb)
    neg_inf = -jnp.inf
    blk_f = blk.astype(F32)
    for _ in range(N_SELECT - 3):
        best = jnp.max(cand, axis=0, keepdims=True)
        first = jnp.min(jnp.where(cand == best, blk_f, float(nb)), axis=0, keepdims=True)
        cand = jnp.where(blk_f == first, neg_inf, cand)
    sel = causal & (forced | (cand == neg_inf))
    selneg = jnp.where(sel, 0.0, NEG_BIG).T.astype(MXU_DTYPE)
    return jnp.concatenate([selneg] * NSA_GROUP, axis=0)


def _nsa_body(full, c, q_ref, ks_ref, vs_ref, kw_ref, vw_ref, kc_ref, vc_ref, gl_ref, ac_ref, bm_ref,
              agg_ref, o_ref, kcx_scr, qs_scr, m_scr, acc_scr, sa_scr, sb_scr, part_scr):
    q = q_ref[...].reshape(ROWS, LANES)
    acc_c, cand = _nsa_compressed(c, q, kc_ref, vc_ref, ac_ref, agg_ref, kcx_scr)
    qs_scr[:, 0:LANES] = q
    qs_scr[:, LANES:2 * LANES] = _nsa_select(c, cand)

    def flash_init():
        m_scr[...] = jnp.full(m_scr.shape, NEG_BIG, F32)
        acc_scr[...] = jnp.zeros(acc_scr.shape, F32)

    def scores(k_ref, kt, ntiles=1):
        start = pl.multiple_of(kt * TQ, TQ)
        return _dot(qs_scr[:, 0:k_ref.shape[0]], k_ref[:, pl.ds(start, ntiles * TQ)])

    def flash_update(s, v_ref, kt):
        start = pl.multiple_of(kt * TQ, TQ)
        width = s.shape[1]
        m_prev = m_scr[...]
        m_new = jnp.maximum(m_prev, jnp.max(s, axis=1, keepdims=True))
        p = jnp.exp(s - jnp.concatenate([m_new] * (width // LANES), axis=1))
        acc_scr[...] = jnp.exp(m_prev - m_new) * acc_scr[...] \
            + _dot(p.astype(MXU_DTYPE), v_ref[pl.ds(start, width), :])
        m_scr[...] = m_new

    def flash_step(k_ref, v_ref, kt, bias_tile):
        s = scores(k_ref, kt)
        if bias_tile is not None:
            s = s + bm_ref[:, bias_tile * TQ:(bias_tile + 1) * TQ]
        flash_update(s, v_ref, kt)

    def flash_out():
        acc = acc_scr[...]
        return acc * (1.0 / acc[:, HEAD_DIM:HEAD_DIM + 1])

    if full:
        w0 = pl.multiple_of((c - 2) * TQ, TQ)
        s_w = _dot_halves(q, kw_ref[:, pl.ds(w0, 3 * TQ)]) + bm_ref[...]
        p_w = jnp.exp(s_w - jnp.max(s_w, axis=1, keepdims=True))
        acc_w = _dot_halves(p_w.astype(MXU_DTYPE), vw_ref[pl.ds(w0, 3 * TQ), :])
        o_w = acc_w * (1.0 / acc_w[:, HEAD_DIM:HEAD_DIM + 1])
    else:
        flash_init()

        @pl.when(c >= 1)
        def _():
            flash_step(kw_ref, vw_ref, c - 1, 1)

        flash_step(kw_ref, vw_ref, c, 2)
        o_w = flash_out()

    gates = _sigmoid(gl_ref[...])
    for g in range(NSA_GROUP):
        r = slice(g * TQ, (g + 1) * TQ)
        part_scr[r, :] = gates[:, g * N_BRANCH:g * N_BRANCH + 1] * acc_c[r] \
            + gates[:, g * N_BRANCH + 2:g * N_BRANCH + 3] * o_w[r]

    flash_init()
    if full:
        n_far = c - 1
        n_loop = lax.shift_right_logical(n_far, FAR_UNROLL_LOG2)

        sa_scr[...] = scores(ks_ref, 0, 2)

        def far_pair(t0):
            sb_scr[...] = scores(ks_ref, t0 + 2, 2)
            flash_update(sa_scr[...], vs_ref, t0)
            sa_scr[...] = scores(ks_ref, t0 + 4, 2)
            flash_update(sb_scr[...], vs_ref, t0 + 2)

        def far_block(t0, ntiles):
            for k in range(0, ntiles, 4):
                far_pair(t0 + k)

        def far_body(j, carry):
            far_block(j * (1 << FAR_UNROLL_LOG2), 1 << FAR_UNROLL_LOG2)
            return carry

        lax.fori_loop(0, n_loop, far_body, 0)
        done = n_loop * (1 << FAR_UNROLL_LOG2)
        for bit in range(FAR_UNROLL_LOG2 - 1, 1, -1):
            size = 1 << bit
            take = (n_far & size) != 0

            @pl.when(take)
            def _(done=done, size=size):
                far_block(done, size)

            done = done + jnp.where(take, size, 0)

        @pl.when((n_far & 2) != 0)
        def _(done=done):
            flash_update(sa_scr[...], vs_ref, done)
            sa_scr[...] = scores(ks_ref, done + 2, 2)

        @pl.when((n_far & 1) == 1)
        def _():
            flash_update(sa_scr[:, 0:TQ], vs_ref, n_far - 1)

        flash_update(scores(ks_ref, c - 1, 2) + bm_ref[:, TQ:3 * TQ], vs_ref, c - 1)
    else:
        @pl.when(c >= 1)
        def _():
            flash_step(ks_ref, vs_ref, c - 1, 1)

        flash_step(ks_ref, vs_ref, c, 2)
    o_s = flash_out()

    gates = _sigmoid(gl_ref[...])
    lane = lax.broadcasted_iota(jnp.int32, (TQ, LANES), 1)
    outs = []
    for g in range(NSA_GROUP):
        r = slice(g * TQ, (g + 1) * TQ)
        outs.append(part_scr[r, :] + gates[:, g * N_BRANCH + 1:g * N_BRANCH + 2] * o_s[r])
    for pair in range(NSA_GROUP // 2):
        both = jnp.where(lane < HEAD_DIM, outs[2 * pair], pltpu.roll(outs[2 * pair + 1], HEAD_DIM, 1))
        o_ref[:, pair * LANES:(pair + 1) * LANES] = both.astype(o_ref.dtype)


def _nsa(q8, ks, vs, kw, vw, kc, vc, gl, ac, bm, agg):
    _, b, s, _ = q8.shape
    nc = kc.shape[-1]
    nb = agg.shape[1]
    kvmap = lambda bi, h, c: (bi, h, 0, 0)
    hmap = lambda bi, h, c: (h, 0, 0)
    return pl.pallas_call(
        _nsa_kernel,
        grid=(b, NSA_KV_HEADS, s // TQ),
        in_specs=[
            pl.BlockSpec((NSA_GROUP, None, TQ, LANES), lambda bi, h, c: (h, bi, c, 0)),
            pl.BlockSpec((None, None, 2 * LANES, s), kvmap),
            pl.BlockSpec((None, None, s, LANES), kvmap),
            pl.BlockSpec((None, None, LANES, s), kvmap),
            pl.BlockSpec((None, None, s, LANES), kvmap),
            pl.BlockSpec((None, None, LANES, nc), kvmap),
            pl.BlockSpec((None, None, nc, LANES), kvmap),
            pl.BlockSpec((None, None, TQ, LANES), lambda bi, h, c: (h, bi, c, 0)),
            pl.BlockSpec((None, ROWS, LANES), hmap),
            pl.BlockSpec((None, ROWS, 3 * TQ), hmap),
            _full((nc, nb)),
        ],
        out_specs=pl.BlockSpec((None, TQ, NSA_GROUP * HEAD_DIM), lambda bi, h, c: (bi, c, h)),
        out_shape=jax.ShapeDtypeStruct((b, s, NSA_HEADS * HEAD_DIM), MXU_DTYPE),
        scratch_shapes=[
            pltpu.VMEM((2 * LANES, nc), MXU_DTYPE),
            pltpu.VMEM((ROWS, 2 * LANES), MXU_DTYPE),
            pltpu.VMEM((ROWS, LANES), F32),
            pltpu.VMEM((ROWS, LANES), F32),
            pltpu.VMEM((ROWS, 2 * TQ), F32),
            pltpu.VMEM((ROWS, 2 * TQ), F32),
            pltpu.VMEM((ROWS, LANES), F32),
        ],
        compiler_params=_params("parallel", "parallel", "arbitrary"),
        name="nsa_attention",
    )(q8, ks, vs, kw, vw, kc, vc, gl, ac, bm, agg)


def _conv_kernel(cu_ref, halo_ref, w_ref, b_ref, g_ref, bb_ref, o_ref, hh_scr, sh_scr):
    ts = cu_ref.shape[0]
    ch = CONV_CHANNELS

    def glu(u):
        return u[:, :ch] * _sigmoid(u[:, ch:])

    first = pl.program_id(1) == 0
    hh_scr[0:CONV_HALO, :] = jnp.where(first, 0.0, glu(halo_ref[...]))
    hh_scr[CONV_HALO:CONV_HALO + ts, :] = glu(cu_ref[...])
    hh_scr[CONV_HALO + ts:, :] = jnp.zeros((SUBLANES, ch), F32)
    acc = jnp.broadcast_to(b_ref[...], (ts, ch))
    lead = CONV_HALO - (CONV_WIDTH - 1)
    for shift in range(SUBLANES):
        sh_scr[...] = hh_scr[pl.ds(shift, CONV_HALO + ts), :]
        for start in range(shift, lead + CONV_WIDTH, SUBLANES):
            w = start - lead
            if 0 <= w < CONV_WIDTH:
                acc = acc + sh_scr[pl.ds(start - shift, ts), :] * w_ref[w:w + 1, :]
    mu = jnp.mean(acc, axis=-1, keepdims=True)
    xc = acc - mu
    y = xc * lax.rsqrt(jnp.mean(xc * xc, axis=-1, keepdims=True) + EPS) * g_ref[...] + bb_ref[...]
    o_ref[...] = _silu(y).astype(o_ref.dtype)


def _conv(cu, w, bias, ln_g, ln_b):
    b, s, _ = cu.shape
    ts = min(TS_CONV, s)
    per = ts // CONV_HALO
    return pl.pallas_call(
        _conv_kernel,
        grid=(b, s // ts),
        in_specs=[
            pl.BlockSpec((None, ts, C_CU), lambda bi, i: (bi, i, 0)),
            pl.BlockSpec((None, CONV_HALO, C_CU), lambda bi, i: (bi, jnp.maximum(i * per - 1, 0), 0)),
            _full((CONV_HALO, CONV_CHANNELS)), _full((1, CONV_CHANNELS)),
            _full((1, CONV_CHANNELS)), _full((1, CONV_CHANNELS)),
        ],
        out_specs=pl.BlockSpec((None, ts, CONV_CHANNELS), lambda bi, i: (bi, i, 0)),
        out_shape=jax.ShapeDtypeStruct((b, s, CONV_CHANNELS), MXU_DTYPE),
        scratch_shapes=[pltpu.VMEM((CONV_HALO + ts + SUBLANES, CONV_CHANNELS), F32),
                        pltpu.VMEM((CONV_HALO + ts, CONV_CHANNELS), F32)],
        compiler_params=_params("parallel", "parallel"),
        name="conformer_conv",
    )(cu, cu, w, bias, ln_g, ln_b)


def _gla_kernel(q_ref, k_ref, v_ref, r_ref, ga_ref, wa_ref, ba_ref, g_ref, tri_ref, hm_ref, vm_ref, bmk_ref,
                gm_ref, o_ref, st_scr, sb_scr, o_scr, u_scr):
    ts = q_ref.shape[0]
    nch = ts // GLA_CHUNK
    dk = GLA_KEY_DIM // GLA_HEADS
    cs = GLA_CHUNK

    @pl.when(pl.program_id(1) == 0)
    def _():
        st_scr[...] = jnp.zeros(st_scr.shape, F32)

    x = _dot(ga_ref[...].astype(MXU_DTYPE), wa_ref[...]) + ba_ref[...]
    log_a = (jnp.minimum(x, 0.0) - jnp.log(1.0 + jnp.exp(-jnp.abs(x)))) * (1.0 / GLA_TAU)
    wide = jnp.concatenate([log_a[n * cs:(n + 1) * cs] for n in range(nch)], axis=1)
    l_hi, l_mid, l_lo = _split3(wide)
    tri = tri_ref[...]
    b_wide = _dot(tri, l_hi) + _dot(tri, l_mid) + _dot(tri, l_lo)
    b = jnp.concatenate([b_wide[:, n * GLA_KEY_DIM:(n + 1) * GLA_KEY_DIM] for n in range(nch)], axis=0)
    eb = jnp.exp(b)
    q_t = (q_ref[...] * (dk ** -0.5) * eb).astype(MXU_DTYPE)
    k_all = k_ref[...]
    k_t = (k_all * jnp.exp(-b)).astype(MXU_DTYPE)
    v_all = v_ref[...].astype(MXU_DTYPE)
    causal = tri_ref[...] > 0
    blockmask = bmk_ref[...]

    for n in range(nch):
        r = slice(n * cs, (n + 1) * cs)
        b_last = b[(n + 1) * cs - 1:(n + 1) * cs, :]
        k_d = (k_all[r] * jnp.exp(b_last - b[r])).astype(MXU_DTYPE)
        u_scr[n] = _dot_tn(v_all[r], k_d) * blockmask
    state = st_scr[...]
    for n in range(nch):
        sb_scr[n] = state.astype(MXU_DTYPE)
        b_last = b[(n + 1) * cs - 1:(n + 1) * cs, :]
        state = state * jnp.exp(b_last) + u_scr[n]
    st_scr[...] = state

    causal_h = jnp.concatenate([causal] * GLA_HEADS, axis=0)
    for n in range(nch):
        r = slice(n * cs, (n + 1) * cs)
        qn = q_t[r]
        o = _dot_nt(qn, sb_scr[n])
        q_heads = jnp.concatenate([qn * hm_ref[h:h + 1, :] for h in range(GLA_HEADS)], axis=0)
        attn = jnp.where(causal_h, _dot_nt(q_heads, k_t[r]), 0.0).astype(MXU_DTYPE)
        pv = _dot(attn, v_all[r])
        for h in range(GLA_HEADS):
            o = o + pv[h * cs:(h + 1) * cs] * vm_ref[h:h + 1, :]
        o_scr[r, :] = o
    o = o_scr[...]
    o_hi, o_mid, o_lo = _split3(o * o)
    gm = gm_ref[...]
    ms = _dot(o_hi, gm) + _dot(o_mid, gm) + _dot(o_lo, gm)
    y = o * lax.rsqrt(ms + EPS) * g_ref[...]
    o_ref[...] = (y * _silu(r_ref[...])).astype(o_ref.dtype)


def _gla(gla, misc, wa_pad, ba, g):
    b, s, _ = gla.shape
    ts = min(TS_GLA, s)
    nch = ts // GLA_CHUNK
    dk, dv = GLA_KEY_DIM // GLA_HEADS, GLA_VALUE_DIM // GLA_HEADS
    tri = np.tril(np.ones((GLA_CHUNK, GLA_CHUNK), np.float32))
    hm = (np.arange(GLA_KEY_DIM)[None, :] // dk == np.arange(GLA_HEADS)[:, None]).astype(np.float32)
    vm = (np.arange(GLA_VALUE_DIM)[None, :] // dv == np.arange(GLA_HEADS)[:, None]).astype(np.float32)
    bmk = (np.arange(GLA_VALUE_DIM)[:, None] // dv == np.arange(GLA_KEY_DIM)[None, :] // dk).astype(np.float32)
    gm = (np.arange(GLA_VALUE_DIM)[:, None] // dv == np.arange(GLA_VALUE_DIM)[None, :] // dv).astype(np.float32) / dv
    return pl.pallas_call(
        _gla_kernel,
        grid=(b, s // ts),
        in_specs=[
            pl.BlockSpec((None, ts, GLA_KEY_DIM), lambda bi, i: (bi, i, 0)),
            pl.BlockSpec((None, ts, GLA_KEY_DIM), lambda bi, i: (bi, i, 1)),
            pl.BlockSpec((None, ts, GLA_VALUE_DIM), lambda bi, i: (bi, i, 1)),
            pl.BlockSpec((None, ts, GLA_VALUE_DIM), lambda bi, i: (bi, i, 2)),
            pl.BlockSpec((None, None, ts, LANES), lambda bi, i: (0, bi, i, 0)),
            _full((LANES, GLA_KEY_DIM)), _full((1, GLA_KEY_DIM)), _full((1, GLA_VALUE_DIM)),
            _full(tri.shape), _full(hm.shape), _full(vm.shape), _full(bmk.shape), _full(gm.shape),
        ],
        out_specs=pl.BlockSpec((None, ts, GLA_VALUE_DIM), lambda bi, i: (bi, i, 0)),
        out_shape=jax.ShapeDtypeStruct((b, s, GLA_VALUE_DIM), MXU_DTYPE),
        scratch_shapes=[
            pltpu.VMEM((GLA_VALUE_DIM, GLA_KEY_DIM), F32),
            pltpu.VMEM((nch, GLA_VALUE_DIM, GLA_KEY_DIM), MXU_DTYPE),
            pltpu.VMEM((ts, GLA_VALUE_DIM), F32),
            pltpu.VMEM((nch, GLA_VALUE_DIM, GLA_KEY_DIM), F32),
        ],
        compiler_params=_params("parallel", "arbitrary"),
        name="gla",
    )(gla, gla, gla, gla, misc, wa_pad, ba, g,
      jnp.asarray(tri, MXU_DTYPE), jnp.asarray(hm, MXU_DTYPE), jnp.asarray(vm), jnp.asarray(bmk),
      jnp.asarray(gm, MXU_DTYPE))


def _post_kernel(x_ref, yn_ref, yc_ref, yg_ref, p_ref, wo_ref, gf_ref, wg_ref, wu_ref, wd_ref,
                 gp_ref, wpg_ref, wpp_ref, gfin_ref, o_ref, *, final):
    dn = yn_ref.shape[1]
    dc = yc_ref.shape[1]
    mix = _dot(yn_ref[...], wo_ref[0:dn, :]) + _dot(yc_ref[...], wo_ref[dn:dn + dc, :]) \
        + _dot(yg_ref[...], wo_ref[dn + dc:, :])
    x = x_ref[...] + mix
    h = _rms(x, gf_ref[...]).astype(MXU_DTYPE)
    dff = wg_ref.shape[1]
    step = dff // FF_CHUNKS
    ff = None
    for j in range(FF_CHUNKS):
        cols = slice(j * step, (j + 1) * step)
        act = _silu(_dot(h, wg_ref[:, cols])) * _dot(h, wu_ref[:, cols])
        down = _dot(act.astype(MXU_DTYPE), wd_ref[cols, :])
        ff = down if ff is None else ff + down
    x = x + ff
    gate = _sigmoid(_dot(_rms(x, gp_ref[...]).astype(MXU_DTYPE), wpg_ref[...]))
    x = x + _dot(p_ref[...].astype(MXU_DTYPE), wpp_ref[...]) * gate
    if final:
        x = _rms(x, gfin_ref[...])
    o_ref[...] = x


def _post(x2d, yn, yc, yg, p_all, layer, wo, gf, wg, wu, wd, gp, wpg, wpp, gfin, final):
    t, d = x2d.shape
    tm = min(TM_PROJ, t)
    row = lambda i: (i, 0)
    once = dict(pipeline_mode=pl.Buffered(1))

    def const(a):
        return pl.BlockSpec(a.shape, lambda i: (0,) * a.ndim, **once)

    return pl.pallas_call(
        functools.partial(_post_kernel, final=final),
        grid=(t // tm,),
        in_specs=[
            pl.BlockSpec((tm, d), row),
            pl.BlockSpec((tm, yn.shape[1]), row), pl.BlockSpec((tm, yc.shape[1]), row),
            pl.BlockSpec((tm, yg.shape[1]), row),
            pl.BlockSpec((None, tm, p_all.shape[2]), lambda i: (layer, i, 0)),
            const(wo), const(gf), const(wg), const(wu), const(wd), const(gp), const(wpg), const(wpp),
            const(gfin),
        ],
        out_specs=pl.BlockSpec((tm, d), row),
        out_shape=jax.ShapeDtypeStruct((t, d), F32),
        compiler_params=_params("parallel"),
        name="post_mixer",
    )(x2d, yn, yc, yg, p_all, wo, gf, wg, wu, wd, gp, wpg, wpp, gfin)


def _layer(x2d, p_all, layer, bsz, seq, rel_tabs, agg, lw, final_g, final):
    (mix_g, w_in, w_out, pos_kv, w1_kv, b1_kv, w2_kv, conv_w, conv_b, conv_ln_g, conv_ln_b,
     w_alpha, b_alpha, gla_g, ffn_g, w_gate, w_up, w_down, ple_g, w_pg, w_pp, qconst) = lw
    bm, ac = rel_tabs
    t = bsz * seq
    hk = NSA_KV_HEADS

    w_all = _gather_columns(w_in, _W_RUNS).astype(MXU_DTYPE)
    wk_t = _gather_columns(w_in, _WK_RUNS).T.astype(MXU_DTYPE)
    q8, ks, kw, vs, vw, kvc, cu, gla, misc = _in_proj(x2d, mix_g[None, :], w_all, wk_t, qconst, bsz, seq)

    nsub = seq // CMP_STRIDE
    cmp = _compress(kvc.reshape(bsz, seq, C_KVC), pos_kv, w1_kv, b1_kv, w2_kv)
    cmp = cmp.reshape(2, bsz, nsub, hk, HEAD_DIM)
    ones_row = jnp.ones((bsz, hk, 2, nsub), MXU_DTYPE)
    kc = jnp.concatenate([cmp[0].transpose(0, 2, 3, 1), ones_row,
                          jnp.zeros((bsz, hk, LANES - HEAD_DIM - 2, nsub), MXU_DTYPE)], axis=2)
    ones_col = jnp.ones((bsz, hk, nsub, 1), MXU_DTYPE)
    vc = jnp.concatenate([cmp[1].transpose(0, 2, 1, 3), ones_col,
                          jnp.zeros((bsz, hk, nsub, LANES - HEAD_DIM - 1), MXU_DTYPE)], axis=3)

    y_nsa = _nsa(q8.reshape(NSA_HEADS, bsz, seq, LANES), ks, vs, kw, vw, kc, vc,
                 misc.reshape(hk, bsz, seq, LANES), ac, bm, agg)

    y_conv = _conv(cu.reshape(bsz, seq, C_CU), conv_w, conv_b[None, :], conv_ln_g[None, :], conv_ln_b[None, :])
    y_gla = _gla(gla.reshape(bsz, seq, C_GLA), misc.reshape(hk, bsz, seq, LANES), w_alpha, b_alpha[None, :],
                 gla_g[None, :])

    return _post(x2d, y_nsa.reshape(t, -1), y_conv.reshape(t, -1), y_gla.reshape(t, -1), p_all, layer,
                 w_out.astype(MXU_DTYPE), ffn_g[None, :], w_gate.astype(MXU_DTYPE), w_up.astype(MXU_DTYPE),
                 w_down.astype(MXU_DTYPE), ple_g[None, :], w_pg.astype(MXU_DTYPE), w_pp.astype(MXU_DTYPE),
                 final_g[None, :], final)


def kernel(x, p, rel_bias, mix_norm_g, w_in, w_out, cmp_pos_k, cmp_w1_k, cmp_b1_k, cmp_w2_k, cmp_pos_v, cmp_w1_v, cmp_b1_v, cmp_w2_v, conv_w, conv_b, conv_ln_g, conv_ln_b, gla_w_alpha, gla_b_alpha, gla_norm_g, ffn_norm_g, ffn_w_gate, ffn_w_up, ffn_w_down, ple_norm_g, ple_w_gate, ple_w_proj, final_norm_g):
    bsz, seq, d = x.shape
    depth = w_in.shape[0]
    assert seq % TQ == 0 and seq // SLC_BLOCK <= LANES
    t = bsz * seq
    rel_tabs = _nsa_bias_tables(rel_bias.astype(F32))
    nbp = LANES
    agg = jnp.asarray(_agg_matrix(seq // CMP_STRIDE, nbp), MXU_DTYPE)

    far = rel_bias[REL_BUCKETS - 1].astype(F32)
    f_hi, f_lo = _split_hi_lo(far)
    qconst = jnp.zeros((NSA_HEADS + 1, LANES), F32)
    qconst = qconst.at[:NSA_HEADS, FAR_LANE].set(f_hi.astype(F32)).at[:NSA_HEADS, FAR_LANE + 1].set(f_lo.astype(F32))
    qconst = qconst.at[NSA_HEADS, HEAD_DIM].set(1.0)

    x2d = x.reshape(t, d)
    for i in range(depth):
        wa_pad = jnp.zeros((LANES, GLA_KEY_DIM), F32).at[GA_LANE0:GA_LANE0 + GLA_GATE_RANK].set(gla_w_alpha[i])
        conv_w_pad = jnp.zeros((CONV_HALO, CONV_CHANNELS), F32).at[:CONV_WIDTH].set(conv_w[i])
        both = lambda a: jnp.concatenate([a] * NSA_KV_HEADS, axis=-1)
        w1_blocks = lambda w: _block_diag2(w.reshape(CMP_BLOCK, HEAD_DIM, CMP_HIDDEN))
        lw = (mix_norm_g[i], w_in[i], w_out[i],
              jnp.stack([both(cmp_pos_k[i]), both(cmp_pos_v[i])]),
              jnp.stack([w1_blocks(cmp_w1_k[i]), w1_blocks(cmp_w1_v[i])]).astype(MXU_DTYPE),
              jnp.stack([both(cmp_b1_k[i])[None, :], both(cmp_b1_v[i])[None, :]]),
              jnp.stack([_block_diag2(cmp_w2_k[i]), _block_diag2(cmp_w2_v[i])]).astype(MXU_DTYPE),
              conv_w_pad, conv_b[i], conv_ln_g[i], conv_ln_b[i],
              wa_pad.astype(MXU_DTYPE), gla_b_alpha[i], gla_norm_g[i],
              ffn_norm_g[i], ffn_w_gate[i], ffn_w_up[i], ffn_w_down[i],
              ple_norm_g[i], ple_w_gate[i], ple_w_proj[i], qconst)
        x2d = _layer(x2d, p.reshape(depth, t, -1), i, bsz, seq, rel_tabs, agg, lw, final_norm_g, i == depth - 1)
    return x2d.reshape(bsz, seq, d)
```

```python
import functools
import math

import numpy as np
import jax
import jax.numpy as jnp
from jax import lax
from jax.experimental import pallas as pl
from jax.experimental.pallas import tpu as pltpu

F32 = jnp.float32
MXU_DTYPE = jnp.bfloat16

HEAD_DIM = 64
NSA_HEADS = 8
NSA_KV_HEADS = 2
NSA_GROUP = NSA_HEADS // NSA_KV_HEADS
CMP_BLOCK = 32
CMP_STRIDE = 16
CMP_HIDDEN = 256
SLC_BLOCK = 64
N_SELECT = 16
WINDOW = 512
N_BRANCH = 3
CONV_CHANNELS = 256
CONV_WIDTH = 31
GLA_HEADS = 4
GLA_KEY_DIM = 128
GLA_VALUE_DIM = 256
GLA_GATE_RANK = 16
GLA_TAU = 16.0
GLA_CHUNK = 64
REL_BUCKETS = 32
REL_MAX_DIST = 128
PLE_DIM = 256
EPS = 1e-6
NEG_BIG = -1e30

LANES = 128
SUBLANES = 8
VMEM_LIMIT_BYTES = 56 * 1024 * 1024

TM_PROJ = 512
TQ = 256
ROWS = NSA_GROUP * TQ
NEAR_W = 2 * TQ // CMP_STRIDE
FAR_UNROLL_LOG2 = 3
assert WINDOW == 2 * TQ and REL_MAX_DIST <= TQ and TQ % SLC_BLOCK == 0 and TQ % CMP_STRIDE == 0
TS_CONV = 512
CONV_HALO = 32
TS_GLA = 1024
FF_CHUNKS = 1

C_Q = NSA_HEADS * HEAD_DIM
C_V = 2 * NSA_KV_HEADS * HEAD_DIM
C_KT = 2 * NSA_KV_HEADS * HEAD_DIM
C_KVC = 2 * NSA_KV_HEADS * HEAD_DIM
C_CU = 2 * CONV_CHANNELS
C_GLA = 2 * GLA_KEY_DIM + 2 * GLA_VALUE_DIM
C_MISC = NSA_KV_HEADS * LANES
GA_LANE0 = 16
FAR_LANE = HEAD_DIM


def _split_hi_lo(x):
    hi = x.astype(MXU_DTYPE)
    lo = (x - hi.astype(F32)).astype(MXU_DTYPE)
    return hi, lo


def _split3(x):
    hi = x.astype(MXU_DTYPE)
    r = x - hi.astype(F32)
    mid = r.astype(MXU_DTYPE)
    lo = (r - mid.astype(F32)).astype(MXU_DTYPE)
    return hi, mid, lo


def _dot(a, b):
    return jnp.dot(a, b, preferred_element_type=F32)


def _dot_halves(a, b):
    half = a.shape[0] // 2
    return jnp.concatenate([_dot(a[:half], b), _dot(a[half:], b)], axis=0)


def _dot_nt(a, b):
    return lax.dot_general(a, b, (((1,), (1,)), ((), ())), preferred_element_type=F32)


def _dot_tn(a, b):
    return lax.dot_general(a, b, (((0,), (0,)), ((), ())), preferred_element_type=F32)


def _rms(x, g):
    return x * lax.rsqrt(jnp.mean(x * x, axis=-1, keepdims=True) + EPS) * g


def _sigmoid(x):
    return 1.0 / (1.0 + jnp.exp(-x))


def _silu(x):
    return x * _sigmoid(x)


def _params(*sem):
    return pltpu.CompilerParams(dimension_semantics=sem, vmem_limit_bytes=VMEM_LIMIT_BYTES)


def _full(shape):
    nd = len(shape)
    return pl.BlockSpec(shape, lambda *_: (0,) * nd)


def _t5_bucket_np(dist):
    n = np.maximum(dist, 0)
    max_exact = REL_BUCKETS // 2
    nf = np.maximum(n, 1).astype(np.float32)
    large = max_exact + (np.log(nf / np.float32(max_exact)) / np.float32(math.log(REL_MAX_DIST / max_exact))
                         * np.float32(REL_BUCKETS - max_exact)).astype(np.int32)
    large = np.minimum(large, REL_BUCKETS - 1)
    return np.where(n < max_exact, n, large).astype(np.int32)


def _in_proj_columns(in_splits):
    offs = np.concatenate([[0], np.cumsum(in_splits)])
    (o_nq, o_kc, o_vc, o_ks, o_vs, o_kw, o_vw, o_ng, o_cu, o_gq, o_gk, o_gv, o_ga, o_gr) = offs[:-1]
    src, scale = [], []

    def put(cols, s=1.0):
        src.extend(cols)
        scale.extend([s] * len(cols))

    put(list(range(o_nq, o_nq + C_Q)), HEAD_DIM ** -0.5)
    kvw = NSA_KV_HEADS * HEAD_DIM
    for o in (o_vs, o_vw, o_kc, o_vc):
        put(list(range(o, o + kvw)))
    put(list(range(o_cu, o_cu + C_CU)))
    put(list(range(o_gq, o_gq + GLA_KEY_DIM)))
    put(list(range(o_gk, o_gk + GLA_KEY_DIM)))
    put(list(range(o_gv, o_gv + GLA_VALUE_DIM)))
    put(list(range(o_gr, o_gr + GLA_VALUE_DIM)))
    per = NSA_GROUP * N_BRANCH
    for h in range(NSA_KV_HEADS):
        slab = [-1] * LANES
        slab[:per] = list(range(o_ng + h * per, o_ng + (h + 1) * per))
        if h == 0:
            slab[GA_LANE0:GA_LANE0 + GLA_GATE_RANK] = list(range(o_ga, o_ga + GLA_GATE_RANK))
        put(slab)
    key_src = list(range(o_ks, o_ks + kvw)) + list(range(o_kw, o_kw + kvw))
    return np.asarray(src, np.int32), np.asarray(scale, np.float32), np.asarray(key_src, np.int32)


def _column_runs(src, scale):
    runs, i = [], 0
    while i < len(src):
        j = i + 1
        while j < len(src) and scale[j] == scale[i] and (
                (src[i] < 0 and src[j] < 0) or (src[i] >= 0 and src[j] == src[i] + (j - i))):
            j += 1
        runs.append((int(src[i]), j - i, float(scale[i])))
        i = j
    return runs


IN_SPLITS = (
    NSA_HEADS * HEAD_DIM,
    NSA_KV_HEADS * HEAD_DIM, NSA_KV_HEADS * HEAD_DIM,
    NSA_KV_HEADS * HEAD_DIM, NSA_KV_HEADS * HEAD_DIM,
    NSA_KV_HEADS * HEAD_DIM, NSA_KV_HEADS * HEAD_DIM,
    NSA_HEADS * N_BRANCH,
    2 * CONV_CHANNELS,
    GLA_KEY_DIM, GLA_KEY_DIM, GLA_VALUE_DIM,
    GLA_GATE_RANK,
    GLA_VALUE_DIM,
)
_W_SRC, _W_SCALE, _WK_SRC = _in_proj_columns(IN_SPLITS)
_W_RUNS = _column_runs(_W_SRC, _W_SCALE)
_WK_RUNS = _column_runs(_WK_SRC, np.ones_like(_WK_SRC, np.float32))
C_ALL = C_Q + C_V + C_KVC + C_CU + C_GLA + C_MISC
assert _W_SRC.shape[0] == C_ALL and _WK_SRC.shape[0] == C_KT


def _gather_columns(w, runs):
    parts = []
    for start, width, scale in runs:
        if start < 0:
            parts.append(jnp.zeros((w.shape[0], width), w.dtype))
        else:
            part = w[:, start:start + width]
            parts.append(part if scale == 1.0 else part * scale)
    return jnp.concatenate(parts, axis=1)


def _nsa_bias_tables(rel_bias):
    i = np.arange(TQ)[:, None]
    j = np.arange(TQ)[None, :]
    far = rel_bias[REL_BUCKETS - 1]
    tab = rel_bias - far[None, :]

    d = np.arange(2 * TQ - 1, -2 * TQ - 1, -1)
    by_dist = jnp.where(jnp.asarray(d >= 0)[:, None], jnp.take(tab, jnp.asarray(_t5_bucket_np(d)), axis=0), NEG_BIG).T

    n = by_dist.shape[1]
    skew = jnp.tile(by_dist, (1, TQ))[:, :TQ * (n - 1)].reshape(NSA_HEADS, TQ, n - 1)

    def toeplitz(dist0, step, width):
        first = 2 * TQ - 1 - dist0
        assert first >= TQ - 1 and first + step * (width - 1) < n - 1
        vals = lax.slice_in_dim(skew, first, first + step * (width - 1) + 1, stride=step, axis=2)
        return vals.reshape(NSA_KV_HEADS, ROWS, width)

    tri = jnp.asarray(np.where(j <= i, NEG_BIG, 0.0).astype(np.float32))
    tri = jnp.broadcast_to(jnp.tile(tri, (NSA_GROUP, 1))[None], (NSA_KV_HEADS, ROWS, TQ))
    bm = jnp.concatenate([tri, toeplitz(TQ, 1, 2 * TQ)], axis=-1)

    near = toeplitz((NEAR_W // 2) * CMP_STRIDE - (CMP_BLOCK - 1), CMP_STRIDE, NEAR_W)
    hi, lo = _split_hi_lo(near)
    pad = jnp.zeros((NSA_KV_HEADS, ROWS, LANES - 2 * NEAR_W - 1), MXU_DTYPE)
    big = jnp.full((NSA_KV_HEADS, ROWS, 1), NEG_BIG, MXU_DTYPE)
    ac = jnp.concatenate([hi, lo, big, pad], axis=-1)
    return bm, ac


def _agg_matrix(ncp, nbp):
    rs, rc = SLC_BLOCK // CMP_STRIDE, CMP_BLOCK // CMP_STRIDE
    agg = np.zeros((ncp, nbp), np.float32)
    ncmp = ncp - rc + 1
    for jb in range(nbp):
        for mm in range(rs):
            for nn in range(rc):
                idx = jb * rs + mm - nn
                if 0 <= idx < ncmp:
                    agg[idx, jb] += 1.0
    return agg


def _in_proj_kernel(x_ref, g_ref, w_ref, wk_ref, qc_ref, q_ref, ks_ref, kw_ref, vs_ref, vw_ref,
                    kvc_ref, cu_ref, gla_ref, misc_ref, *, tiles_per_seq):
    h = _rms(x_ref[...], g_ref[...]).astype(MXU_DTYPE)
    tm = h.shape[0]
    low = lax.broadcasted_iota(jnp.int32, (tm, LANES), 1) < HEAD_DIM

    def head_group(z, idx, const):
        g = z[:, (idx // 2) * LANES:(idx // 2 + 1) * LANES]
        if idx % 2:
            g = pltpu.roll(g, HEAD_DIM, 1)
        return jnp.where(low, g, const)

    o = 0
    zq = _dot(h, w_ref[:, o:o + C_Q])
    for hd in range(NSA_HEADS):
        q_ref[hd] = head_group(zq, hd, qc_ref[hd:hd + 1, :]).astype(q_ref.dtype)
    o += C_Q
    zv = _dot(h, w_ref[:, o:o + C_V])
    ones_lane = qc_ref[NSA_HEADS:NSA_HEADS + 1, :]
    for hh in range(NSA_KV_HEADS):
        vs_ref[hh] = head_group(zv, hh, ones_lane).astype(vs_ref.dtype)
        vw_ref[hh] = head_group(zv, NSA_KV_HEADS + hh, ones_lane).astype(vw_ref.dtype)
    o += C_V
    kt = _dot_nt(wk_ref[...], h)
    rows = lax.broadcasted_iota(jnp.int32, (LANES - HEAD_DIM, tm), 0)
    ones_rows = jnp.where(rows < 2, 1.0, 0.0).astype(ks_ref.dtype)
    tok = (pl.program_id(0) % tiles_per_seq) * tm + lax.broadcasted_iota(jnp.int32, (LANES, tm), 1)
    blk = lax.broadcasted_iota(jnp.int32, (LANES, tm), 0)
    blk_rows = jnp.where(lax.shift_right_logical(tok, int(math.log2(SLC_BLOCK))) == blk, 1.0, 0.0).astype(ks_ref.dtype)
    for hh in range(NSA_KV_HEADS):
        ks_ref[hh, 0:HEAD_DIM, :] = kt[hh * HEAD_DIM:(hh + 1) * HEAD_DIM].astype(ks_ref.dtype)
        ks_ref[hh, HEAD_DIM:LANES, :] = ones_rows
        ks_ref[hh, LANES:2 * LANES, :] = blk_rows
        kw_ref[hh, 0:HEAD_DIM, :] = kt[(NSA_KV_HEADS + hh) * HEAD_DIM:(NSA_KV_HEADS + hh + 1) * HEAD_DIM].astype(kw_ref.dtype)
        kw_ref[hh, HEAD_DIM:LANES, :] = ones_rows
    kvc_ref[...] = _dot(h, w_ref[:, o:o + C_KVC])
    o += C_KVC
    cu_ref[...] = _dot(h, w_ref[:, o:o + C_CU])
    o += C_CU
    gla_ref[...] = _dot(h, w_ref[:, o:o + C_GLA])
    o += C_GLA
    zm = _dot(h, w_ref[:, o:o + C_MISC])
    for s in range(NSA_KV_HEADS):
        misc_ref[s] = zm[:, s * LANES:(s + 1) * LANES]


def _in_proj(x2d, g, w_all, wk_t, qconst, bsz, seq):
    t, d = x2d.shape
    tm = min(TM_PROJ, seq)
    tps = seq // tm
    hk = NSA_KV_HEADS
    row = lambda i: (i, 0)
    kmap = lambda i: (i // tps, 0, 0, i % tps)
    vmap = lambda i: (i // tps, 0, i % tps, 0)
    return pl.pallas_call(
        functools.partial(_in_proj_kernel, tiles_per_seq=tps),
        grid=(t // tm,),
        in_specs=[pl.BlockSpec((tm, d), row), _full((1, d)), _full((d, C_ALL)), _full((C_KT, d)),
                  _full((NSA_HEADS + 1, LANES))],
        out_specs=[
            pl.BlockSpec((NSA_HEADS, tm, LANES), lambda i: (0, i, 0)),
            pl.BlockSpec((None, hk, 2 * LANES, tm), kmap),
            pl.BlockSpec((None, hk, LANES, tm), kmap),
            pl.BlockSpec((None, hk, tm, LANES), vmap),
            pl.BlockSpec((None, hk, tm, LANES), vmap),
            pl.BlockSpec((tm, C_KVC), row),
            pl.BlockSpec((tm, C_CU), row),
            pl.BlockSpec((tm, C_GLA), row),
            pl.BlockSpec((NSA_KV_HEADS, tm, LANES), lambda i: (0, i, 0)),
        ],
        out_shape=[
            jax.ShapeDtypeStruct((NSA_HEADS, t, LANES), MXU_DTYPE),
            jax.ShapeDtypeStruct((bsz, hk, 2 * LANES, seq), MXU_DTYPE),
            jax.ShapeDtypeStruct((bsz, hk, LANES, seq), MXU_DTYPE),
            jax.ShapeDtypeStruct((bsz, hk, seq, LANES), MXU_DTYPE),
            jax.ShapeDtypeStruct((bsz, hk, seq, LANES), MXU_DTYPE),
            jax.ShapeDtypeStruct((t, C_KVC), F32),
            jax.ShapeDtypeStruct((t, C_CU), F32),
            jax.ShapeDtypeStruct((t, C_GLA), F32),
            jax.ShapeDtypeStruct((NSA_KV_HEADS, t, LANES), F32),
        ],
        compiler_params=_params("parallel"),
        name="in_proj",
    )(x2d, g, w_all, wk_t, qconst)


def _compress_kernel(x_ref, pos_ref, w1_ref, b1_ref, w2_ref, o_ref):
    nc = o_ref.shape[0]
    top = bot = None
    for l in range(0, CMP_STRIDE, 2):
        xa = x_ref[pl.ds(l, nc, stride=CMP_STRIDE), :]
        xb = x_ref[pl.ds(l + 1, nc, stride=CMP_STRIDE), :]

        def pair(first):
            lhs = jnp.concatenate([(xa + pos_ref[first:first + 1, :]).astype(MXU_DTYPE),
                                   (xb + pos_ref[first + 1:first + 2, :]).astype(MXU_DTYPE)], axis=1)
            return _dot(lhs, w1_ref[first:first + 2].reshape(2 * xa.shape[1], -1))

        t = pair(l)
        u = pair(CMP_STRIDE + l)
        top = t if top is None else top + t
        bot = u if bot is None else bot + u
    hid = _silu(top + pltpu.roll(bot, nc - 1, 0) + b1_ref[...])
    o_ref[...] = _dot(hid.astype(MXU_DTYPE), w2_ref[...]).astype(o_ref.dtype)


def _compress(kvc, pos, w1, b1, w2):
    b, s, _ = kvc.shape
    nc = s // CMP_STRIDE
    hw = NSA_KV_HEADS * HEAD_DIM
    hid = NSA_KV_HEADS * CMP_HIDDEN
    return pl.pallas_call(
        _compress_kernel,
        grid=(2, b),
        in_specs=[
            pl.BlockSpec((None, s, hw), lambda kv, bi: (bi, 0, kv)),
            pl.BlockSpec((None, CMP_BLOCK, hw), lambda kv, bi: (kv, 0, 0)),
            pl.BlockSpec((None, CMP_BLOCK, hw, hid), lambda kv, bi: (kv, 0, 0, 0)),
            pl.BlockSpec((None, 1, hid), lambda kv, bi: (kv, 0, 0)),
            pl.BlockSpec((None, hid, hw), lambda kv, bi: (kv, 0, 0)),
        ],
        out_specs=pl.BlockSpec((None, None, nc, hw), lambda kv, bi: (kv, bi, 0, 0)),
        out_shape=jax.ShapeDtypeStruct((2, b, nc, hw), MXU_DTYPE),
        compiler_params=_params("parallel", "parallel"),
        name="compress_kv",
    )(kvc, pos, w1, b1, w2)


def _block_diag2(w):
    z = jnp.zeros_like(w)
    return jnp.concatenate([jnp.concatenate([w, z], axis=-1), jnp.concatenate([z, w], axis=-1)], axis=-2)


def _nsa_kernel(*refs):
    c = pl.program_id(2)

    @pl.when(c >= 2)
    def _():
        _nsa_body(True, c, *refs)

    @pl.when(c < 2)
    def _():
        _nsa_body(False, c, *refs)


def _nsa_block_masks(tile, nb):
    t_q = tile * TQ + lax.broadcasted_iota(jnp.int32, (nb, TQ), 1)
    blk = lax.broadcasted_iota(jnp.int32, (nb, TQ), 0)
    cur = lax.shift_right_logical(t_q, int(math.log2(SLC_BLOCK)))
    return blk, blk <= cur, (blk == 0) | (blk == cur) | (blk == cur - 1)


def _nsa_compressed(tile, q, kc_ref, vc_ref, ac_ref, agg_ref, kcx_scr):
    nc = kc_ref.shape[1]
    nb = agg_ref.shape[1]
    rowi = lax.broadcasted_iota(jnp.int32, (LANES, nc), 0)
    coli = lax.broadcasted_iota(jnp.int32, (LANES, nc), 1)
    near0 = tile * (TQ // CMP_STRIDE) - NEAR_W // 2
    shift = (coli == near0 + (rowi & (NEAR_W - 1))) & (rowi < 2 * NEAR_W)
    future = (rowi == 2 * NEAR_W) & (coli >= near0 + NEAR_W)
    kcx_scr[0:LANES, :] = kc_ref[...]
    kcx_scr[LANES:2 * LANES, :] = jnp.where(shift | future, 1.0, 0.0).astype(MXU_DTYPE)
    s_c = _dot_halves(jnp.concatenate([q, ac_ref[...]], axis=1), kcx_scr[...])
    m_c = jnp.maximum(jnp.max(s_c, axis=1, keepdims=True), 0.1 * NEG_BIG)
    e_c = jnp.exp(s_c - m_c)
    l_c = jnp.sum(e_c, axis=1, keepdims=True)
    p_c = e_c * (1.0 / jnp.maximum(l_c, 1e-30))
    acc_c = _dot_halves(p_c.astype(MXU_DTYPE), vc_ref[...])

    imp = p_c[0:TQ] + p_c[TQ:2 * TQ] + p_c[2 * TQ:3 * TQ] + p_c[3 * TQ:4 * TQ]
    agg = agg_ref[...]
    i_hi, i_mid, i_lo = _split3(imp)
    imps = _dot(i_hi, agg) + _dot(i_mid, agg) + _dot(i_lo, agg)
    _, causal, forced = _nsa_block_masks(tile, nb)
    return acc_c, jnp.where(causal & jnp.logical_not(forced), imps.T, -jnp.inf)


def _nsa_select(tile, cand):
    nb = cand.shape[0]
    blk, causal, forced = _nsa_block_masks(tile, nb)
    neg_inf = -jnp.inf
    blk_f = blk.astype(F32)
    for _ in range(N_SELECT - 3):
        best = jnp.max(cand, axis=0, keepdims=True)
        first = jnp.min(jnp.where(cand == best, blk_f, float(nb)), axis=0, keepdims=True)
        cand = jnp.where(blk_f == first, neg_inf, cand)
    sel = causal & (forced | (cand == neg_inf))
    selneg = jnp.where(sel, 0.0, NEG_BIG).T.astype(MXU_DTYPE)
    return jnp.concatenate([selneg] * NSA_GROUP, axis=0)


def _nsa_body(full, c, q_ref, ks_ref, vs_ref, kw_ref, vw_ref, kc_ref, vc_ref, gl_ref, ac_ref, bm_ref,
              agg_ref, o_ref, kcx_scr, qs_scr, m_scr, acc_scr, sa_scr, sb_scr, part_scr):
    q = q_ref[...].reshape(ROWS, LANES)
    acc_c, cand = _nsa_compressed(c, q, kc_ref, vc_ref, ac_ref, agg_ref, kcx_scr)
    qs_scr[:, 0:LANES] = q
    qs_scr[:, LANES:2 * LANES] = _nsa_select(c, cand)

    def flash_init():
        m_scr[...] = jnp.full(m_scr.shape, NEG_BIG, F32)
        acc_scr[...] = jnp.zeros(acc_scr.shape, F32)

    def scores(k_ref, kt, ntiles=1):
        start = pl.multiple_of(kt * TQ, TQ)
        return _dot(qs_scr[:, 0:k_ref.shape[0]], k_ref[:, pl.ds(start, ntiles * TQ)])

    def flash_update(s, v_ref, kt):
        start = pl.multiple_of(kt * TQ, TQ)
        width = s.shape[1]
        m_prev = m_scr[...]
        m_new = jnp.maximum(m_prev, jnp.max(s, axis=1, keepdims=True))
        p = jnp.exp(s - jnp.concatenate([m_new] * (width // LANES), axis=1))
        acc_scr[...] = jnp.exp(m_prev - m_new) * acc_scr[...] \
            + _dot(p.astype(MXU_DTYPE), v_ref[pl.ds(start, width), :])
        m_scr[...] = m_new

    def flash_step(k_ref, v_ref, kt, bias_tile):
        s = scores(k_ref, kt)
        if bias_tile is not None:
            s = s + bm_ref[:, bias_tile * TQ:(bias_tile + 1) * TQ]
        flash_update(s, v_ref, kt)

    def flash_out():
        acc = acc_scr[...]
        return acc * (1.0 / acc[:, HEAD_DIM:HEAD_DIM + 1])

    if full:
        w0 = pl.multiple_of((c - 2) * TQ, TQ)
        s_w = _dot_halves(q, kw_ref[:, pl.ds(w0, 3 * TQ)]) + bm_ref[...]
        p_w = jnp.exp(s_w - jnp.max(s_w, axis=1, keepdims=True))
        acc_w = _dot_halves(p_w.astype(MXU_DTYPE), vw_ref[pl.ds(w0, 3 * TQ), :])
        o_w = acc_w * (1.0 / acc_w[:, HEAD_DIM:HEAD_DIM + 1])
    else:
        flash_init()

        @pl.when(c >= 1)
        def _():
            flash_step(kw_ref, vw_ref, c - 1, 1)

        flash_step(kw_ref, vw_ref, c, 2)
        o_w = flash_out()

    gates = _sigmoid(gl_ref[...])
    for g in range(NSA_GROUP):
        r = slice(g * TQ, (g + 1) * TQ)
        part_scr[r, :] = gates[:, g * N_BRANCH:g * N_BRANCH + 1] * acc_c[r] \
            + gates[:, g * N_BRANCH + 2:g * N_BRANCH + 3] * o_w[r]

    flash_init()
    if full:
        n_far = c - 1
        n_loop = lax.shift_right_logical(n_far, FAR_UNROLL_LOG2)

        sa_scr[...] = scores(ks_ref, 0, 2)

        def far_pair(t0):
            sb_scr[...] = scores(ks_ref, t0 + 2, 2)
            flash_update(sa_scr[...], vs_ref, t0)
            sa_scr[...] = scores(ks_ref, t0 + 4, 2)
            flash_update(sb_scr[...], vs_ref, t0 + 2)

        def far_block(t0, ntiles):
            for k in range(0, ntiles, 4):
                far_pair(t0 + k)

        def far_body(j, carry):
            far_block(j * (1 << FAR_UNROLL_LOG2), 1 << FAR_UNROLL_LOG2)
            return carry

        lax.fori_loop(0, n_loop, far_body, 0)
        done = n_loop * (1 << FAR_UNROLL_LOG2)
        for bit in range(FAR_UNROLL_LOG2 - 1, 1, -1):
            size = 1 << bit
            take = (n_far & size) != 0

            @pl.when(take)
            def _(done=done, size=size):
                far_block(done, size)

            done = done + jnp.where(take, size, 0)

        @pl.when((n_far & 2) != 0)
        def _(done=done):
            flash_update(sa_scr[...], vs_ref, done)
            sa_scr[...] = scores(ks_ref, done + 2, 2)

        @pl.when((n_far & 1) == 1)
        def _():
            flash_update(sa_scr[:, 0:TQ], vs_ref, n_far - 1)

        flash_update(scores(ks_ref, c - 1, 2) + bm_ref[:, TQ:3 * TQ], vs_ref, c - 1)
    else:
        @pl.when(c >= 1)
        def _():
            flash_step(ks_ref, vs_ref, c - 1, 1)

        flash_step(ks_ref, vs_ref, c, 2)
    o_s = flash_out()

    gates = _sigmoid(gl_ref[...])
    lane = lax.broadcasted_iota(jnp.int32, (TQ, LANES), 1)
    outs = []
    for g in range(NSA_GROUP):
        r = slice(g * TQ, (g + 1) * TQ)
        outs.append(part_scr[r, :] + gates[:, g * N_BRANCH + 1:g * N_BRANCH + 2] * o_s[r])
    for pair in range(NSA_GROUP // 2):
        both = jnp.where(lane < HEAD_DIM, outs[2 * pair], pltpu.roll(outs[2 * pair + 1], HEAD_DIM, 1))
        o_ref[:, pair * LANES:(pair + 1) * LANES] = both.astype(o_ref.dtype)


def _nsa(q8, ks, vs, kw, vw, kc, vc, gl, ac, bm, agg):
    _, b, s, _ = q8.shape
    nc = kc.shape[-1]
    nb = agg.shape[1]
    kvmap = lambda bi, h, c: (bi, h, 0, 0)
    hmap = lambda bi, h, c: (h, 0, 0)
    return pl.pallas_call(
        _nsa_kernel,
        grid=(b, NSA_KV_HEADS, s // TQ),
        in_specs=[
            pl.BlockSpec((NSA_GROUP, None, TQ, LANES), lambda bi, h, c: (h, bi, c, 0)),
            pl.BlockSpec((None, None, 2 * LANES, s), kvmap),
            pl.BlockSpec((None, None, s, LANES), kvmap),
            pl.BlockSpec((None, None, LANES, s), kvmap),
            pl.BlockSpec((None, None, s, LANES), kvmap),
            pl.BlockSpec((None, None, LANES, nc), kvmap),
            pl.BlockSpec((None, None, nc, LANES), kvmap),
            pl.BlockSpec((None, None, TQ, LANES), lambda bi, h, c: (h, bi, c, 0)),
            pl.BlockSpec((None, ROWS, LANES), hmap),
            pl.BlockSpec((None, ROWS, 3 * TQ), hmap),
            _full((nc, nb)),
        ],
        out_specs=pl.BlockSpec((None, TQ, NSA_GROUP * HEAD_DIM), lambda bi, h, c: (bi, c, h)),
        out_shape=jax.ShapeDtypeStruct((b, s, NSA_HEADS * HEAD_DIM), MXU_DTYPE),
        scratch_shapes=[
            pltpu.VMEM((2 * LANES, nc), MXU_DTYPE),
            pltpu.VMEM((ROWS, 2 * LANES), MXU_DTYPE),
            pltpu.VMEM((ROWS, LANES), F32),
            pltpu.VMEM((ROWS, LANES), F32),
            pltpu.VMEM((ROWS, 2 * TQ), F32),
            pltpu.VMEM((ROWS, 2 * TQ), F32),
            pltpu.VMEM((ROWS, LANES), F32),
        ],
        compiler_params=_params("parallel", "parallel", "arbitrary"),
        name="nsa_attention",
    )(q8, ks, vs, kw, vw, kc, vc, gl, ac, bm, agg)


def _conv_kernel(cu_ref, halo_ref, w_ref, b_ref, g_ref, bb_ref, o_ref, hh_scr, sh_scr):
    ts = cu_ref.shape[0]
    ch = CONV_CHANNELS

    def glu(u):
        return u[:, :ch] * _sigmoid(u[:, ch:])

    first = pl.program_id(1) == 0
    hh_scr[0:CONV_HALO, :] = jnp.where(first, 0.0, glu(halo_ref[...]))
    hh_scr[CONV_HALO:CONV_HALO + ts, :] = glu(cu_ref[...])
    hh_scr[CONV_HALO + ts:, :] = jnp.zeros((SUBLANES, ch), F32)
    acc = jnp.broadcast_to(b_ref[...], (ts, ch))
    lead = CONV_HALO - (CONV_WIDTH - 1)
    for shift in range(SUBLANES):
        sh_scr[...] = hh_scr[pl.ds(shift, CONV_HALO + ts), :]
        for start in range(shift, lead + CONV_WIDTH, SUBLANES):
            w = start - lead
            if 0 <= w < CONV_WIDTH:
                acc = acc + sh_scr[pl.ds(start - shift, ts), :] * w_ref[w:w + 1, :]
    mu = jnp.mean(acc, axis=-1, keepdims=True)
    xc = acc - mu
    y = xc * lax.rsqrt(jnp.mean(xc * xc, axis=-1, keepdims=True) + EPS) * g_ref[...] + bb_ref[...]
    o_ref[...] = _silu(y).astype(o_ref.dtype)


def _conv(cu, w, bias, ln_g, ln_b):
    b, s, _ = cu.shape
    ts = min(TS_CONV, s)
    per = ts // CONV_HALO
    return pl.pallas_call(
        _conv_kernel,
        grid=(b, s // ts),
        in_specs=[
            pl.BlockSpec((None, ts, C_CU), lambda bi, i: (bi, i, 0)),
            pl.BlockSpec((None, CONV_HALO, C_CU), lambda bi, i: (bi, jnp.maximum(i * per - 1, 0), 0)),
            _full((CONV_HALO, CONV_CHANNELS)), _full((1, CONV_CHANNELS)),
            _full((1, CONV_CHANNELS)), _full((1, CONV_CHANNELS)),
        ],
        out_specs=pl.BlockSpec((None, ts, CONV_CHANNELS), lambda bi, i: (bi, i, 0)),
        out_shape=jax.ShapeDtypeStruct((b, s, CONV_CHANNELS), MXU_DTYPE),
        scratch_shapes=[pltpu.VMEM((CONV_HALO + ts + SUBLANES, CONV_CHANNELS), F32),
                        pltpu.VMEM((CONV_HALO + ts, CONV_CHANNELS), F32)],
        compiler_params=_params("parallel", "parallel"),
        name="conformer_conv",
    )(cu, cu, w, bias, ln_g, ln_b)


def _gla_kernel(q_ref, k_ref, v_ref, r_ref, ga_ref, wa_ref, ba_ref, g_ref, tri_ref, hm_ref, vm_ref, bmk_ref,
                gm_ref, o_ref, st_scr, sb_scr, o_scr, u_scr):
    ts = q_ref.shape[0]
    nch = ts // GLA_CHUNK
    dk = GLA_KEY_DIM // GLA_HEADS
    cs = GLA_CHUNK

    @pl.when(pl.program_id(1) == 0)
    def _():
        st_scr[...] = jnp.zeros(st_scr.shape, F32)

    x = _dot(ga_ref[...].astype(MXU_DTYPE), wa_ref[...]) + ba_ref[...]
    log_a = (jnp.minimum(x, 0.0) - jnp.log(1.0 + jnp.exp(-jnp.abs(x)))) * (1.0 / GLA_TAU)
    wide = jnp.concatenate([log_a[n * cs:(n + 1) * cs] for n in range(nch)], axis=1)
    l_hi, l_mid, l_lo = _split3(wide)
    tri = tri_ref[...]
    b_wide = _dot(tri, l_hi) + _dot(tri, l_mid) + _dot(tri, l_lo)
    b = jnp.concatenate([b_wide[:, n * GLA_KEY_DIM:(n + 1) * GLA_KEY_DIM] for n in range(nch)], axis=0)
    eb = jnp.exp(b)
    q_t = (q_ref[...] * (dk ** -0.5) * eb).astype(MXU_DTYPE)
    k_all = k_ref[...]
    k_t = (k_all * jnp.exp(-b)).astype(MXU_DTYPE)
    v_all = v_ref[...].astype(MXU_DTYPE)
    causal = tri_ref[...] > 0
    blockmask = bmk_ref[...]

    for n in range(nch):
        r = slice(n * cs, (n + 1) * cs)
        b_last = b[(n + 1) * cs - 1:(n + 1) * cs, :]
        k_d = (k_all[r] * jnp.exp(b_last - b[r])).astype(MXU_DTYPE)
        u_scr[n] = _dot_tn(v_all[r], k_d) * blockmask
    state = st_scr[...]
    for n in range(nch):
        sb_scr[n] = state.astype(MXU_DTYPE)
        b_last = b[(n + 1) * cs - 1:(n + 1) * cs, :]
        state = state * jnp.exp(b_last) + u_scr[n]
    st_scr[...] = state

    causal_h = jnp.concatenate([causal] * GLA_HEADS, axis=0)
    for n in range(nch):
        r = slice(n * cs, (n + 1) * cs)
        qn = q_t[r]
        o = _dot_nt(qn, sb_scr[n])
        q_heads = jnp.concatenate([qn * hm_ref[h:h + 1, :] for h in range(GLA_HEADS)], axis=0)
        attn = jnp.where(causal_h, _dot_nt(q_heads, k_t[r]), 0.0).astype(MXU_DTYPE)
        pv = _dot(attn, v_all[r])
        for h in range(GLA_HEADS):
            o = o + pv[h * cs:(h + 1) * cs] * vm_ref[h:h + 1, :]
        o_scr[r, :] = o
    o = o_scr[...]
    o_hi, o_mid, o_lo = _split3(o * o)
    gm = gm_ref[...]
    ms = _dot(o_hi, gm) + _dot(o_mid, gm) + _dot(o_lo, gm)
    y = o * lax.rsqrt(ms + EPS) * g_ref[...]
    o_ref[...] = (y * _silu(r_ref[...])).astype(o_ref.dtype)


def _gla(gla, misc, wa_pad, ba, g):
    b, s, _ = gla.shape
    ts = min(TS_GLA, s)
    nch = ts // GLA_CHUNK
    dk, dv = GLA_KEY_DIM // GLA_HEADS, GLA_VALUE_DIM // GLA_HEADS
    tri = np.tril(np.ones((GLA_CHUNK, GLA_CHUNK), np.float32))
    hm = (np.arange(GLA_KEY_DIM)[None, :] // dk == np.arange(GLA_HEADS)[:, None]).astype(np.float32)
    vm = (np.arange(GLA_VALUE_DIM)[None, :] // dv == np.arange(GLA_HEADS)[:, None]).astype(np.float32)
    bmk = (np.arange(GLA_VALUE_DIM)[:, None] // dv == np.arange(GLA_KEY_DIM)[None, :] // dk).astype(np.float32)
    gm = (np.arange(GLA_VALUE_DIM)[:, None] // dv == np.arange(GLA_VALUE_DIM)[None, :] // dv).astype(np.float32) / dv
    return pl.pallas_call(
        _gla_kernel,
        grid=(b, s // ts),
        in_specs=[
            pl.BlockSpec((None, ts, GLA_KEY_DIM), lambda bi, i: (bi, i, 0)),
            pl.BlockSpec((None, ts, GLA_KEY_DIM), lambda bi, i: (bi, i, 1)),
            pl.BlockSpec((None, ts, GLA_VALUE_DIM), lambda bi, i: (bi, i, 1)),
            pl.BlockSpec((None, ts, GLA_VALUE_DIM), lambda bi, i: (bi, i, 2)),
            pl.BlockSpec((None, None, ts, LANES), lambda bi, i: (0, bi, i, 0)),
            _full((LANES, GLA_KEY_DIM)), _full((1, GLA_KEY_DIM)), _full((1, GLA_VALUE_DIM)),
            _full(tri.shape), _full(hm.shape), _full(vm.shape), _full(bmk.shape), _full(gm.shape),
        ],
        out_specs=pl.BlockSpec((None, ts, GLA_VALUE_DIM), lambda bi, i: (bi, i, 0)),
        out_shape=jax.ShapeDtypeStruct((b, s, GLA_VALUE_DIM), MXU_DTYPE),
        scratch_shapes=[
            pltpu.VMEM((GLA_VALUE_DIM, GLA_KEY_DIM), F32),
            pltpu.VMEM((nch, GLA_VALUE_DIM, GLA_KEY_DIM), MXU_DTYPE),
            pltpu.VMEM((ts, GLA_VALUE_DIM), F32),
            pltpu.VMEM((nch, GLA_VALUE_DIM, GLA_KEY_DIM), F32),
        ],
        compiler_params=_params("parallel", "arbitrary"),
        name="gla",
    )(gla, gla, gla, gla, misc, wa_pad, ba, g,
      jnp.asarray(tri, MXU_DTYPE), jnp.asarray(hm, MXU_DTYPE), jnp.asarray(vm), jnp.asarray(bmk),
      jnp.asarray(gm, MXU_DTYPE))


def _post_kernel(x_ref, yn_ref, yc_ref, yg_ref, p_ref, wo_ref, gf_ref, wg_ref, wu_ref, wd_ref,
                 gp_ref, wpg_ref, wpp_ref, gfin_ref, o_ref, *, final):
    dn = yn_ref.shape[1]
    dc = yc_ref.shape[1]
    mix = _dot(yn_ref[...], wo_ref[0:dn, :]) + _dot(yc_ref[...], wo_ref[dn:dn + dc, :]) \
        + _dot(yg_ref[...], wo_ref[dn + dc:, :])
    x = x_ref[...] + mix
    h = _rms(x, gf_ref[...]).astype(MXU_DTYPE)
    dff = wg_ref.shape[1]
    step = dff // FF_CHUNKS
    ff = None
    for j in range(FF_CHUNKS):
        cols = slice(j * step, (j + 1) * step)
        act = _silu(_dot(h, wg_ref[:, cols])) * _dot(h, wu_ref[:, cols])
        down = _dot(act.astype(MXU_DTYPE), wd_ref[cols, :])
        ff = down if ff is None else ff + down
    x = x + ff
    gate = _sigmoid(_dot(_rms(x, gp_ref[...]).astype(MXU_DTYPE), wpg_ref[...]))
    x = x + _dot(p_ref[...].astype(MXU_DTYPE), wpp_ref[...]) * gate
    if final:
        x = _rms(x, gfin_ref[...])
    o_ref[...] = x


def _post(x2d, yn, yc, yg, p_all, layer, wo, gf, wg, wu, wd, gp, wpg, wpp, gfin, final):
    t, d = x2d.shape
    tm = min(TM_PROJ, t)
    row = lambda i: (i, 0)
    once = dict(pipeline_mode=pl.Buffered(1))

    def const(a):
        return pl.BlockSpec(a.shape, lambda i: (0,) * a.ndim, **once)

    return pl.pallas_call(
        functools.partial(_post_kernel, final=final),
        grid=(t // tm,),
        in_specs=[
            pl.BlockSpec((tm, d), row),
            pl.BlockSpec((tm, yn.shape[1]), row), pl.BlockSpec((tm, yc.shape[1]), row),
            pl.BlockSpec((tm, yg.shape[1]), row),
            pl.BlockSpec((None, tm, p_all.shape[2]), lambda i: (layer, i, 0)),
            const(wo), const(gf), const(wg), const(wu), const(wd), const(gp), const(wpg), const(wpp),
            const(gfin),
        ],
        out_specs=pl.BlockSpec((tm, d), row),
        out_shape=jax.ShapeDtypeStruct((t, d), F32),
        compiler_params=_params("parallel"),
        name="post_mixer",
    )(x2d, yn, yc, yg, p_all, wo, gf, wg, wu, wd, gp, wpg, wpp, gfin)


def _layer(x2d, p_all, layer, bsz, seq, rel_tabs, agg, lw, final_g, final):
    (mix_g, w_in, w_out, pos_kv, w1_kv, b1_kv, w2_kv, conv_w, conv_b, conv_ln_g, conv_ln_b,
     w_alpha, b_alpha, gla_g, ffn_g, w_gate, w_up, w_down, ple_g, w_pg, w_pp, qconst) = lw
    bm, ac = rel_tabs
    t = bsz * seq
    hk = NSA_KV_HEADS

    w_all = _gather_columns(w_in, _W_RUNS).astype(MXU_DTYPE)
    wk_t = _gather_columns(w_in, _WK_RUNS).T.astype(MXU_DTYPE)
    q8, ks, kw, vs, vw, kvc, cu, gla, misc = _in_proj(x2d, mix_g[None, :], w_all, wk_t, qconst, bsz, seq)

    nsub = seq // CMP_STRIDE
    cmp = _compress(kvc.reshape(bsz, seq, C_KVC), pos_kv, w1_kv, b1_kv, w2_kv)
    cmp = cmp.reshape(2, bsz, nsub, hk, HEAD_DIM)
    ones_row = jnp.ones((bsz, hk, 2, nsub), MXU_DTYPE)
    kc = jnp.concatenate([cmp[0].transpose(0, 2, 3, 1), ones_row,
                          jnp.zeros((bsz, hk, LANES - HEAD_DIM - 2, nsub), MXU_DTYPE)], axis=2)
    ones_col = jnp.ones((bsz, hk, nsub, 1), MXU_DTYPE)
    vc = jnp.concatenate([cmp[1].transpose(0, 2, 1, 3), ones_col,
                          jnp.zeros((bsz, hk, nsub, LANES - HEAD_DIM - 1), MXU_DTYPE)], axis=3)

    y_nsa = _nsa(q8.reshape(NSA_HEADS, bsz, seq, LANES), ks, vs, kw, vw, kc, vc,
                 misc.reshape(hk, bsz, seq, LANES), ac, bm, agg)

    y_conv = _conv(cu.reshape(bsz, seq, C_CU), conv_w, conv_b[None, :], conv_ln_g[None, :], conv_ln_b[None, :])
    y_gla = _gla(gla.reshape(bsz, seq, C_GLA), misc.reshape(hk, bsz, seq, LANES), w_alpha, b_alpha[None, :],
                 gla_g[None, :])

    return _post(x2d, y_nsa.reshape(t, -1), y_conv.reshape(t, -1), y_gla.reshape(t, -1), p_all, layer,
                 w_out.astype(MXU_DTYPE), ffn_g[None, :], w_gate.astype(MXU_DTYPE), w_up.astype(MXU_DTYPE),
                 w_down.astype(MXU_DTYPE), ple_g[None, :], w_pg.astype(MXU_DTYPE), w_pp.astype(MXU_DTYPE),
                 final_g[None, :], final)


def kernel(x, p, rel_bias, mix_norm_g, w_in, w_out, cmp_pos_k, cmp_w1_k, cmp_b1_k, cmp_w2_k, cmp_pos_v, cmp_w1_v, cmp_b1_v, cmp_w2_v, conv_w, conv_b, conv_ln_g, conv_ln_b, gla_w_alpha, gla_b_alpha, gla_norm_g, ffn_norm_g, ffn_w_gate, ffn_w_up, ffn_w_down, ple_norm_g, ple_w_gate, ple_w_proj, final_norm_g):
    bsz, seq, d = x.shape
    depth = w_in.shape[0]
    assert seq % TQ == 0 and seq // SLC_BLOCK <= LANES
    t = bsz * seq
    rel_tabs = _nsa_bias_tables(rel_bias.astype(F32))
    nbp = LANES
    agg = jnp.asarray(_agg_matrix(seq // CMP_STRIDE, nbp), MXU_DTYPE)

    far = rel_bias[REL_BUCKETS - 1].astype(F32)
    f_hi, f_lo = _split_hi_lo(far)
    qconst = jnp.zeros((NSA_HEADS + 1, LANES), F32)
    qconst = qconst.at[:NSA_HEADS, FAR_LANE].set(f_hi.astype(F32)).at[:NSA_HEADS, FAR_LANE + 1].set(f_lo.astype(F32))
    qconst = qconst.at[NSA_HEADS, HEAD_DIM].set(1.0)

    x2d = x.reshape(t, d)
    for i in range(depth):
        wa_pad = jnp.zeros((LANES, GLA_KEY_DIM), F32).at[GA_LANE0:GA_LANE0 + GLA_GATE_RANK].set(gla_w_alpha[i])
        conv_w_pad = jnp.zeros((CONV_HALO, CONV_CHANNELS), F32).at[:CONV_WIDTH].set(conv_w[i])
        both = lambda a: jnp.concatenate([a] * NSA_KV_HEADS, axis=-1)
        w1_blocks = lambda w: _block_diag2(w.reshape(CMP_BLOCK, HEAD_DIM, CMP_HIDDEN))
        lw = (mix_norm_g[i], w_in[i], w_out[i],
              jnp.stack([both(cmp_pos_k[i]), both(cmp_pos_v[i])]),
              jnp.stack([w1_blocks(cmp_w1_k[i]), w1_blocks(cmp_w1_v[i])]).astype(MXU_DTYPE),
              jnp.stack([both(cmp_b1_k[i])[None, :], both(cmp_b1_v[i])[None, :]]),
              jnp.stack([_block_diag2(cmp_w2_k[i]), _block_diag2(cmp_w2_v[i])]).astype(MXU_DTYPE),
              conv_w_pad, conv_b[i], conv_ln_g[i], conv_ln_b[i],
              wa_pad.astype(MXU_DTYPE), gla_b_alpha[i], gla_norm_g[i],
              ffn_norm_g[i], ffn_w_gate[i], ffn_w_up[i], ffn_w_down[i],
              ple_norm_g[i], ple_w_gate[i], ple_w_proj[i], qconst)
        x2d = _layer(x2d, p.reshape(depth, t, -1), i, bsz, seq, rel_tabs, agg, lw, final_norm_g, i == depth - 1)
    return x2d.reshape(bsz, seq, d)
```
